```python
import math
import jax
import jax.numpy as jnp
from jax import lax
import numpy as np

D_MODEL = 2048
BATCH = 4
SEQ = 4096
DEPTH = 1

GRID_W = 64
CTX_LEN = 256

DIFF_HEADS = 8
DIFF_HALF = 64
DIFF_VDIM = 2 * DIFF_HALF
DIFF_W = DIFF_HEADS * DIFF_VDIM
MLSTM_HEADS = 8
MLSTM_DHEAD = 128
ML_W = MLSTM_HEADS * MLSTM_DHEAD
MLSTM_CONV = 5
MLSTM_CHUNK = 64
N_GATES = 4
MIX_W = DIFF_W + ML_W
IN_SPLITS = (DIFF_W, 2 * DIFF_W, 3 * DIFF_W, 3 * DIFF_W + 2 * ML_W,
             3 * DIFF_W + 3 * ML_W, 3 * DIFF_W + 4 * ML_W)
IN_COLS = 3 * DIFF_W + 4 * ML_W + N_GATES * MLSTM_HEADS
Q_BLOCK = 128
ROPE_BASE = 10000.0
PEER_HEADS = 8
PEER_NKEYS = 128
PEER_EXPERTS = PEER_NKEYS * PEER_NKEYS
PEER_DQ = 256
PEER_TOPK = 16
PEER_TOKEN_BLOCK = 128
DEEPNORM_ALPHA = (2.0 * DEPTH) ** 0.25
DEEPNORM_BETA = (8.0 * DEPTH) ** -0.25
LN_EPS = 1e-5
RMS_EPS = 1e-6

kernel_name = 'hybrid_diffattn_mlstm_peer_dit'


def layer_norm(h, g, b):
    hf = h.astype(jnp.float32)
    mu = jnp.mean(hf, -1, keepdims=True)
    var = jnp.mean(jnp.square(hf - mu), -1, keepdims=True)
    return ((hf - mu) * lax.rsqrt(var + LN_EPS) * g + b).astype(h.dtype)


def head_rms_norm(a, g):
    af = a.astype(jnp.float32)
    return af * lax.rsqrt(jnp.mean(jnp.square(af), -1, keepdims=True) + RMS_EPS) * g


def modulate(h, shift, scale):
    return h * (1.0 + scale) + shift


def to_heads(a, n_heads):
    b, t, _ = a.shape
    return a.reshape(b, t, n_heads, -1).transpose(0, 2, 1, 3)


def merge_heads(a):
    b, h, t, d = a.shape
    return a.transpose(0, 2, 1, 3).reshape(b, t, h * d)


def rope_1d(a, pos):
    nf = a.shape[-1] // 2
    inv = ROPE_BASE ** (-jnp.arange(nf, dtype=jnp.float32) / nf)
    ang = pos.astype(jnp.float32)[:, None] * inv
    cos = jnp.cos(ang).astype(a.dtype)
    sin = jnp.sin(ang).astype(a.dtype)
    a1, a2 = a[..., :nf], a[..., nf:]
    return jnp.concatenate([a1 * cos - a2 * sin, a1 * sin + a2 * cos], -1)


def axial_rope(a, row, col):
    half = a.shape[-1] // 2
    return jnp.concatenate([rope_1d(a[..., :half], row), rope_1d(a[..., half:], col)], -1)


def centred_dwconv(a, w, b):
    out = lax.conv_general_dilated(a, w[:, None, :], window_strides=(1,), padding='SAME',
                                   dimension_numbers=('NWC', 'WIO', 'NWC'),
                                   feature_group_count=a.shape[-1])
    return out + b


def in_projections(h, lp):
    b, t, _ = h.shape
    proj = h @ lp['w_in']
    dq, dk, dv, mqk, mv, mo, mg = jnp.split(proj, IN_SPLITS, axis=-1)
    mqk = jax.nn.silu(centred_dwconv(mqk, lp['conv_w'], lp['conv_b']))
    mq, mk = jnp.split(mqk, 2, axis=-1)
    gates = mg.reshape(b, t, N_GATES, MLSTM_HEADS).astype(jnp.float32) + lp['gate_b'].astype(jnp.float32)
    i_f, f_f, i_b, f_b = gates.transpose(2, 0, 3, 1)
    diff = (to_heads(dq, DIFF_HEADS), to_heads(dk, DIFF_HEADS), to_heads(dv, DIFF_HEADS))
    ml = (to_heads(mq, MLSTM_HEADS), to_heads(mk, MLSTM_HEADS) * MLSTM_DHEAD ** -0.5,
          to_heads(mv, MLSTM_HEADS), to_heads(mo, MLSTM_HEADS),
          i_f, jax.nn.log_sigmoid(f_f), i_b, jax.nn.log_sigmoid(f_b))
    return diff, ml


def diff_attention(q1, q2, k1, k2, v, lam):
    b, h, t, d = q1.shape
    nb = t // Q_BLOCK
    scale = d ** -0.5

    def blocks(a):
        return jnp.moveaxis(a.reshape(b, h, nb, Q_BLOCK, d), 2, 0)

    def one_block(qs):
        a1, a2 = qs
        s1 = jnp.einsum('bhqd,bhkd->bhqk', a1, k1).astype(jnp.float32) * scale
        s2 = jnp.einsum('bhqd,bhkd->bhqk', a2, k2).astype(jnp.float32) * scale
        w = jax.nn.softmax(s1, axis=-1) - lam * jax.nn.softmax(s2, axis=-1)
        return jnp.einsum('bhqk,bhkd->bhqd', w.astype(v.dtype), v)

    o = lax.map(one_block, (blocks(q1), blocks(q2)))
    return jnp.moveaxis(o, 0, 2).reshape(b, h, t, v.shape[-1])


def mlstm_chunkwise(q, k, v, i_pre, logf, state):
    b, h, t, d = q.shape
    nc = t // MLSTM_CHUNK
    lc = MLSTM_CHUNK

    def to_chunks(a):
        return jnp.moveaxis(a.reshape(b, h, nc, lc, *a.shape[3:]), 2, 0)

    tri = jnp.tril(jnp.ones((lc, lc), dtype=bool))

    def step(carry, inp):
        c_st, n_st, m_st = carry
        qc, kc, vc, ic, fc = inp
        bcum = jnp.cumsum(fc, axis=-1)
        log_d = bcum[..., :, None] - bcum[..., None, :] + ic[..., None, :]
        log_d = jnp.where(tri, log_d, -jnp.inf)
        inter = bcum + m_st[..., None]
        m_row = jnp.maximum(inter, jnp.max(log_d, axis=-1))
        dmat = jnp.exp(log_d - m_row[..., None])
        g_in = jnp.exp(inter - m_row)
        s = jnp.einsum('bhld,bhsd->bhls', qc, kc) * dmat
        num = jnp.einsum('bhls,bhsd->bhld', s, vc) + g_in[..., None] * jnp.einsum('bhld,bhde->bhle', qc, c_st)
        den = jnp.sum(s, axis=-1) + g_in * jnp.einsum('bhld,bhd->bhl', qc, n_st)
        h_out = num / jnp.maximum(jnp.abs(den), jnp.exp(-m_row))[..., None]
        b_last = bcum[..., -1]
        wlog = b_last[..., None] - bcum + ic
        m_new = jnp.maximum(b_last + m_st, jnp.max(wlog, axis=-1))
        wk = jnp.exp(wlog - m_new[..., None])
        decay = jnp.exp(b_last + m_st - m_new)
        c_new = decay[..., None, None] * c_st + jnp.einsum('bhl,bhld,bhle->bhde', wk, kc, vc)
        n_new = decay[..., None] * n_st + jnp.einsum('bhl,bhld->bhd', wk, kc)
        return (c_new, n_new, m_new), h_out

    state, hs = lax.scan(step, state, (to_chunks(q), to_chunks(k), to_chunks(v),
                                       to_chunks(i_pre), to_chunks(logf)))
    return jnp.moveaxis(hs, 0, 2).reshape(b, h, t, v.shape[-1]), state


def mlstm_bidir(q, k, v, i_f, lf_f, i_b, lf_b, state_f, state_b):
    h_f, st_f = mlstm_chunkwise(q, k, v, i_f, lf_f, state_f)
    flip = lambda a: jnp.flip(a, axis=2)
    h_b, st_b = mlstm_chunkwise(flip(q), flip(k), flip(v), flip(i_b), flip(lf_b), state_b)
    return h_f + flip(h_b), st_f, st_b


def zero_mlstm_state(b):
    return (jnp.zeros((b, MLSTM_HEADS, MLSTM_DHEAD, MLSTM_DHEAD), jnp.float32),
            jnp.zeros((b, MLSTM_HEADS, MLSTM_DHEAD), jnp.float32),
            jnp.zeros((b, MLSTM_HEADS), jnp.float32))


def diff_group_out(o, lp, lam_init, dtype):
    return merge_heads(head_rms_norm(o, lp['diff_norm_g']) * (1.0 - lam_init)).astype(dtype)


def mlstm_group_out(hm, mo, lp, dtype):
    out = head_rms_norm(hm, lp['mlstm_norm_g']) * jax.nn.sigmoid(mo.astype(jnp.float32))
    return merge_heads(out).astype(dtype)


def token_mixer(h, hc, lp, row, col, lam_init, need_ctx_out):
    (q, k, v), (mq, mk, mv, mo, i_f, lf_f, i_b, lf_b) = in_projections(h, lp)
    (qc, kc, vc), (mqc, mkc, mvc, moc, ic_f, lfc_f, ic_b, lfc_b) = in_projections(hc, lp)
    lq1, lk1, lq2, lk2 = lp['diff_lambda'].astype(jnp.float32)
    lam = jnp.exp(jnp.dot(lq1, lk1)) - jnp.exp(jnp.dot(lq2, lk2)) + lam_init
    q1 = axial_rope(q[..., :DIFF_HALF], row, col)
    q2 = axial_rope(q[..., DIFF_HALF:], row, col)
    k1 = axial_rope(k[..., :DIFF_HALF], row, col)
    k2 = axial_rope(k[..., DIFF_HALF:], row, col)
    kc1, kc2 = kc[..., :DIFF_HALF], kc[..., DIFF_HALF:]
    keys1 = jnp.concatenate([kc1, k1], axis=2)
    keys2 = jnp.concatenate([kc2, k2], axis=2)
    vals = jnp.concatenate([vc, v], axis=2)
    d_lat = diff_group_out(diff_attention(q1, q2, keys1, keys2, vals, lam), lp, lam_init, h.dtype)
    b = h.shape[0]
    hm_c, st_f, st_b = mlstm_bidir(mqc, mkc, mvc, ic_f, lfc_f, ic_b, lfc_b,
                                   zero_mlstm_state(b), zero_mlstm_state(b))
    hm, _, _ = mlstm_bidir(mq, mk, mv, i_f, lf_f, i_b, lf_b, st_f, st_b)
    m_lat = mlstm_group_out(hm, mo, lp, h.dtype)
    y = jnp.concatenate([d_lat, m_lat], axis=-1) @ lp['w_out']
    yc = None
    if need_ctx_out:
        qc1, qc2 = qc[..., :DIFF_HALF], qc[..., DIFF_HALF:]
        d_ctx = diff_group_out(diff_attention(qc1, qc2, kc1, kc2, vc, lam), lp, lam_init, hc.dtype)
        m_ctx = mlstm_group_out(hm_c, moc, lp, hc.dtype)
        yc = jnp.concatenate([d_ctx, m_ctx], axis=-1) @ lp['w_out']
    return y, yc


def peer_ffn(h, lp):
    b, t, d = h.shape
    half = PEER_DQ // 2
    q = (h @ lp['peer_wq']).reshape(b, t, PEER_HEADS, PEER_DQ)
    keys = lp['peer_keys']
    s1 = jnp.einsum('bthd,hkd->bthk', q[..., :half], keys[0]).astype(jnp.float32)
    s2 = jnp.einsum('bthd,hkd->bthk', q[..., half:], keys[1]).astype(jnp.float32)
    v1, i1 = lax.top_k(s1, PEER_TOPK)
    v2, i2 = lax.top_k(s2, PEER_TOPK)
    cand = (v1[..., :, None] + v2[..., None, :]).reshape(b, t, PEER_HEADS, PEER_TOPK * PEER_TOPK)
    cidx = (i1[..., :, None] * PEER_NKEYS + i2[..., None, :]).reshape(b, t, PEER_HEADS, PEER_TOPK * PEER_TOPK)
    top_s, pos = lax.top_k(cand, PEER_TOPK)
    eidx = jnp.take_along_axis(cidx, pos, axis=-1)
    gates = jax.nn.softmax(top_s, axis=-1)
    n_e = PEER_HEADS * PEER_TOPK
    nb = (b * t) // PEER_TOKEN_BLOCK
    xs = h.reshape(nb, PEER_TOKEN_BLOCK, d)
    es = eidx.reshape(nb, PEER_TOKEN_BLOCK, n_e)
    gs = gates.reshape(nb, PEER_TOKEN_BLOCK, n_e)
    u_tab, v_tab = lp['peer_u'], lp['peer_v']

    def one_block(args):
        xb, eb, gb = args
        act = jnp.einsum('td,ted->te', xb, u_tab[eb]).astype(jnp.float32)
        act = jax.nn.gelu(act, approximate=False) * gb
        return jnp.einsum('te,ted->td', act.astype(xb.dtype), v_tab[eb])

    return lax.map(one_block, (xs, es, gs)).reshape(b, t, d)


def trunk_layer(x, xc, c, c_ctx, lp, row, col, lam_init, need_ctx_out):
    sh1, sc1, g1, sh2, sc2, g2 = jnp.split((jax.nn.silu(c) @ lp['w_ada'] + lp['b_ada'])[:, None, :], 6, axis=-1)
    csh1, csc1, cg1, csh2, csc2, cg2 = jnp.split(jax.nn.silu(c_ctx) @ lp['w_ada'] + lp['b_ada'], 6, axis=-1)
    y, yc = token_mixer(modulate(x, sh1, sc1), modulate(xc, csh1, csc1), lp, row, col, lam_init, need_ctx_out)
    x = layer_norm(DEEPNORM_ALPHA * x + g1 * y, lp['ln1_g'], lp['ln1_b'])
    x = layer_norm(DEEPNORM_ALPHA * x + g2 * peer_ffn(modulate(x, sh2, sc2), lp), lp['ln2_g'], lp['ln2_b'])
    if need_ctx_out:
        xc = layer_norm(DEEPNORM_ALPHA * xc + cg1 * yc, lp['ln1_g'], lp['ln1_b'])
        xc = layer_norm(DEEPNORM_ALPHA * xc + cg2 * peer_ffn(modulate(xc, csh2, csc2), lp), lp['ln2_g'], lp['ln2_b'])
    return x, xc


def setup_inputs(seed: int = 0) -> dict:
    key = jax.random.key(seed)
    ks = jax.random.split(key, 26)
    f32 = jnp.float32
    D, L, H = D_MODEL, DEPTH, MLSTM_HEADS

    def nrm(k, shape, s):
        return s * jax.random.normal(k, shape, f32)

    fgate_bias = jnp.linspace(3.0, 6.0, H, dtype=f32)
    gate_b = jnp.stack([nrm(ks[10], (L, H), 0.1),
                        fgate_bias + nrm(ks[11], (L, H), 0.1),
                        nrm(ks[12], (L, H), 0.1),
                        fgate_bias + nrm(ks[13], (L, H), 0.1)], axis=1)
    return {
        'x': nrm(ks[0], (BATCH, SEQ, D), 1.0),
        'c': nrm(ks[1], (BATCH, D), 1.0),
        'ctx': nrm(ks[2], (BATCH, CTX_LEN, D), 1.0),
        'c_ctx': nrm(ks[3], (D,), 1.0),
        'w_ada': nrm(ks[4], (L, D, 6 * D), 0.5 * D ** -0.5),
        'b_ada': nrm(ks[5], (L, 6 * D), 0.02),
        'w_in': nrm(ks[6], (L, D, IN_COLS), D ** -0.5),
        'conv_w': nrm(ks[7], (L, MLSTM_CONV, 2 * ML_W), MLSTM_CONV ** -0.5),
        'conv_b': nrm(ks[8], (L, 2 * ML_W), 0.02),
        'gate_b': gate_b,
        'diff_lambda': nrm(ks[9], (L, 4, DIFF_HALF), 0.1),
        'diff_norm_g': 1.0 + nrm(ks[14], (L, DIFF_VDIM), 0.02),
        'mlstm_norm_g': 1.0 + nrm(ks[15], (L, MLSTM_DHEAD), 0.02),
        'w_out': nrm(ks[16], (L, MIX_W, D), DEEPNORM_BETA * MIX_W ** -0.5),
        'ln1_g': 1.0 + nrm(ks[17], (L, D), 0.02),
        'ln1_b': nrm(ks[18], (L, D), 0.02),
        'ln2_g': 1.0 + nrm(ks[19], (L, D), 0.02),
        'ln2_b': nrm(ks[20], (L, D), 0.02),
        'peer_wq': nrm(ks[21], (L, D, PEER_HEADS * PEER_DQ), D ** -0.5),
        'peer_keys': nrm(ks[22], (L, 2, PEER_HEADS, PEER_NKEYS, PEER_DQ // 2), (PEER_DQ // 2) ** -0.5),
        'peer_u': nrm(ks[23], (L, PEER_EXPERTS, D), D ** -0.5),
        'peer_v': nrm(ks[24], (L, PEER_EXPERTS, D), DEEPNORM_BETA * (PEER_HEADS * PEER_TOPK) ** -0.5),
    }


def reference(x, c, ctx, c_ctx, w_ada, b_ada, w_in, conv_w, conv_b, gate_b, diff_lambda,
              diff_norm_g, mlstm_norm_g, w_out, ln1_g, ln1_b, ln2_g, ln2_b,
              peer_wq, peer_keys, peer_u, peer_v):
    n_lat = x.shape[1]
    rows = n_lat // GRID_W
    row = jnp.repeat(jnp.arange(rows, dtype=jnp.int32), GRID_W)
    col = jnp.tile(jnp.arange(GRID_W, dtype=jnp.int32), rows)
    xc = ctx
    for l in range(DEPTH):
        lp = {'w_ada': w_ada[l], 'b_ada': b_ada[l], 'w_in': w_in[l], 'conv_w': conv_w[l],
              'conv_b': conv_b[l], 'gate_b': gate_b[l], 'diff_lambda': diff_lambda[l],
              'diff_norm_g': diff_norm_g[l], 'mlstm_norm_g': mlstm_norm_g[l], 'w_out': w_out[l],
              'ln1_g': ln1_g[l], 'ln1_b': ln1_b[l], 'ln2_g': ln2_g[l], 'ln2_b': ln2_b[l],
              'peer_wq': peer_wq[l], 'peer_keys': peer_keys[l], 'peer_u': peer_u[l], 'peer_v': peer_v[l]}
        lam_init = 0.8 - 0.6 * math.exp(-0.3 * l)
        x, xc = trunk_layer(x, xc, c, c_ctx, lp, row, col, lam_init, l < DEPTH - 1)
    return x
```

```python
import functools
import math

import jax
import jax.numpy as jnp
from jax import lax
from jax.experimental import pallas as pl
from jax.experimental.pallas import tpu as pltpu

F32 = jnp.float32
BF16 = jnp.bfloat16

LANES = 128
SUBLANES = 8
VMEM_LIMIT = 56 * 1024 * 1024

GRID_W = 64
DIFF_HEADS = 8
DIFF_HALF = 64
DIFF_W = DIFF_HEADS * 2 * DIFF_HALF
MLSTM_HEADS = 8
MLSTM_DHEAD = 128
ML_W = MLSTM_HEADS * MLSTM_DHEAD
MLSTM_CONV = 5
CHUNK = 64
N_GATES = 4
MAIN_COLS = 3 * DIFF_W + 4 * ML_W
ROPE_COLS = 2 * DIFF_W
ROPE_BASE = 10000.0
ROPE_NF = 16
PEER_HEADS = 8
PEER_NKEYS = 128
PEER_DQ = 256
PEER_TOPK = 16
DEPTH = 1
DEEPNORM_ALPHA = (2.0 * DEPTH) ** 0.25
LN_EPS = 1e-5
RMS_EPS = 1e-6
LAM_INIT = 0.8 - 0.6 * math.exp(-0.3 * 0)
NEG_INF = float("-inf")


def _cparams(sem):
    return pltpu.CompilerParams(dimension_semantics=sem, vmem_limit_bytes=VMEM_LIMIT)


def _dot(a, b):
    return jnp.dot(a, b, preferred_element_type=F32)


def _dot_nt(a, b):
    return lax.dot_general(a, b, (((1,), (1,)), ((), ())), preferred_element_type=F32)


def _split3(x):
    hi = x.astype(BF16)
    r1 = x - hi.astype(F32)
    mid = r1.astype(BF16)
    lo = (r1 - mid.astype(F32)).astype(BF16)
    return hi, mid, lo


def _adaln_kernel(cb_ref, w_ref, b_ref, o_ref, act_scr, *, n_rows):
    @pl.when(pl.program_id(0) == 0)
    def _():
        c = cb_ref[...]
        act_scr[...] = c * jax.nn.sigmoid(c)

    tn = w_ref.shape[1]
    o_ref[...] = jnp.zeros_like(o_ref)
    for cb in range(tn // LANES):
        w = w_ref[:, cb * LANES:(cb + 1) * LANES]
        for r in range(n_rows):
            s = jnp.sum(act_scr[r] * w, axis=0, keepdims=True)
            o_ref[r:r + 1, cb * LANES:(cb + 1) * LANES] = s + b_ref[:, cb * LANES:(cb + 1) * LANES]


def _adaln(c_rows, w_ada, b_ada):
    n_rows, k = c_rows.shape
    n = w_ada.shape[1]
    tn = 512
    cb = jnp.broadcast_to(c_rows[:, :, None], (n_rows, k, LANES))
    return pl.pallas_call(
        functools.partial(_adaln_kernel, n_rows=n_rows),
        grid=(n // tn,),
        in_specs=[pl.BlockSpec((n_rows, k, LANES), lambda j: (0, 0, 0)),
                  pl.BlockSpec((k, tn), lambda j: (0, j)),
                  pl.BlockSpec((1, tn), lambda j: (0, j))],
        out_specs=pl.BlockSpec((SUBLANES, tn), lambda j: (0, j)),
        out_shape=jax.ShapeDtypeStruct((SUBLANES, n), F32),
        scratch_shapes=[pltpu.VMEM((n_rows, k, LANES), F32)],
        compiler_params=_cparams(("arbitrary",)),
        name="adaln",
    )(cb, w_ada, b_ada.reshape(1, n))


def _inproj_kernel(x_ref, sh_ref, sc_ref, w_ref, wg_ref, cos_ref, sin_ref, o_ref, g_ref, h_scr,
                   *, n_rope_tiles):
    j = pl.program_id(2)

    @pl.when(j == 0)
    def _():
        h = x_ref[0] * (1.0 + sc_ref[0]) + sh_ref[0]
        hb = h.astype(BF16)
        h_scr[...] = hb
        hl = (h - hb.astype(F32)).astype(BF16)
        wgh = wg_ref[0]
        wgl = wg_ref[1]
        g_ref[0] = _dot(hb, wgh) + _dot(hl, wgh) + _dot(hb, wgl)

    acc = _dot(h_scr[...], w_ref[...])
    tn = acc.shape[1]

    @pl.when(j < n_rope_tiles)
    def _():
        cos = cos_ref[...]
        sin = sin_ref[...]
        lane = lax.broadcasted_iota(jnp.int32, cos.shape, 1)
        first = (lane % (2 * ROPE_NF)) < ROPE_NF
        for cb in range(tn // LANES):
            a = acc[:, cb * LANES:(cb + 1) * LANES]
            sw = jnp.where(first, pltpu.roll(a, LANES - ROPE_NF, 1), pltpu.roll(a, ROPE_NF, 1))
            o_ref[0, :, cb * LANES:(cb + 1) * LANES] = (a * cos + sw * sin).astype(BF16)

    @pl.when(j >= n_rope_tiles)
    def _():
        o_ref[0] = acc.astype(BF16)


def _in_proj(x, shift, scale, w_main, wg, cos_t, sin_t, *, tm):
    b, t, d = x.shape
    n = w_main.shape[1]
    tn = 512
    return pl.pallas_call(
        functools.partial(_inproj_kernel, n_rope_tiles=ROPE_COLS // tn),
        grid=(b, t // tm, n // tn),
        in_specs=[pl.BlockSpec((1, tm, d), lambda bi, i, j: (bi, i, 0)),
                  pl.BlockSpec((1, 1, d), lambda bi, i, j: (bi, 0, 0)),
                  pl.BlockSpec((1, 1, d), lambda bi, i, j: (bi, 0, 0)),
                  pl.BlockSpec((d, tn), lambda bi, i, j: (0, j)),
                  pl.BlockSpec((2, d, LANES), lambda bi, i, j: (0, 0, 0)),
                  pl.BlockSpec((tm, LANES), lambda bi, i, j: (i, 0)),
                  pl.BlockSpec((tm, LANES), lambda bi, i, j: (i, 0))],
        out_specs=[pl.BlockSpec((1, tm, tn), lambda bi, i, j: (bi, i, j)),
                   pl.BlockSpec((1, tm, LANES), lambda bi, i, j: (bi, i, 0))],
        out_shape=[jax.ShapeDtypeStruct((b, t, n), BF16),
                   jax.ShapeDtypeStruct((b, t, LANES), F32)],
        scratch_shapes=[pltpu.VMEM((tm, d), BF16)],
        compiler_params=_cparams(("arbitrary", "arbitrary", "arbitrary")),
        name="in_proj",
    )(x, shift, scale, w_main, wg, cos_t, sin_t)


def _attn_kernel(lam_ref, gn_ref, q_ref, kc_ref, vc_ref, kl_ref, vl_ref, o_ref,
                 q2_scr, m_scr, l_scr, acc_scr, *, tq):
    s = pl.program_id(3)
    ns = pl.num_programs(3)

    @pl.when(s == 0)
    def _():
        q = q_ref[0]
        lane = lax.broadcasted_iota(jnp.int32, q.shape, 1)
        zero = jnp.zeros_like(q)
        q2_scr[0:tq] = jnp.where(lane < DIFF_HALF, q, zero)
        q2_scr[tq:2 * tq] = jnp.where(lane >= DIFF_HALF, q, zero)
        m_scr[...] = jnp.full_like(m_scr, NEG_INF)
        l_scr[...] = jnp.zeros_like(l_scr)
        acc_scr[...] = jnp.zeros_like(acc_scr)

    def step(k, v):
        sc = _dot_nt(q2_scr[...], k)
        m_prev = m_scr[...]
        m_new = jnp.maximum(m_prev, jnp.max(sc, axis=1, keepdims=True))
        alpha = jnp.exp(m_prev - m_new)
        p = jnp.exp(sc - m_new)
        l_scr[...] = alpha * l_scr[...] + jnp.sum(p, axis=1, keepdims=True)
        acc_scr[...] = alpha * acc_scr[...] + _dot(p.astype(BF16), v)
        m_scr[...] = m_new

    @pl.when(s == 0)
    def _():
        step(kc_ref[0], vc_ref[0])

    @pl.when(s > 0)
    def _():
        step(kl_ref[0], vl_ref[0])

    @pl.when(s == ns - 1)
    def _():
        lm = lam_ref[...]
        d1 = jnp.sum(lm[0:1] * lm[1:2], axis=1, keepdims=True)
        d2 = jnp.sum(lm[2:3] * lm[3:4], axis=1, keepdims=True)
        lam = jnp.exp(d1) - jnp.exp(d2) + LAM_INIT
        o1 = acc_scr[0:tq] / l_scr[0:tq]
        o2 = acc_scr[tq:2 * tq] / l_scr[tq:2 * tq]
        o = o1 - lam * o2
        ms = jnp.mean(o * o, axis=1, keepdims=True)
        o = o * lax.rsqrt(ms + RMS_EPS) * gn_ref[...] * (1.0 - LAM_INIT)
        o_ref[0] = o.astype(BF16)


def _diff_attn(proj_l, proj_c, lam_pad, gn, *, tq, tk):
    b, t, _ = proj_l.shape
    tc = proj_c.shape[1]
    h = DIFF_HEADS
    nk = t // tk
    kv_l = lambda off: (lambda bi, hi, qi, si: (bi, jnp.maximum(si - 1, 0), off + hi))
    kv_c = lambda off: (lambda bi, hi, qi, si: (bi, 0, off + hi))
    return pl.pallas_call(
        functools.partial(_attn_kernel, tq=tq),
        grid=(b, h, t // tq, nk + 1),
        in_specs=[pl.BlockSpec((SUBLANES, LANES), lambda bi, hi, qi, si: (0, 0)),
                  pl.BlockSpec((1, LANES), lambda bi, hi, qi, si: (0, 0)),
                  pl.BlockSpec((1, tq, LANES), lambda bi, hi, qi, si: (bi, qi, hi)),
                  pl.BlockSpec((1, tc, LANES), kv_c(h)),
                  pl.BlockSpec((1, tc, LANES), kv_c(2 * h)),
                  pl.BlockSpec((1, tk, LANES), kv_l(h)),
                  pl.BlockSpec((1, tk, LANES), kv_l(2 * h))],
        out_specs=pl.BlockSpec((1, tq, LANES), lambda bi, hi, qi, si: (bi, qi, hi)),
        out_shape=jax.ShapeDtypeStruct((b, t, DIFF_W), BF16),
        scratch_shapes=[pltpu.VMEM((2 * tq, LANES), BF16),
                        pltpu.VMEM((2 * tq, 1), F32),
                        pltpu.VMEM((2 * tq, 1), F32),
                        pltpu.VMEM((2 * tq, LANES), F32)],
        compiler_params=_cparams(("arbitrary",) * 4),
        name="diff_attn",
    )(lam_pad, gn, proj_l, proj_c, proj_c, proj_l, proj_l)


def _gate_rows(a_ref, b_ref, ba_ref, bb_ref, rt_scr):
    ng = a_ref.shape[1]
    x = (a_ref[0] + ba_ref[0]).reshape(ng * SUBLANES, LANES)
    ls = jax.nn.log_sigmoid(b_ref[0] + bb_ref[0]).reshape(ng * SUBLANES, LANES)
    ji = lax.broadcasted_iota(jnp.int32, (LANES, LANES), 0)
    si = lax.broadcasted_iota(jnp.int32, (LANES, LANES), 1)
    same = (ji // CHUNK) == (si // CHUNK)
    ones_where = lambda cond: jnp.where(cond, 1.0, 0.0).astype(BF16)
    m_pre = ones_where(same & (ji <= si))
    m_suf = ones_where(same & (ji >= si))
    m_tot = ones_where(same)
    hi, mid, lo = _split3(ls)
    mm = lambda m: _dot(hi, m) + _dot(mid, m) + _dot(lo, m)
    sub = lax.broadcasted_iota(jnp.int32, x.shape, 0) % SUBLANES
    cum = jnp.where(sub % 2 == 0, mm(m_pre), mm(m_suf))
    rt = jnp.where(sub < 2, x - cum, jnp.where(sub < 4, cum, mm(m_tot)))
    rt_scr[...] = rt.reshape(ng, SUBLANES, LANES)


def _conv_silu(raw_ref, w, bias, pad_scr, t, emit):
    pad_scr[0:SUBLANES] = jnp.zeros((SUBLANES, LANES), F32)
    pad_scr[SUBLANES + t:2 * SUBLANES + t] = jnp.zeros((SUBLANES, LANES), F32)
    pad_scr[SUBLANES:SUBLANES + t] = raw_ref[0].astype(F32)
    half = MLSTM_CONV // 2
    for g in range(t // LANES):
        acc = jnp.zeros((LANES, LANES), F32) + bias
        for j in range(MLSTM_CONV):
            off = SUBLANES + g * LANES + j - half
            acc = acc + pad_scr[off:off + LANES] * w[j:j + 1]
        emit(g, acc * jax.nn.sigmoid(acc))


def _mlstm_prologue(q_ref, k_ref, v_ref, cwq_ref, cwk_ref, cbq_ref, cbk_ref,
                    pad_scr, q_scr, kt_scr, vx_scr, t, g0):
    def emit_q(g, blk):
        q_scr[g0 + g] = blk.astype(BF16)

    def emit_k(g, blk):
        kt_scr[g0 + g] = (blk * (MLSTM_DHEAD ** -0.5)).T

    _conv_silu(q_ref, cwq_ref[0], cbq_ref[0], pad_scr, t, emit_q)
    _conv_silu(k_ref, cwk_ref[0], cbk_ref[0], pad_scr, t, emit_k)
    lane = lax.broadcasted_iota(jnp.int32, (LANES, LANES), 1)
    ones_col = jnp.where(lane == 0, 1.0, 0.0).astype(BF16)
    for g in range(t // LANES):
        vx_scr[g0 + g, :, 0:LANES] = v_ref[0, g * LANES:(g + 1) * LANES, :]
        vx_scr[g0 + g, :, LANES:2 * LANES] = ones_col


def _chunk(q_scr, kt_scr, vx_scr, rt, c_scr, m_st, g, half, bwd, out_scr, og):
    lo = half * CHUNK
    d = 1 if bwd else 0
    r_row = rt[d:d + 1, lo:lo + CHUNK]
    cum_row = rt[2 + d:3 + d, lo:lo + CHUNK]
    tot = rt[4 + d:5 + d, lo:lo + 1]
    kt = kt_scr[g, :, lo:lo + CHUNK]
    vx = vx_scr[g, lo:lo + CHUNK, :]
    c_st = c_scr[...]
    mu_full = jnp.maximum(m_st, jnp.max(r_row, axis=1, keepdims=True))
    if out_scr is not None:
        qc = q_scr[g, lo:lo + CHUNK, :]
        li = lax.broadcasted_iota(jnp.int32, (CHUNK, CHUNK), 0)
        si = lax.broadcasted_iota(jnp.int32, (CHUNK, CHUNK), 1)
        mask = (si >= li) if bwd else (si <= li)
        rm = jnp.where(mask, r_row, NEG_INF)
        mu = jnp.maximum(m_st, jnp.max(rm, axis=1, keepdims=True))
        dm = jnp.exp(rm - mu)
        g_in = jnp.exp(m_st - mu)
        cum_col = jnp.sum(jnp.where(li == si, cum_row, 0.0), axis=1, keepdims=True)
        s = _dot(qc, kt.astype(BF16)) * dm
        tot_o = _dot(s.astype(BF16), vx) + g_in * _dot(qc, c_st.astype(BF16))
        num = tot_o[:, 0:LANES]
        den = tot_o[:, LANES:LANES + 1]
        floor = jnp.exp(-(cum_col + mu))
        out_scr[og, lo:lo + CHUNK, :] = num / jnp.maximum(jnp.abs(den), floor)
    wk = jnp.exp(r_row - mu_full)
    decay = jnp.exp(m_st - mu_full)
    kw = (kt * wk).astype(BF16)
    c_scr[...] = decay * c_st + _dot(kw, vx)
    return tot + mu_full


def _mlstm_kernel(ql_ref, kl_ref, vl_ref, ol_ref, qc_ref, kc_ref, vc_ref,
                  al_ref, bl_ref, ac_ref, bc_ref, ba_ref, bb_ref,
                  cwq_ref, cwk_ref, cbq_ref, cbk_ref, gn_ref, o_ref,
                  pad_scr, q_scr, kt_scr, vx_scr, rtc_scr, rtl_scr, cf_scr, cb_scr, hf_scr, hb_scr,
                  *, t, tc):
    _gate_rows(ac_ref.at[0], bc_ref.at[0], ba_ref, bb_ref, rtc_scr)
    _gate_rows(al_ref.at[0], bl_ref.at[0], ba_ref, bb_ref, rtl_scr)
    _mlstm_prologue(qc_ref, kc_ref, vc_ref, cwq_ref, cwk_ref, cbq_ref, cbk_ref,
                    pad_scr, q_scr, kt_scr, vx_scr, tc, 0)
    _mlstm_prologue(ql_ref, kl_ref, vl_ref, cwq_ref, cwk_ref, cbq_ref, cbk_ref,
                    pad_scr, q_scr, kt_scr, vx_scr, t, tc // LANES)
    cf_scr[...] = jnp.zeros_like(cf_scr)
    cb_scr[...] = jnp.zeros_like(cb_scr)
    m0 = jnp.zeros((1, 1), F32)

    def group(rt_scr, ng, g0, outs):
        def body(it, carry):
            mf, mb = carry
            gf = it
            gb = ng - 1 - it
            rf = rt_scr[gf]
            rb = rt_scr[gb]
            args = (q_scr, kt_scr, vx_scr)
            mf = _chunk(*args, rf, cf_scr, mf, g0 + gf, 0, False, outs and hf_scr, gf)
            mb = _chunk(*args, rb, cb_scr, mb, g0 + gb, 1, True, outs and hb_scr, gb)
            mf = _chunk(*args, rf, cf_scr, mf, g0 + gf, 1, False, outs and hf_scr, gf)
            mb = _chunk(*args, rb, cb_scr, mb, g0 + gb, 0, True, outs and hb_scr, gb)
            return mf, mb
        return body

    ngc = tc // LANES
    carry = lax.fori_loop(0, ngc, group(rtc_scr, ngc, 0, None), (m0, m0))
    lax.fori_loop(0, t // LANES, group(rtl_scr, t // LANES, ngc, True), carry)

    hm = (hf_scr[...] + hb_scr[...]).reshape(t, LANES)
    ms = jnp.mean(hm * hm, axis=1, keepdims=True)
    out = hm * lax.rsqrt(ms + RMS_EPS) * gn_ref[...] * jax.nn.sigmoid(ol_ref[0].astype(F32))
    o_ref[0] = out.astype(BF16)


def _mlstm(proj_l, proj_c, ga_l, gb_l, ga_c, gb_c, bias_a, bias_b, conv_w, conv_b, gn):
    b, t, _ = proj_l.shape
    tc = proj_c.shape[1]
    h = MLSTM_HEADS
    col = lambda off: (lambda bi, hi: (bi, 0, off + hi))
    q0 = (3 * DIFF_W) // LANES
    gspec = lambda ng: pl.BlockSpec((1, 1, ng, SUBLANES, LANES), lambda bi, hi: (bi, hi, 0, 0, 0))
    hspec = pl.BlockSpec((1, SUBLANES, LANES), lambda bi, hi: (hi, 0, 0))
    cw = jnp.pad(conv_w, ((0, SUBLANES - MLSTM_CONV), (0, 0)))
    cwspec = lambda off: pl.BlockSpec((1, SUBLANES, LANES), lambda bi, hi: (0, 0, off + hi))
    cbspec = lambda off: pl.BlockSpec((1, 1, LANES), lambda bi, hi: (0, 0, off + hi))
    tt = t + tc
    return pl.pallas_call(
        functools.partial(_mlstm_kernel, t=t, tc=tc),
        grid=(b, h),
        in_specs=[pl.BlockSpec((1, t, LANES), col(q0)),
                  pl.BlockSpec((1, t, LANES), col(q0 + h)),
                  pl.BlockSpec((1, t, LANES), col(q0 + 2 * h)),
                  pl.BlockSpec((1, t, LANES), col(q0 + 3 * h)),
                  pl.BlockSpec((1, tc, LANES), col(q0)),
                  pl.BlockSpec((1, tc, LANES), col(q0 + h)),
                  pl.BlockSpec((1, tc, LANES), col(q0 + 2 * h)),
                  gspec(t // LANES), gspec(t // LANES), gspec(tc // LANES), gspec(tc // LANES),
                  hspec, hspec,
                  cwspec(0), cwspec(h), cbspec(0), cbspec(h),
                  pl.BlockSpec((1, LANES), lambda bi, hi: (0, 0))],
        out_specs=pl.BlockSpec((1, t, LANES), lambda bi, hi: (bi, 0, hi)),
        out_shape=jax.ShapeDtypeStruct((b, t, ML_W), BF16),
        scratch_shapes=[pltpu.VMEM((t + 2 * SUBLANES, LANES), F32),
                        pltpu.VMEM((tt // LANES, LANES, LANES), BF16),
                        pltpu.VMEM((tt // LANES, LANES, LANES), F32),
                        pltpu.VMEM((tt // LANES, LANES, 2 * LANES), BF16),
                        pltpu.VMEM((tc // LANES, SUBLANES, LANES), F32),
                        pltpu.VMEM((t // LANES, SUBLANES, LANES), F32),
                        pltpu.VMEM((LANES, 2 * LANES), F32),
                        pltpu.VMEM((LANES, 2 * LANES), F32),
                        pltpu.VMEM((t // LANES, LANES, LANES), F32),
                        pltpu.VMEM((t // LANES, LANES, LANES), F32)],
        compiler_params=_cparams(("arbitrary", "arbitrary")),
        name="mlstm",
    )(proj_l, proj_l, proj_l, proj_l, proj_c, proj_c, proj_c,
      ga_l, gb_l, ga_c, gb_c, bias_a, bias_b,
      cw.reshape(1, SUBLANES, 2 * ML_W), cw.reshape(1, SUBLANES, 2 * ML_W),
      conv_b.reshape(1, 1, 2 * ML_W), conv_b.reshape(1, 1, 2 * ML_W), gn)


def _layer_norm(v, g, b):
    mu = jnp.mean(v, axis=1, keepdims=True)
    c = v - mu
    var = jnp.mean(c * c, axis=1, keepdims=True)
    return c * lax.rsqrt(var + LN_EPS) * g + b


def _outproj_kernel(d_ref, m_ref, w_ref, x_ref, g1_ref, lg_ref, lb_ref, sh_ref, sc_ref,
                    x1_ref, h2_ref):
    y = _dot(d_ref[0], w_ref[0]) + _dot(m_ref[0], w_ref[1])
    v = DEEPNORM_ALPHA * x_ref[0] + g1_ref[0] * y
    x1 = _layer_norm(v, lg_ref[...], lb_ref[...])
    x1_ref[0] = x1
    h2_ref[0] = (x1 * (1.0 + sc_ref[0]) + sh_ref[0]).astype(BF16)


def _out_proj(d_lat, m_lat, w_out2, x, g1, ln_g, ln_b, sh2, sc2, *, tm):
    b, t, d = x.shape
    hw = d_lat.shape[2]
    row = pl.BlockSpec((1, 1, d), lambda bi, i: (bi, 0, 0))
    vec = pl.BlockSpec((1, d), lambda bi, i: (0, 0))
    return pl.pallas_call(
        _outproj_kernel,
        grid=(b, t // tm),
        in_specs=[pl.BlockSpec((1, tm, hw), lambda bi, i: (bi, i, 0)),
                  pl.BlockSpec((1, tm, hw), lambda bi, i: (bi, i, 0)),
                  pl.BlockSpec((2, hw, d), lambda bi, i: (0, 0, 0)),
                  pl.BlockSpec((1, tm, d), lambda bi, i: (bi, i, 0)),
                  row, vec, vec, row, row],
        out_specs=[pl.BlockSpec((1, tm, d), lambda bi, i: (bi, i, 0)),
                   pl.BlockSpec((1, tm, d), lambda bi, i: (bi, i, 0))],
        out_shape=[jax.ShapeDtypeStruct((b, t, d), F32),
                   jax.ShapeDtypeStruct((b, t, d), BF16)],
        compiler_params=_cparams(("arbitrary", "arbitrary")),
        name="out_proj",
    )(d_lat, m_lat, w_out2, x, g1, ln_g, ln_b, sh2, sc2)


def _top_rows(x, k, dst_scr):
    cur = x
    for r in range(k):
        m = jnp.max(cur, axis=0, keepdims=True)
        dst_scr[r:r + 1] = m
        if r + 1 < k:
            cur = jnp.where(cur == m, NEG_INF, cur)


def _peer_prep_kernel(h_ref, wq_ref, k1_ref, k2_ref, s1_ref, s2_ref, e1_ref, e2_ref, tau_ref,
                      v1_scr, v2_scr, tp_scr):
    q = _dot(h_ref[...], wq_ref[...])
    half = PEER_DQ // 2
    for h in range(PEER_HEADS):
        q1 = q[:, h * PEER_DQ:h * PEER_DQ + half].astype(BF16)
        q2 = q[:, h * PEER_DQ + half:(h + 1) * PEER_DQ].astype(BF16)
        s1 = _dot_nt(k1_ref[h], q1)
        s2 = _dot_nt(k2_ref[h], q2)
        _top_rows(s1, PEER_TOPK, v1_scr)
        _top_rows(s2, PEER_TOPK, v2_scr)
        v1 = v1_scr[...]
        v2 = v2_scr[...]
        pieces = [v1[0:1] + v2]
        pieces += [v1[a:a + 1] + v2[0:SUBLANES] for a in range(1, SUBLANES)]
        pieces += [v1[SUBLANES:PEER_TOPK] + v2[0:1]]
        cand = jnp.concatenate(pieces, axis=0)
        _top_rows(cand, PEER_TOPK, tp_scr)
        tp = tp_scr[...]
        top0 = tp[0:1]
        tau = tp[PEER_TOPK - 1:PEER_TOPK]
        z = jnp.sum(jnp.exp(tp - top0), axis=0, keepdims=True)
        s1_ref[h] = s1
        s2_ref[h] = s2
        e1_ref[h] = jnp.exp(s1 - v1[0:1]) / z
        e2_ref[h] = jnp.exp(s2 - v2[0:1])
        tau_ref[h] = jnp.broadcast_to(tau, (SUBLANES, tau.shape[1]))


def _peer_prep(h2, wq, k1, k2, *, tt):
    n, d = h2.shape
    hp = PEER_HEADS
    big = jax.ShapeDtypeStruct((hp, PEER_NKEYS, n), F32)
    bspec = pl.BlockSpec((hp, PEER_NKEYS, tt), lambda i: (0, 0, i))
    kspec = pl.BlockSpec((hp, PEER_NKEYS, PEER_DQ // 2), lambda i: (0, 0, 0))
    return pl.pallas_call(
        _peer_prep_kernel,
        grid=(n // tt,),
        in_specs=[pl.BlockSpec((tt, d), lambda i: (i, 0)),
                  pl.BlockSpec((d, hp * PEER_DQ), lambda i: (0, 0)),
                  kspec, kspec],
        out_specs=[bspec, bspec, bspec, bspec,
                   pl.BlockSpec((hp, SUBLANES, tt), lambda i: (0, 0, i))],
        out_shape=[big, big, big, big, jax.ShapeDtypeStruct((hp, SUBLANES, n), F32)],
        scratch_shapes=[pltpu.VMEM((PEER_TOPK, tt), F32),
                        pltpu.VMEM((PEER_TOPK, tt), F32),
                        pltpu.VMEM((PEER_TOPK, tt), F32)],
        compiler_params=_cparams(("arbitrary",)),
        name="peer_prep",
    )(h2, wq, k1, k2)


def _peer_dense_kernel(h_ref, u_ref, vt_ref, s1_ref, s2_ref, e1_ref, e2_ref, tau_ref, o_ref,
                       acc_scr, w_scr, *, te):
    j = pl.program_id(1)
    nj = pl.num_programs(1)

    @pl.when(j == 0)
    def _():
        acc_scr[...] = jnp.zeros_like(acc_scr)

    zt = _dot_nt(u_ref[...], h_ref[...])
    nk = PEER_NKEYS
    for a in range(te // nk):
        i1 = j * (te // nk) + a
        g = None
        for h in range(PEER_HEADS):
            s1row = s1_ref[h, pl.ds(i1, 1), :]
            e1row = e1_ref[h, pl.ds(i1, 1), :]
            ssum = s2_ref[h] + s1row
            term = jnp.where(ssum >= tau_ref[h, 0:1, :], e2_ref[h] * e1row, 0.0)
            g = term if g is None else g + term
        z = zt[a * nk:(a + 1) * nk]
        gelu = 0.5 * z * (1.0 + lax.erf(z * math.sqrt(0.5)))
        w_scr[a * nk:(a + 1) * nk] = (gelu * g).astype(BF16)
    acc_scr[...] += _dot(vt_ref[...], w_scr[...])

    @pl.when(j == nj - 1)
    def _():
        o_ref[...] = acc_scr[...].T


def _peer_dense(h2, u_bf, vt_bf, s1, s2, e1, e2, tau, *, tt, te):
    n, d = h2.shape
    ne = u_bf.shape[0]
    hp = PEER_HEADS
    bspec = pl.BlockSpec((hp, PEER_NKEYS, tt), lambda i, j: (0, 0, i))
    return pl.pallas_call(
        functools.partial(_peer_dense_kernel, te=te),
        grid=(n // tt, ne // te),
        in_specs=[pl.BlockSpec((tt, d), lambda i, j: (i, 0)),
                  pl.BlockSpec((te, d), lambda i, j: (j, 0)),
                  pl.BlockSpec((d, te), lambda i, j: (0, j)),
                  bspec, bspec, bspec, bspec,
                  pl.BlockSpec((hp, SUBLANES, tt), lambda i, j: (0, 0, i))],
        out_specs=pl.BlockSpec((tt, d), lambda i, j: (i, 0)),
        out_shape=jax.ShapeDtypeStruct((n, d), F32),
        scratch_shapes=[pltpu.VMEM((d, tt), F32),
                        pltpu.VMEM((te, tt), BF16)],
        compiler_params=_cparams(("arbitrary", "arbitrary")),
        name="peer_dense",
    )(h2, u_bf, vt_bf, s1, s2, e1, e2, tau)


def _final_kernel(x1_ref, p_ref, g2_ref, lg_ref, lb_ref, o_ref):
    v = DEEPNORM_ALPHA * x1_ref[0] + g2_ref[0] * p_ref[0]
    o_ref[0] = _layer_norm(v, lg_ref[...], lb_ref[...])


def _final_ln(x1, peer, g2, ln_g, ln_b, *, tm):
    b, t, d = x1.shape
    blk = pl.BlockSpec((1, tm, d), lambda bi, i: (bi, i, 0))
    vec = pl.BlockSpec((1, d), lambda bi, i: (0, 0))
    return pl.pallas_call(
        _final_kernel,
        grid=(b, t // tm),
        in_specs=[blk, blk, pl.BlockSpec((1, 1, d), lambda bi, i: (bi, 0, 0)), vec, vec],
        out_specs=blk,
        out_shape=jax.ShapeDtypeStruct((b, t, d), F32),
        compiler_params=_cparams(("arbitrary", "arbitrary")),
        name="final_ln",
    )(x1, peer, g2, ln_g, ln_b)


def _rope_tables(t):
    pos = jnp.arange(t, dtype=jnp.int32)
    row = (pos // GRID_W).astype(F32)
    col = (pos % GRID_W).astype(F32)
    inv = ROPE_BASE ** (-jnp.arange(ROPE_NF, dtype=F32) / ROPE_NF)
    lane = jnp.arange(LANES)
    use_col = ((lane // (2 * ROPE_NF)) % 2) == 1
    p = jnp.where(use_col[None, :], col[:, None], row[:, None])
    ang = p * inv[lane % ROPE_NF][None, :]
    sign = jnp.where((lane % (2 * ROPE_NF)) < ROPE_NF, -1.0, 1.0).astype(F32)
    return jnp.cos(ang), jnp.sin(ang) * sign[None, :]


def _gate_layout(g):
    b, t, _ = g.shape
    gt = g[:, :, :N_GATES * MLSTM_HEADS].reshape(b, t // LANES, LANES, N_GATES, MLSTM_HEADS)
    gt = gt.transpose(0, 4, 1, 3, 2)
    zeros = jnp.zeros_like(gt[:, :, :, 0:1])
    a = jnp.concatenate([gt[:, :, :, 0:1], gt[:, :, :, 2:3]] + [zeros] * 6, axis=3)
    fb = jnp.concatenate([gt[:, :, :, 1:2], gt[:, :, :, 3:4]], axis=3)
    return a, jnp.tile(fb, (1, 1, 1, 4, 1))


def _gate_bias(gate_b):
    gb = gate_b.astype(F32)
    z = jnp.zeros_like(gb[0])
    a = jnp.stack([gb[0], gb[2]] + [z] * 6, axis=1)
    bm = jnp.stack([gb[1], gb[3]] * 4, axis=1)
    bc = lambda v: jnp.broadcast_to(v[:, :, None], (MLSTM_HEADS, SUBLANES, LANES))
    return bc(a), bc(bm)


def kernel(x, c, ctx, c_ctx, w_ada, b_ada, w_in, conv_w, conv_b, gate_b, diff_lambda, diff_norm_g,
           mlstm_norm_g, w_out, ln1_g, ln1_b, ln2_g, ln2_b, peer_wq, peer_keys, peer_u, peer_v):
    b, t, d = x.shape
    tc = ctx.shape[1]
    l = 0

    c_rows = jnp.concatenate([c, c_ctx[None, :]], axis=0)
    mod = _adaln(c_rows, w_ada[l], b_ada[l])
    sh1, sc1, g1, sh2, sc2, g2 = [m[:, None, :] for m in jnp.split(mod[:b], 6, axis=1)]
    csh1, csc1 = [m[:, None, :] for m in jnp.split(mod[b:b + 1], 6, axis=1)[:2]]
    csh1 = jnp.broadcast_to(csh1, (b, 1, d))
    csc1 = jnp.broadcast_to(csc1, (b, 1, d))

    w = w_in[l]
    w_main = jnp.concatenate([w[:, :DIFF_W] * (DIFF_HALF ** -0.5), w[:, DIFF_W:MAIN_COLS]], axis=1).astype(BF16)
    wg = jnp.pad(w[:, MAIN_COLS:], ((0, 0), (0, LANES - N_GATES * MLSTM_HEADS)))
    wg_hi = wg.astype(BF16)
    wg_lo = (wg - wg_hi.astype(F32)).astype(BF16)
    wg2 = jnp.stack([wg_hi, wg_lo])
    cos_t, sin_t = _rope_tables(t)
    proj_l, gates_l = _in_proj(x, sh1, sc1, w_main, wg2, cos_t, sin_t, tm=min(1024, t))
    proj_c, gates_c = _in_proj(ctx, csh1, csc1, w_main, wg2,
                               jnp.ones((tc, LANES), F32), jnp.zeros((tc, LANES), F32), tm=tc)

    lam_pad = jnp.pad(diff_lambda[l].astype(F32), ((0, SUBLANES - 4), (0, LANES - DIFF_HALF)))
    d_lat = _diff_attn(proj_l, proj_c, lam_pad, diff_norm_g[l].reshape(1, LANES),
                       tq=min(512, t), tk=min(512, t))

    ga_l, gb_l = _gate_layout(gates_l)
    ga_c, gb_c = _gate_layout(gates_c)
    bias_a, bias_b = _gate_bias(gate_b[l])
    m_lat = _mlstm(proj_l, proj_c, ga_l, gb_l, ga_c, gb_c, bias_a, bias_b,
                   conv_w[l], conv_b[l], mlstm_norm_g[l].reshape(1, LANES))

    w_out2 = w_out[l].astype(BF16).reshape(2, DIFF_W, d)
    x1, h2 = _out_proj(d_lat, m_lat, w_out2, x, g1, ln1_g[l].reshape(1, d), ln1_b[l].reshape(1, d),
                       sh2, sc2, tm=min(512, t))

    n = b * t
    h2f = h2.reshape(n, d)
    keys = peer_keys[l].astype(BF16)
    s1, s2, e1, e2, tau = _peer_prep(h2f, peer_wq[l].astype(BF16), keys[0], keys[1], tt=min(512, n))
    peer = _peer_dense(h2f, peer_u[l].astype(BF16), peer_v[l].T.astype(BF16), s1, s2, e1, e2, tau,
                       tt=min(512, n), te=512)

    return _final_ln(x1, peer.reshape(b, t, d), g2, ln2_g[l].reshape(1, d), ln2_b[l].reshape(1, d),
                     tm=min(512, t))
```

```python
import functools
import math

import jax
import jax.numpy as jnp
from jax import lax
from jax.experimental import pallas as pl
from jax.experimental.pallas import tpu as pltpu

F32 = jnp.float32
BF16 = jnp.bfloat16

LANES = 128
SUBLANES = 8
VMEM_LIMIT = 56 * 1024 * 1024

GRID_W = 64
DIFF_HEADS = 8
DIFF_HALF = 64
DIFF_W = DIFF_HEADS * 2 * DIFF_HALF
MLSTM_HEADS = 8
MLSTM_DHEAD = 128
ML_W = MLSTM_HEADS * MLSTM_DHEAD
MLSTM_CONV = 5
CHUNK = 64
N_GATES = 4
MAIN_COLS = 3 * DIFF_W + 4 * ML_W
ROPE_COLS = 2 * DIFF_W
ROPE_BASE = 10000.0
ROPE_NF = 16
PEER_HEADS = 8
PEER_NKEYS = 128
PEER_DQ = 256
PEER_TOPK = 16
DEPTH = 1
DEEPNORM_ALPHA = (2.0 * DEPTH) ** 0.25
LN_EPS = 1e-5
RMS_EPS = 1e-6
LAM_INIT = 0.8 - 0.6 * math.exp(-0.3 * 0)
NEG_INF = float("-inf")


def _cparams(sem):
    return pltpu.CompilerParams(dimension_semantics=sem, vmem_limit_bytes=VMEM_LIMIT)


def _dot(a, b):
    return jnp.dot(a, b, preferred_element_type=F32)


def _dot_nt(a, b):
    return lax.dot_general(a, b, (((1,), (1,)), ((), ())), preferred_element_type=F32)


def _split3(x):
    hi = x.astype(BF16)
    r1 = x - hi.astype(F32)
    mid = r1.astype(BF16)
    lo = (r1 - mid.astype(F32)).astype(BF16)
    return hi, mid, lo


def _adaln_kernel(cb_ref, w_ref, b_ref, o_ref, act_scr, *, n_rows):
    @pl.when(pl.program_id(0) == 0)
    def _():
        c = cb_ref[...]
        act_scr[...] = c * jax.nn.sigmoid(c)

    tn = w_ref.shape[1]
    o_ref[...] = jnp.zeros_like(o_ref)
    for cb in range(tn // LANES):
        w = w_ref[:, cb * LANES:(cb + 1) * LANES]
        for r in range(n_rows):
            s = jnp.sum(act_scr[r] * w, axis=0, keepdims=True)
            o_ref[r:r + 1, cb * LANES:(cb + 1) * LANES] = s + b_ref[:, cb * LANES:(cb + 1) * LANES]


def _adaln(c_rows, w_ada, b_ada):
    n_rows, k = c_rows.shape
    n = w_ada.shape[1]
    tn = 512
    cb = jnp.broadcast_to(c_rows[:, :, None], (n_rows, k, LANES))
    return pl.pallas_call(
        functools.partial(_adaln_kernel, n_rows=n_rows),
        grid=(n // tn,),
        in_specs=[pl.BlockSpec((n_rows, k, LANES), lambda j: (0, 0, 0)),
                  pl.BlockSpec((k, tn), lambda j: (0, j)),
                  pl.BlockSpec((1, tn), lambda j: (0, j))],
        out_specs=pl.BlockSpec((SUBLANES, tn), lambda j: (0, j)),
        out_shape=jax.ShapeDtypeStruct((SUBLANES, n), F32),
        scratch_shapes=[pltpu.VMEM((n_rows, k, LANES), F32)],
        compiler_params=_cparams(("arbitrary",)),
        name="adaln",
    )(cb, w_ada, b_ada.reshape(1, n))


def _inproj_kernel(x_ref, sh_ref, sc_ref, w_ref, wg_ref, cos_ref, sin_ref, o_ref, g_ref, h_scr,
                   *, n_rope_tiles):
    j = pl.program_id(2)

    @pl.when(j == 0)
    def _():
        h = x_ref[0] * (1.0 + sc_ref[0]) + sh_ref[0]
        hb = h.astype(BF16)
        h_scr[...] = hb
        hl = (h - hb.astype(F32)).astype(BF16)
        wgh = wg_ref[0]
        wgl = wg_ref[1]
        g_ref[0] = _dot(hb, wgh) + _dot(hl, wgh) + _dot(hb, wgl)

    acc = _dot(h_scr[...], w_ref[...])
    tn = acc.shape[1]

    @pl.when(j < n_rope_tiles)
    def _():
        cos = cos_ref[...]
        sin = sin_ref[...]
        lane = lax.broadcasted_iota(jnp.int32, cos.shape, 1)
        first = (lane % (2 * ROPE_NF)) < ROPE_NF
        for cb in range(tn // LANES):
            a = acc[:, cb * LANES:(cb + 1) * LANES]
            sw = jnp.where(first, pltpu.roll(a, LANES - ROPE_NF, 1), pltpu.roll(a, ROPE_NF, 1))
            o_ref[0, :, cb * LANES:(cb + 1) * LANES] = (a * cos + sw * sin).astype(BF16)

    @pl.when(j >= n_rope_tiles)
    def _():
        o_ref[0] = acc.astype(BF16)


def _in_proj(x, shift, scale, w_main, wg, cos_t, sin_t, *, tm):
    b, t, d = x.shape
    n = w_main.shape[1]
    tn = 512
    return pl.pallas_call(
        functools.partial(_inproj_kernel, n_rope_tiles=ROPE_COLS // tn),
        grid=(b, t // tm, n // tn),
        in_specs=[pl.BlockSpec((1, tm, d), lambda bi, i, j: (bi, i, 0)),
                  pl.BlockSpec((1, 1, d), lambda bi, i, j: (bi, 0, 0)),
                  pl.BlockSpec((1, 1, d), lambda bi, i, j: (bi, 0, 0)),
                  pl.BlockSpec((d, tn), lambda bi, i, j: (0, j)),
                  pl.BlockSpec((2, d, LANES), lambda bi, i, j: (0, 0, 0)),
                  pl.BlockSpec((tm, LANES), lambda bi, i, j: (i, 0)),
                  pl.BlockSpec((tm, LANES), lambda bi, i, j: (i, 0))],
        out_specs=[pl.BlockSpec((1, tm, tn), lambda bi, i, j: (bi, i, j)),
                   pl.BlockSpec((1, tm, LANES), lambda bi, i, j: (bi, i, 0))],
        out_shape=[jax.ShapeDtypeStruct((b, t, n), BF16),
                   jax.ShapeDtypeStruct((b, t, LANES), F32)],
        scratch_shapes=[pltpu.VMEM((tm, d), BF16)],
        compiler_params=_cparams(("arbitrary", "arbitrary", "arbitrary")),
        name="in_proj",
    )(x, shift, scale, w_main, wg, cos_t, sin_t)


def _attn_kernel(lam_ref, gn_ref, q_ref, kc_ref, vc_ref, kl_ref, vl_ref, o_ref,
                 q2_scr, vx_scr, m_scr, acc_scr, *, tq, tk, rq):
    tc = kc_ref.shape[1]
    t = kl_ref.shape[1]

    @pl.when(pl.program_id(2) == 0)
    def _():
        lane = lax.broadcasted_iota(jnp.int32, (tc, LANES), 1)
        ones_c = jnp.where(lane == 0, 1.0, 0.0).astype(BF16)
        vx_scr[0:tc, 0:LANES] = vc_ref[0]
        vx_scr[0:tc, LANES:2 * LANES] = ones_c
        for r0 in range(0, t, tc):
            vx_scr[tc + r0:tc + r0 + tc, 0:LANES] = vl_ref[0, r0:r0 + tc, :]
            vx_scr[tc + r0:tc + r0 + tc, LANES:2 * LANES] = ones_c

    q = q_ref[0]
    qf = q.astype(F32)
    lane = lax.broadcasted_iota(jnp.int32, qf.shape, 1)
    q2_scr[0:tq] = jnp.where(lane < DIFF_HALF, qf, 0.0).astype(BF16)
    q2_scr[tq:2 * tq] = jnp.where(lane >= DIFF_HALF, qf, 0.0).astype(BF16)
    m_scr[...] = jnp.full_like(m_scr, NEG_INF)
    acc_scr[...] = jnp.zeros_like(acc_scr)

    def step(k, vx):
        nkc = k.shape[0] // LANES
        for r0 in range(0, 2 * tq, rq):
            sc = _dot_nt(q2_scr[r0:r0 + rq], k)
            cols = [sc[:, c * LANES:(c + 1) * LANES] for c in range(nkc)]
            mx = cols[0]
            for c in range(1, nkc):
                mx = jnp.maximum(mx, cols[c])
            m_prev = m_scr[r0:r0 + rq]
            m_new = jnp.maximum(m_prev, jnp.max(mx, axis=1, keepdims=True))
            alpha = jnp.exp(m_prev - m_new)
            p = jnp.concatenate([jnp.exp(cb - m_new).astype(BF16) for cb in cols], axis=1)
            alpha2 = jnp.concatenate([alpha, alpha], axis=1)
            acc_scr[r0:r0 + rq] = alpha2 * acc_scr[r0:r0 + rq] + _dot(p, vx)
            m_scr[r0:r0 + rq] = m_new

    step(kc_ref[0], vx_scr[0:tc])

    def body(j, carry):
        r = pl.multiple_of(j * tk, tk)
        step(kl_ref[0, pl.ds(r, tk), :], vx_scr[pl.ds(tc + r, tk)])
        return carry

    lax.fori_loop(0, t // tk, body, 0)

    lm = lam_ref[...]
    d1 = jnp.sum(lm[0:1] * lm[1:2], axis=1, keepdims=True)
    d2 = jnp.sum(lm[2:3] * lm[3:4], axis=1, keepdims=True)
    lam = jnp.exp(d1) - jnp.exp(d2) + LAM_INIT
    a1 = acc_scr[0:tq]
    a2 = acc_scr[tq:2 * tq]
    o1 = a1[:, 0:LANES] / a1[:, LANES:LANES + 1]
    o2 = a2[:, 0:LANES] / a2[:, LANES:LANES + 1]
    o = o1 - lam * o2
    ms = jnp.mean(o * o, axis=1, keepdims=True)
    o = o * lax.rsqrt(ms + RMS_EPS) * gn_ref[...] * (1.0 - LAM_INIT)
    o_ref[0] = o.astype(BF16)


def _diff_attn(proj_l, proj_c, lam_pad, gn, *, tq, tk, rq):
    b, t, _ = proj_l.shape
    tc = proj_c.shape[1]
    h = DIFF_HEADS
    kv = lambda off: (lambda bi, hi, qi: (bi, 0, off + hi))
    return pl.pallas_call(
        functools.partial(_attn_kernel, tq=tq, tk=tk, rq=rq),
        grid=(b, h, t // tq),
        in_specs=[pl.BlockSpec((SUBLANES, LANES), lambda bi, hi, qi: (0, 0)),
                  pl.BlockSpec((1, LANES), lambda bi, hi, qi: (0, 0)),
                  pl.BlockSpec((1, tq, LANES), lambda bi, hi, qi: (bi, qi, hi)),
                  pl.BlockSpec((1, tc, LANES), kv(h)),
                  pl.BlockSpec((1, tc, LANES), kv(2 * h)),
                  pl.BlockSpec((1, t, LANES), kv(h)),
                  pl.BlockSpec((1, t, LANES), kv(2 * h))],
        out_specs=pl.BlockSpec((1, tq, LANES), lambda bi, hi, qi: (bi, qi, hi)),
        out_shape=jax.ShapeDtypeStruct((b, t, DIFF_W), BF16),
        scratch_shapes=[pltpu.VMEM((2 * tq, LANES), BF16),
                        pltpu.VMEM((tc + t, 2 * LANES), BF16),
                        pltpu.VMEM((2 * tq, LANES), F32),
                        pltpu.VMEM((2 * tq, 2 * LANES), F32)],
        compiler_params=_cparams(("arbitrary",) * 3),
        name="diff_attn",
    )(lam_pad, gn, proj_l, proj_c, proj_c, proj_l, proj_l)


def _gate_rows(a_ref, b_ref, ba_ref, bb_ref, rt_scr):
    ng = a_ref.shape[1]
    x = (a_ref[0] + ba_ref[0]).reshape(ng * SUBLANES, LANES)
    ls = jax.nn.log_sigmoid(b_ref[0] + bb_ref[0]).reshape(ng * SUBLANES, LANES)
    ji = lax.broadcasted_iota(jnp.int32, (LANES, LANES), 0)
    si = lax.broadcasted_iota(jnp.int32, (LANES, LANES), 1)
    same = (ji // CHUNK) == (si // CHUNK)
    ones_where = lambda cond: jnp.where(cond, 1.0, 0.0).astype(BF16)
    m_pre = ones_where(same & (ji <= si))
    m_suf = ones_where(same & (ji >= si))
    m_tot = ones_where(same)
    hi, mid, lo = _split3(ls)
    mm = lambda m: _dot(hi, m) + _dot(mid, m) + _dot(lo, m)
    sub = lax.broadcasted_iota(jnp.int32, x.shape, 0) % SUBLANES
    cum = jnp.where(sub % 2 == 0, mm(m_pre), mm(m_suf))
    rt = jnp.where(sub < 2, x - cum, jnp.where(sub < 4, cum, mm(m_tot)))
    rt_scr[...] = rt.reshape(ng, SUBLANES, LANES)


def _conv_silu(raw_ref, w, bias, pad_scr, t, emit):
    pad_scr[0:SUBLANES] = jnp.zeros((SUBLANES, LANES), F32)
    pad_scr[SUBLANES + t:2 * SUBLANES + t] = jnp.zeros((SUBLANES, LANES), F32)
    pad_scr[SUBLANES:SUBLANES + t] = raw_ref[0].astype(F32)
    half = MLSTM_CONV // 2
    for g in range(t // LANES):
        acc = jnp.zeros((LANES, LANES), F32) + bias
        for j in range(MLSTM_CONV):
            off = SUBLANES + g * LANES + j - half
            acc = acc + pad_scr[off:off + LANES] * w[j:j + 1]
        emit(g, acc * jax.nn.sigmoid(acc))


def _mlstm_prologue(q_ref, k_ref, v_ref, cwq_ref, cwk_ref, cbq_ref, cbk_ref,
                    pad_scr, q_scr, kt_scr, vx_scr, t, g0):
    def emit_q(g, blk):
        q_scr[g0 + g] = blk.astype(BF16)

    def emit_k(g, blk):
        kt_scr[g0 + g] = (blk * (MLSTM_DHEAD ** -0.5)).T

    _conv_silu(q_ref, cwq_ref[0], cbq_ref[0], pad_scr, t, emit_q)
    _conv_silu(k_ref, cwk_ref[0], cbk_ref[0], pad_scr, t, emit_k)
    lane = lax.broadcasted_iota(jnp.int32, (LANES, LANES), 1)
    ones_col = jnp.where(lane == 0, 1.0, 0.0).astype(BF16)
    for g in range(t // LANES):
        vx_scr[g0 + g, :, 0:LANES] = v_ref[0, g * LANES:(g + 1) * LANES, :]
        vx_scr[g0 + g, :, LANES:2 * LANES] = ones_col


def _chunk(q_scr, kt_scr, vx_scr, rt, c_scr, m_st, g, half, bwd, out_scr, og):
    lo = half * CHUNK
    d = 1 if bwd else 0
    r_row = rt[d:d + 1, lo:lo + CHUNK]
    cum_row = rt[2 + d:3 + d, lo:lo + CHUNK]
    tot = rt[4 + d:5 + d, lo:lo + 1]
    kt = kt_scr[g, :, lo:lo + CHUNK]
    vx = vx_scr[g, lo:lo + CHUNK, :]
    c_st = c_scr[...]
    mu_full = jnp.maximum(m_st, jnp.max(r_row, axis=1, keepdims=True))
    if out_scr is not None:
        qc = q_scr[g, lo:lo + CHUNK, :]
        li = lax.broadcasted_iota(jnp.int32, (CHUNK, CHUNK), 0)
        si = lax.broadcasted_iota(jnp.int32, (CHUNK, CHUNK), 1)
        mask = (si >= li) if bwd else (si <= li)
        rm = jnp.where(mask, r_row, NEG_INF)
        mu = jnp.maximum(m_st, jnp.max(rm, axis=1, keepdims=True))
        dm = jnp.exp(rm - mu)
        g_in = jnp.exp(m_st - mu)
        cum_col = jnp.sum(jnp.where(li == si, cum_row, 0.0), axis=1, keepdims=True)
        s = _dot(qc, kt.astype(BF16)) * dm
        tot_o = _dot(s.astype(BF16), vx) + g_in * _dot(qc, c_st.astype(BF16))
        num = tot_o[:, 0:LANES]
        den = tot_o[:, LANES:LANES + 1]
        floor = jnp.exp(-(cum_col + mu))
        out_scr[og, lo:lo + CHUNK, :] = num / jnp.maximum(jnp.abs(den), floor)
    wk = jnp.exp(r_row - mu_full)
    decay = jnp.exp(m_st - mu_full)
    kw = (kt * wk).astype(BF16)
    c_scr[...] = decay * c_st + _dot(kw, vx)
    return tot + mu_full


def _mlstm_kernel(ql_ref, kl_ref, vl_ref, ol_ref, qc_ref, kc_ref, vc_ref,
                  al_ref, bl_ref, ac_ref, bc_ref, ba_ref, bb_ref,
                  cwq_ref, cwk_ref, cbq_ref, cbk_ref, gn_ref, o_ref,
                  pad_scr, q_scr, kt_scr, vx_scr, rtc_scr, rtl_scr, cf_scr, cb_scr, hf_scr, hb_scr,
                  *, t, tc):
    _gate_rows(ac_ref.at[0], bc_ref.at[0], ba_ref, bb_ref, rtc_scr)
    _gate_rows(al_ref.at[0], bl_ref.at[0], ba_ref, bb_ref, rtl_scr)
    _mlstm_prologue(qc_ref, kc_ref, vc_ref, cwq_ref, cwk_ref, cbq_ref, cbk_ref,
                    pad_scr, q_scr, kt_scr, vx_scr, tc, 0)
    _mlstm_prologue(ql_ref, kl_ref, vl_ref, cwq_ref, cwk_ref, cbq_ref, cbk_ref,
                    pad_scr, q_scr, kt_scr, vx_scr, t, tc // LANES)
    cf_scr[...] = jnp.zeros_like(cf_scr)
    cb_scr[...] = jnp.zeros_like(cb_scr)
    m0 = jnp.zeros((1, 1), F32)

    def group(rt_scr, ng, g0, outs):
        def body(it, carry):
            mf, mb = carry
            gf = it
            gb = ng - 1 - it
            rf = rt_scr[gf]
            rb = rt_scr[gb]
            args = (q_scr, kt_scr, vx_scr)
            mf = _chunk(*args, rf, cf_scr, mf, g0 + gf, 0, False, outs and hf_scr, gf)
            mb = _chunk(*args, rb, cb_scr, mb, g0 + gb, 1, True, outs and hb_scr, gb)
            mf = _chunk(*args, rf, cf_scr, mf, g0 + gf, 1, False, outs and hf_scr, gf)
            mb = _chunk(*args, rb, cb_scr, mb, g0 + gb, 0, True, outs and hb_scr, gb)
            return mf, mb
        return body

    ngc = tc // LANES
    carry = lax.fori_loop(0, ngc, group(rtc_scr, ngc, 0, None), (m0, m0))
    lax.fori_loop(0, t // LANES, group(rtl_scr, t // LANES, ngc, True), carry)

    hm = (hf_scr[...] + hb_scr[...]).reshape(t, LANES)
    ms = jnp.mean(hm * hm, axis=1, keepdims=True)
    out = hm * lax.rsqrt(ms + RMS_EPS) * gn_ref[...] * jax.nn.sigmoid(ol_ref[0].astype(F32))
    o_ref[0] = out.astype(BF16)


def _mlstm(proj_l, proj_c, ga_l, gb_l, ga_c, gb_c, bias_a, bias_b, conv_w, conv_b, gn):
    b, t, _ = proj_l.shape
    tc = proj_c.shape[1]
    h = MLSTM_HEADS
    col = lambda off: (lambda bi, hi: (bi, 0, off + hi))
    q0 = (3 * DIFF_W) // LANES
    gspec = lambda ng: pl.BlockSpec((1, 1, ng, SUBLANES, LANES), lambda bi, hi: (bi, hi, 0, 0, 0))
    hspec = pl.BlockSpec((1, SUBLANES, LANES), lambda bi, hi: (hi, 0, 0))
    cw = jnp.pad(conv_w, ((0, SUBLANES - MLSTM_CONV), (0, 0)))
    cwspec = lambda off: pl.BlockSpec((1, SUBLANES, LANES), lambda bi, hi: (0, 0, off + hi))
    cbspec = lambda off: pl.BlockSpec((1, 1, LANES), lambda bi, hi: (0, 0, off + hi))
    tt = t + tc
    return pl.pallas_call(
        functools.partial(_mlstm_kernel, t=t, tc=tc),
        grid=(b, h),
        in_specs=[pl.BlockSpec((1, t, LANES), col(q0)),
                  pl.BlockSpec((1, t, LANES), col(q0 + h)),
                  pl.BlockSpec((1, t, LANES), col(q0 + 2 * h)),
                  pl.BlockSpec((1, t, LANES), col(q0 + 3 * h)),
                  pl.BlockSpec((1, tc, LANES), col(q0)),
                  pl.BlockSpec((1, tc, LANES), col(q0 + h)),
                  pl.BlockSpec((1, tc, LANES), col(q0 + 2 * h)),
                  gspec(t // LANES), gspec(t // LANES), gspec(tc // LANES), gspec(tc // LANES),
                  hspec, hspec,
                  cwspec(0), cwspec(h), cbspec(0), cbspec(h),
                  pl.BlockSpec((1, LANES), lambda bi, hi: (0, 0))],
        out_specs=pl.BlockSpec((1, t, LANES), lambda bi, hi: (bi, 0, hi)),
        out_shape=jax.ShapeDtypeStruct((b, t, ML_W), BF16),
        scratch_shapes=[pltpu.VMEM((t + 2 * SUBLANES, LANES), F32),
                        pltpu.VMEM((tt // LANES, LANES, LANES), BF16),
                        pltpu.VMEM((tt // LANES, LANES, LANES), F32),
                        pltpu.VMEM((tt // LANES, LANES, 2 * LANES), BF16),
                        pltpu.VMEM((tc // LANES, SUBLANES, LANES), F32),
                        pltpu.VMEM((t // LANES, SUBLANES, LANES), F32),
                        pltpu.VMEM((LANES, 2 * LANES), F32),
                        pltpu.VMEM((LANES, 2 * LANES), F32),
                        pltpu.VMEM((t // LANES, LANES, LANES), F32),
                        pltpu.VMEM((t // LANES, LANES, LANES), F32)],
        compiler_params=_cparams(("arbitrary", "arbitrary")),
        name="mlstm",
    )(proj_l, proj_l, proj_l, proj_l, proj_c, proj_c, proj_c,
      ga_l, gb_l, ga_c, gb_c, bias_a, bias_b,
      cw.reshape(1, SUBLANES, 2 * ML_W), cw.reshape(1, SUBLANES, 2 * ML_W),
      conv_b.reshape(1, 1, 2 * ML_W), conv_b.reshape(1, 1, 2 * ML_W), gn)


def _layer_norm(v, g, b):
    mu = jnp.mean(v, axis=1, keepdims=True)
    c = v - mu
    var = jnp.mean(c * c, axis=1, keepdims=True)
    return c * lax.rsqrt(var + LN_EPS) * g + b


def _outproj_kernel(d_ref, m_ref, w_ref, x_ref, g1_ref, lg_ref, lb_ref, sh_ref, sc_ref,
                    x1_ref, h2_ref):
    y = _dot(d_ref[0], w_ref[0]) + _dot(m_ref[0], w_ref[1])
    v = DEEPNORM_ALPHA * x_ref[0] + g1_ref[0] * y
    x1 = _layer_norm(v, lg_ref[...], lb_ref[...])
    x1_ref[0] = x1
    h2_ref[0] = (x1 * (1.0 + sc_ref[0]) + sh_ref[0]).astype(BF16)


def _out_proj(d_lat, m_lat, w_out2, x, g1, ln_g, ln_b, sh2, sc2, *, tm):
    b, t, d = x.shape
    hw = d_lat.shape[2]
    row = pl.BlockSpec((1, 1, d), lambda bi, i: (bi, 0, 0))
    vec = pl.BlockSpec((1, d), lambda bi, i: (0, 0))
    return pl.pallas_call(
        _outproj_kernel,
        grid=(b, t // tm),
        in_specs=[pl.BlockSpec((1, tm, hw), lambda bi, i: (bi, i, 0)),
                  pl.BlockSpec((1, tm, hw), lambda bi, i: (bi, i, 0)),
                  pl.BlockSpec((2, hw, d), lambda bi, i: (0, 0, 0)),
                  pl.BlockSpec((1, tm, d), lambda bi, i: (bi, i, 0)),
                  row, vec, vec, row, row],
        out_specs=[pl.BlockSpec((1, tm, d), lambda bi, i: (bi, i, 0)),
                   pl.BlockSpec((1, tm, d), lambda bi, i: (bi, i, 0))],
        out_shape=[jax.ShapeDtypeStruct((b, t, d), F32),
                   jax.ShapeDtypeStruct((b, t, d), BF16)],
        compiler_params=_cparams(("arbitrary", "arbitrary")),
        name="out_proj",
    )(d_lat, m_lat, w_out2, x, g1, ln_g, ln_b, sh2, sc2)


def _top_rows(x, k, dst_scr):
    cur = x
    for r in range(k):
        m = jnp.max(cur, axis=0, keepdims=True)
        dst_scr[r:r + 1] = m
        if r + 1 < k:
            cur = jnp.where(cur == m, NEG_INF, cur)


def _peer_prep_kernel(h_ref, wq_ref, k1_ref, k2_ref, n1_ref, r2_ref, e1_ref, e2_ref,
                      v1_scr, v2_scr, tp_scr):
    q = _dot(h_ref[...], wq_ref[...])
    half = PEER_DQ // 2
    for h in range(PEER_HEADS):
        q1 = q[:, h * PEER_DQ:h * PEER_DQ + half].astype(BF16)
        q2 = q[:, h * PEER_DQ + half:(h + 1) * PEER_DQ].astype(BF16)
        s1 = _dot_nt(k1_ref[h], q1)
        s2 = _dot_nt(k2_ref[h], q2)
        _top_rows(s1, PEER_TOPK, v1_scr)
        _top_rows(s2, PEER_TOPK, v2_scr)
        v1 = v1_scr[...]
        v2 = v2_scr[...]
        pieces = [v1[0:1] + v2]
        pieces += [v1[a:a + 1] + v2[0:SUBLANES] for a in range(1, SUBLANES)]
        pieces += [v1[SUBLANES:PEER_TOPK] + v2[0:1]]
        cand = jnp.concatenate(pieces, axis=0)
        _top_rows(cand, PEER_TOPK, tp_scr)
        tp = tp_scr[...]
        top0 = tp[0:1]
        tau = tp[PEER_TOPK - 1:PEER_TOPK]
        z = jnp.sum(jnp.exp(tp - top0), axis=0, keepdims=True)
        cnt = jnp.zeros(s1.shape, F32)
        rank = jnp.zeros(s2.shape, F32)
        for bb in range(PEER_TOPK):
            v2b = v2[bb:bb + 1]
            cnt = cnt + jnp.where(s1 + v2b >= tau, 1.0, 0.0)
            rank = rank + jnp.where(s2 < v2b, 1.0, 0.0)
        n1_ref[h] = cnt
        r2_ref[h] = rank.astype(BF16)
        e1_ref[h] = jnp.exp(s1 - v1[0:1]) / z
        e2_ref[h] = jnp.exp(s2 - v2[0:1]).astype(BF16)


def _peer_prep(h2, wq, k1, k2, *, tt):
    n, d = h2.shape
    hp = PEER_HEADS
    big = jax.ShapeDtypeStruct((hp, PEER_NKEYS, n), F32)
    big16 = jax.ShapeDtypeStruct((hp, PEER_NKEYS, n), BF16)
    bspec = pl.BlockSpec((hp, PEER_NKEYS, tt), lambda i: (0, 0, i))
    kspec = pl.BlockSpec((hp, PEER_NKEYS, PEER_DQ // 2), lambda i: (0, 0, 0))
    return pl.pallas_call(
        _peer_prep_kernel,
        grid=(n // tt,),
        in_specs=[pl.BlockSpec((tt, d), lambda i: (i, 0)),
                  pl.BlockSpec((d, hp * PEER_DQ), lambda i: (0, 0)),
                  kspec, kspec],
        out_specs=[bspec, bspec, bspec, bspec],
        out_shape=[big, big16, big, big16],
        scratch_shapes=[pltpu.VMEM((PEER_TOPK, tt), F32),
                        pltpu.VMEM((PEER_TOPK, tt), F32),
                        pltpu.VMEM((PEER_TOPK, tt), F32)],
        compiler_params=_cparams(("arbitrary",)),
        name="peer_prep",
    )(h2, wq, k1, k2)


def _peer_dense_kernel(h_ref, u_ref, vt_ref, n1_ref, r2_ref, e1_ref, e2_ref, o_ref,
                       ht_scr, z_scr, acc_scr, *, te):
    j = pl.program_id(1)
    nj = pl.num_programs(1)
    nk = PEER_NKEYS

    @pl.when(j == 0)
    def _():
        ht_scr[...] = h_ref[...].astype(F32).T.astype(BF16)
        acc_scr[...] = jnp.zeros_like(acc_scr)
        z_scr[1] = jnp.zeros(z_scr.shape[1:], F32)

    prev = jnp.maximum(j - 1, 0)

    tt = ht_scr.shape[1]
    tn = 2 * LANES

    def body(wslot, rslot):
        for n0 in range(0, tt, tn):
            z_scr[wslot, :, n0:n0 + tn] = _dot(u_ref[...], ht_scr[:, n0:n0 + tn])
            ws = []
            for a in range(te // nk):
                i1 = prev * (te // nk) + a
                g = None
                for h in range(PEER_HEADS):
                    n1row = n1_ref[h, pl.ds(i1, 1), n0:n0 + tn].astype(BF16)
                    e1row = e1_ref[h, pl.ds(i1, 1), n0:n0 + tn].astype(BF16)
                    hit = r2_ref[h, :, n0:n0 + tn] < n1row
                    term = jnp.where(hit, e2_ref[h, :, n0:n0 + tn], jnp.zeros((), BF16)) * e1row
                    g = term if g is None else g + term
                z = z_scr[rslot, a * nk:(a + 1) * nk, n0:n0 + tn]
                gelu = 0.5 * z * (1.0 + lax.erf(z * math.sqrt(0.5)))
                ws.append(gelu.astype(BF16) * g)
            acc_scr[:, n0:n0 + tn] += _dot(vt_ref[...], jnp.concatenate(ws, axis=0))

    @pl.when(j % 2 == 0)
    def _():
        body(0, 1)

    @pl.when(j % 2 == 1)
    def _():
        body(1, 0)

    @pl.when(j == nj - 1)
    def _():
        o_ref[...] = acc_scr[...].T


def _peer_dense(h2, u_bf, vt_bf, n1, r2, e1, e2, *, tt, te):
    n, d = h2.shape
    ne = u_bf.shape[0]
    hp = PEER_HEADS
    nb = ne // te
    bspec = pl.BlockSpec((hp, PEER_NKEYS, tt), lambda i, j: (0, 0, i))
    return pl.pallas_call(
        functools.partial(_peer_dense_kernel, te=te),
        grid=(n // tt, nb + 1),
        in_specs=[pl.BlockSpec((tt, d), lambda i, j: (i, 0)),
                  pl.BlockSpec((te, d), lambda i, j: (jnp.minimum(j, nb - 1), 0)),
                  pl.BlockSpec((d, te), lambda i, j: (0, jnp.maximum(j - 1, 0))),
                  bspec, bspec, bspec, bspec],
        out_specs=pl.BlockSpec((tt, d), lambda i, j: (i, 0)),
        out_shape=jax.ShapeDtypeStruct((n, d), F32),
        scratch_shapes=[pltpu.VMEM((d, tt), BF16),
                        pltpu.VMEM((2, te, tt), F32),
                        pltpu.VMEM((d, tt), F32)],
        compiler_params=_cparams(("arbitrary", "arbitrary")),
        name="peer_dense",
    )(h2, u_bf, vt_bf, n1, r2, e1, e2)


def _final_kernel(x1_ref, p_ref, g2_ref, lg_ref, lb_ref, o_ref):
    v = DEEPNORM_ALPHA * x1_ref[0] + g2_ref[0] * p_ref[0]
    o_ref[0] = _layer_norm(v, lg_ref[...], lb_ref[...])


def _final_ln(x1, peer, g2, ln_g, ln_b, *, tm):
    b, t, d = x1.shape
    blk = pl.BlockSpec((1, tm, d), lambda bi, i: (bi, i, 0))
    vec = pl.BlockSpec((1, d), lambda bi, i: (0, 0))
    return pl.pallas_call(
        _final_kernel,
        grid=(b, t // tm),
        in_specs=[blk, blk, pl.BlockSpec((1, 1, d), lambda bi, i: (bi, 0, 0)), vec, vec],
        out_specs=blk,
        out_shape=jax.ShapeDtypeStruct((b, t, d), F32),
        compiler_params=_cparams(("arbitrary", "arbitrary")),
        name="final_ln",
    )(x1, peer, g2, ln_g, ln_b)


def _rope_tables(t):
    pos = jnp.arange(t, dtype=jnp.int32)
    row = (pos // GRID_W).astype(F32)
    col = (pos % GRID_W).astype(F32)
    inv = ROPE_BASE ** (-jnp.arange(ROPE_NF, dtype=F32) / ROPE_NF)
    lane = jnp.arange(LANES)
    use_col = ((lane // (2 * ROPE_NF)) % 2) == 1
    p = jnp.where(use_col[None, :], col[:, None], row[:, None])
    ang = p * inv[lane % ROPE_NF][None, :]
    sign = jnp.where((lane % (2 * ROPE_NF)) < ROPE_NF, -1.0, 1.0).astype(F32)
    return jnp.cos(ang), jnp.sin(ang) * sign[None, :]


def _gate_layout(g):
    b, t, _ = g.shape
    gt = g[:, :, :N_GATES * MLSTM_HEADS].reshape(b, t // LANES, LANES, N_GATES, MLSTM_HEADS)
    gt = gt.transpose(0, 4, 1, 3, 2)
    zeros = jnp.zeros_like(gt[:, :, :, 0:1])
    a = jnp.concatenate([gt[:, :, :, 0:1], gt[:, :, :, 2:3]] + [zeros] * 6, axis=3)
    fb = jnp.concatenate([gt[:, :, :, 1:2], gt[:, :, :, 3:4]], axis=3)
    return a, jnp.tile(fb, (1, 1, 1, 4, 1))


def _gate_bias(gate_b):
    gb = gate_b.astype(F32)
    z = jnp.zeros_like(gb[0])
    a = jnp.stack([gb[0], gb[2]] + [z] * 6, axis=1)
    bm = jnp.stack([gb[1], gb[3]] * 4, axis=1)
    bc = lambda v: jnp.broadcast_to(v[:, :, None], (MLSTM_HEADS, SUBLANES, LANES))
    return bc(a), bc(bm)


def kernel(x, c, ctx, c_ctx, w_ada, b_ada, w_in, conv_w, conv_b, gate_b, diff_lambda, diff_norm_g,
           mlstm_norm_g, w_out, ln1_g, ln1_b, ln2_g, ln2_b, peer_wq, peer_keys, peer_u, peer_v):
    b, t, d = x.shape
    tc = ctx.shape[1]
    l = 0

    c_rows = jnp.concatenate([c, c_ctx[None, :]], axis=0)
    mod = _adaln(c_rows, w_ada[l], b_ada[l])
    sh1, sc1, g1, sh2, sc2, g2 = [m[:, None, :] for m in jnp.split(mod[:b], 6, axis=1)]
    csh1, csc1 = [m[:, None, :] for m in jnp.split(mod[b:b + 1], 6, axis=1)[:2]]
    csh1 = jnp.broadcast_to(csh1, (b, 1, d))
    csc1 = jnp.broadcast_to(csc1, (b, 1, d))

    w = w_in[l]
    w_main = jnp.concatenate([w[:, :DIFF_W] * (DIFF_HALF ** -0.5), w[:, DIFF_W:MAIN_COLS]], axis=1).astype(BF16)
    wg = jnp.pad(w[:, MAIN_COLS:], ((0, 0), (0, LANES - N_GATES * MLSTM_HEADS)))
    wg_hi = wg.astype(BF16)
    wg_lo = (wg - wg_hi.astype(F32)).astype(BF16)
    wg2 = jnp.stack([wg_hi, wg_lo])
    cos_t, sin_t = _rope_tables(t)
    proj_l, gates_l = _in_proj(x, sh1, sc1, w_main, wg2, cos_t, sin_t, tm=min(1024, t))
    proj_c, gates_c = _in_proj(ctx, csh1, csc1, w_main, wg2,
                               jnp.ones((tc, LANES), F32), jnp.zeros((tc, LANES), F32), tm=tc)

    lam_pad = jnp.pad(diff_lambda[l].astype(F32), ((0, SUBLANES - 4), (0, LANES - DIFF_HALF)))
    d_lat = _diff_attn(proj_l, proj_c, lam_pad, diff_norm_g[l].reshape(1, LANES),
                       tq=min(512, t), tk=min(512, t), rq=256)

    ga_l, gb_l = _gate_layout(gates_l)
    ga_c, gb_c = _gate_layout(gates_c)
    bias_a, bias_b = _gate_bias(gate_b[l])
    m_lat = _mlstm(proj_l, proj_c, ga_l, gb_l, ga_c, gb_c, bias_a, bias_b,
                   conv_w[l], conv_b[l], mlstm_norm_g[l].reshape(1, LANES))

    w_out2 = w_out[l].astype(BF16).reshape(2, DIFF_W, d)
    x1, h2 = _out_proj(d_lat, m_lat, w_out2, x, g1, ln1_g[l].reshape(1, d), ln1_b[l].reshape(1, d),
                       sh2, sc2, tm=min(512, t))

    n = b * t
    h2f = h2.reshape(n, d)
    keys = peer_keys[l].astype(BF16)
    n1, r2, e1, e2 = _peer_prep(h2f, peer_wq[l].astype(BF16), keys[0], keys[1], tt=min(512, n))
    peer = _peer_dense(h2f, peer_u[l].astype(BF16), peer_v[l].T.astype(BF16), n1, r2, e1, e2,
                       tt=min(512, n), te=512)

    return _final_ln(x1, peer.reshape(b, t, d), g2, ln2_g[l].reshape(1, d), ln2_b[l].reshape(1, d),
                     tm=min(512, t))
```

```python
import functools
import math

import jax
import jax.numpy as jnp
from jax import lax
from jax.experimental import pallas as pl
from jax.experimental.pallas import tpu as pltpu

F32 = jnp.float32
BF16 = jnp.bfloat16

LANES = 128
SUBLANES = 8
VMEM_LIMIT = 56 * 1024 * 1024

GRID_W = 64
DIFF_HEADS = 8
DIFF_HALF = 64
DIFF_W = DIFF_HEADS * 2 * DIFF_HALF
MLSTM_HEADS = 8
MLSTM_DHEAD = 128
ML_W = MLSTM_HEADS * MLSTM_DHEAD
MLSTM_CONV = 5
CHUNK = 64
MLSTM_GROUPS_PER_STEP = 2
N_GATES = 4
MAIN_COLS = 3 * DIFF_W + 4 * ML_W
ROPE_COLS = 2 * DIFF_W
ROPE_BASE = 10000.0
ROPE_NF = 16
PEER_HEADS = 8
PEER_NKEYS = 128
PEER_DQ = 256
PEER_TOPK = 16
DEPTH = 1
DEEPNORM_ALPHA = (2.0 * DEPTH) ** 0.25
LN_EPS = 1e-5
RMS_EPS = 1e-6
LAM_INIT = 0.8 - 0.6 * math.exp(-0.3 * 0)
NEG_INF = float("-inf")


def _cparams(sem):
    return pltpu.CompilerParams(dimension_semantics=sem, vmem_limit_bytes=VMEM_LIMIT)


def _dot(a, b):
    return jnp.dot(a, b, preferred_element_type=F32)


def _dot_nt(a, b):
    return lax.dot_general(a, b, (((1,), (1,)), ((), ())), preferred_element_type=F32)


def _split3(x):
    hi = x.astype(BF16)
    r1 = x - hi.astype(F32)
    mid = r1.astype(BF16)
    lo = (r1 - mid.astype(F32)).astype(BF16)
    return hi, mid, lo


def _adaln_kernel(cb_ref, w_ref, b_ref, o_ref, act_scr, *, n_rows):
    @pl.when(pl.program_id(0) == 0)
    def _():
        c = cb_ref[...]
        act_scr[...] = c * jax.nn.sigmoid(c)

    tn = w_ref.shape[1]
    o_ref[...] = jnp.zeros_like(o_ref)
    for cb in range(tn // LANES):
        w = w_ref[:, cb * LANES:(cb + 1) * LANES]
        for r in range(n_rows):
            s = jnp.sum(act_scr[r] * w, axis=0, keepdims=True)
            o_ref[r:r + 1, cb * LANES:(cb + 1) * LANES] = s + b_ref[:, cb * LANES:(cb + 1) * LANES]


def _adaln(c_rows, w_ada, b_ada):
    n_rows, k = c_rows.shape
    n = w_ada.shape[1]
    tn = 512
    cb = jnp.broadcast_to(c_rows[:, :, None], (n_rows, k, LANES))
    return pl.pallas_call(
        functools.partial(_adaln_kernel, n_rows=n_rows),
        grid=(n // tn,),
        in_specs=[pl.BlockSpec((n_rows, k, LANES), lambda j: (0, 0, 0)),
                  pl.BlockSpec((k, tn), lambda j: (0, j)),
                  pl.BlockSpec((1, tn), lambda j: (0, j))],
        out_specs=pl.BlockSpec((SUBLANES, tn), lambda j: (0, j)),
        out_shape=jax.ShapeDtypeStruct((SUBLANES, n), F32),
        scratch_shapes=[pltpu.VMEM((n_rows, k, LANES), F32)],
        compiler_params=_cparams(("arbitrary",)),
        name="adaln",
    )(cb, w_ada, b_ada.reshape(1, n))


def _inproj_kernel(x_ref, sh_ref, sc_ref, w_ref, wg_ref, cos_ref, sin_ref, o_ref, g_ref, h_scr,
                   *, n_rope_tiles):
    j = pl.program_id(2)

    @pl.when(j == 0)
    def _():
        h = x_ref[0] * (1.0 + sc_ref[0]) + sh_ref[0]
        hb = h.astype(BF16)
        h_scr[...] = hb
        hl = (h - hb.astype(F32)).astype(BF16)
        a = _dot(hb, wg_ref[...])
        g_ref[0] = a[:, 0:LANES] + a[:, LANES:2 * LANES] + _dot(hl, wg_ref[:, 0:LANES])

    acc = _dot(h_scr[...], w_ref[...])
    tn = acc.shape[1]

    @pl.when(j < n_rope_tiles)
    def _():
        cos = cos_ref[...]
        sin = sin_ref[...]
        lane = lax.broadcasted_iota(jnp.int32, cos.shape, 1)
        first = (lane % (2 * ROPE_NF)) < ROPE_NF
        for cb in range(tn // LANES):
            a = acc[:, cb * LANES:(cb + 1) * LANES]
            sw = jnp.where(first, pltpu.roll(a, LANES - ROPE_NF, 1), pltpu.roll(a, ROPE_NF, 1))
            o_ref[0, :, cb * LANES:(cb + 1) * LANES] = (a * cos + sw * sin).astype(BF16)

    @pl.when(j >= n_rope_tiles)
    def _():
        o_ref[0] = acc.astype(BF16)


def _in_proj(x, shift, scale, w_main, wg, cos_t, sin_t, *, tm):
    b, t, d = x.shape
    n = w_main.shape[1]
    tn = 512
    return pl.pallas_call(
        functools.partial(_inproj_kernel, n_rope_tiles=ROPE_COLS // tn),
        grid=(b, t // tm, n // tn),
        in_specs=[pl.BlockSpec((1, tm, d), lambda bi, i, j: (bi, i, 0)),
                  pl.BlockSpec((1, 1, d), lambda bi, i, j: (bi, 0, 0)),
                  pl.BlockSpec((1, 1, d), lambda bi, i, j: (bi, 0, 0)),
                  pl.BlockSpec((d, tn), lambda bi, i, j: (0, j)),
                  pl.BlockSpec((d, 2 * LANES), lambda bi, i, j: (0, 0)),
                  pl.BlockSpec((tm, LANES), lambda bi, i, j: (i, 0)),
                  pl.BlockSpec((tm, LANES), lambda bi, i, j: (i, 0))],
        out_specs=[pl.BlockSpec((1, tm, tn), lambda bi, i, j: (bi, i, j)),
                   pl.BlockSpec((1, tm, LANES), lambda bi, i, j: (bi, i, 0))],
        out_shape=[jax.ShapeDtypeStruct((b, t, n), BF16),
                   jax.ShapeDtypeStruct((b, t, LANES), F32)],
        scratch_shapes=[pltpu.VMEM((tm, d), BF16)],
        compiler_params=_cparams(("arbitrary", "arbitrary", "arbitrary")),
        name="in_proj",
    )(x, shift, scale, w_main, wg, cos_t, sin_t)


def _attn_kernel(lam_ref, gn_ref, q_ref, kc_ref, vc_ref, kl_ref, vl_ref, o_ref,
                 q2_scr, vx_scr, m_scr, acc_scr, *, tq, tk, rq):
    tc = kc_ref.shape[1]
    t = kl_ref.shape[1]

    @pl.when(pl.program_id(2) == 0)
    def _():
        lane = lax.broadcasted_iota(jnp.int32, (tc, LANES), 1)
        ones_c = jnp.where(lane == 0, 1.0, 0.0).astype(BF16)
        vx_scr[0:tc, 0:LANES] = vc_ref[0]
        vx_scr[0:tc, LANES:2 * LANES] = ones_c
        for r0 in range(0, t, tc):
            vx_scr[tc + r0:tc + r0 + tc, 0:LANES] = vl_ref[0, r0:r0 + tc, :]
            vx_scr[tc + r0:tc + r0 + tc, LANES:2 * LANES] = ones_c

    q = q_ref[0]
    qf = q.astype(F32)
    lane = lax.broadcasted_iota(jnp.int32, qf.shape, 1)
    q2_scr[0:tq] = jnp.where(lane < DIFF_HALF, qf, 0.0).astype(BF16)
    q2_scr[tq:2 * tq] = jnp.where(lane >= DIFF_HALF, qf, 0.0).astype(BF16)
    m_scr[...] = jnp.full_like(m_scr, NEG_INF)
    acc_scr[...] = jnp.zeros_like(acc_scr)

    def step(k, vx):
        nkc = k.shape[0] // LANES
        for r0 in range(0, 2 * tq, rq):
            sc = _dot_nt(q2_scr[r0:r0 + rq], k)
            cols = [sc[:, c * LANES:(c + 1) * LANES] for c in range(nkc)]
            mx = cols[0]
            for c in range(1, nkc):
                mx = jnp.maximum(mx, cols[c])
            m_prev = m_scr[r0:r0 + rq]
            m_new = jnp.maximum(m_prev, jnp.max(mx, axis=1, keepdims=True))
            alpha = jnp.exp(m_prev - m_new)
            p = jnp.concatenate([jnp.exp(cb - m_new).astype(BF16) for cb in cols], axis=1)
            alpha2 = jnp.concatenate([alpha, alpha], axis=1)
            acc_scr[r0:r0 + rq] = alpha2 * acc_scr[r0:r0 + rq] + _dot(p, vx)
            m_scr[r0:r0 + rq] = m_new

    step(kc_ref[0], vx_scr[0:tc])

    def body(j, carry):
        r = pl.multiple_of(j * tk, tk)
        step(kl_ref[0, pl.ds(r, tk), :], vx_scr[pl.ds(tc + r, tk)])
        return carry

    lax.fori_loop(0, t // tk, body, 0)

    lm = lam_ref[...]
    d1 = jnp.sum(lm[0:1] * lm[1:2], axis=1, keepdims=True)
    d2 = jnp.sum(lm[2:3] * lm[3:4], axis=1, keepdims=True)
    lam = jnp.exp(d1) - jnp.exp(d2) + LAM_INIT
    a1 = acc_scr[0:tq]
    a2 = acc_scr[tq:2 * tq]
    o1 = a1[:, 0:LANES] / a1[:, LANES:LANES + 1]
    o2 = a2[:, 0:LANES] / a2[:, LANES:LANES + 1]
    o = o1 - lam * o2
    ms = jnp.mean(o * o, axis=1, keepdims=True)
    o = o * lax.rsqrt(ms + RMS_EPS) * gn_ref[...] * (1.0 - LAM_INIT)
    o_ref[0] = o.astype(BF16)


def _diff_attn(proj_l, proj_c, lam_pad, gn, *, tq, tk, rq):
    b, t, _ = proj_l.shape
    tc = proj_c.shape[1]
    h = DIFF_HEADS
    kv = lambda off: (lambda bi, hi, qi: (bi, 0, off + hi))
    return pl.pallas_call(
        functools.partial(_attn_kernel, tq=tq, tk=tk, rq=rq),
        grid=(b, h, t // tq),
        in_specs=[pl.BlockSpec((SUBLANES, LANES), lambda bi, hi, qi: (0, 0)),
                  pl.BlockSpec((1, LANES), lambda bi, hi, qi: (0, 0)),
                  pl.BlockSpec((1, tq, LANES), lambda bi, hi, qi: (bi, qi, hi)),
                  pl.BlockSpec((1, tc, LANES), kv(h)),
                  pl.BlockSpec((1, tc, LANES), kv(2 * h)),
                  pl.BlockSpec((1, t, LANES), kv(h)),
                  pl.BlockSpec((1, t, LANES), kv(2 * h))],
        out_specs=pl.BlockSpec((1, tq, LANES), lambda bi, hi, qi: (bi, qi, hi)),
        out_shape=jax.ShapeDtypeStruct((b, t, DIFF_W), BF16),
        scratch_shapes=[pltpu.VMEM((2 * tq, LANES), BF16),
                        pltpu.VMEM((tc + t, 2 * LANES), BF16),
                        pltpu.VMEM((2 * tq, LANES), F32),
                        pltpu.VMEM((2 * tq, 2 * LANES), F32)],
        compiler_params=_cparams(("arbitrary",) * 3),
        name="diff_attn",
    )(lam_pad, gn, proj_l, proj_c, proj_c, proj_l, proj_l)


def _gate_rows(a_ref, b_ref, ba_ref, bb_ref, rt_scr):
    ng = a_ref.shape[1]
    x = (a_ref[0] + ba_ref[0]).reshape(ng * SUBLANES, LANES)
    ls = jax.nn.log_sigmoid(b_ref[0] + bb_ref[0]).reshape(ng * SUBLANES, LANES)
    ji = lax.broadcasted_iota(jnp.int32, (LANES, LANES), 0)
    si = lax.broadcasted_iota(jnp.int32, (LANES, LANES), 1)
    same = (ji // CHUNK) == (si // CHUNK)
    ones_where = lambda cond: jnp.where(cond, 1.0, 0.0).astype(BF16)
    m_pre = ones_where(same & (ji <= si))
    m_suf = ones_where(same & (ji >= si))
    m_tot = ones_where(same)
    hi, mid, lo = _split3(ls)
    mm = lambda m: _dot(hi, m) + _dot(mid, m) + _dot(lo, m)
    sub = lax.broadcasted_iota(jnp.int32, x.shape, 0) % SUBLANES
    cum = jnp.where(sub % 2 == 0, mm(m_pre), mm(m_suf))
    rt = jnp.where(sub < 2, x - cum, jnp.where(sub < 4, cum, mm(m_tot)))
    rt_scr[...] = rt.reshape(ng, SUBLANES, LANES)


def _conv_silu(raw_ref, w, bias, pad_scr, t, emit):
    pad_scr[0:SUBLANES] = jnp.zeros((SUBLANES, LANES), F32)
    pad_scr[SUBLANES + t:2 * SUBLANES + t] = jnp.zeros((SUBLANES, LANES), F32)
    pad_scr[SUBLANES:SUBLANES + t] = raw_ref[0].astype(F32)
    half = MLSTM_CONV // 2
    for g in range(t // LANES):
        acc = jnp.zeros((LANES, LANES), F32) + bias
        for j in range(MLSTM_CONV):
            off = SUBLANES + g * LANES + j - half
            acc = acc + pad_scr[off:off + LANES] * w[j:j + 1]
        emit(g, acc * jax.nn.sigmoid(acc))


def _mlstm_prologue(q_ref, k_ref, v_ref, cwq_ref, cwk_ref, cbq_ref, cbk_ref,
                    pad_scr, q_scr, kt_scr, vx_scr, t, g0):
    def emit_q(g, blk):
        q_scr[g0 + g] = blk.astype(BF16)

    def emit_k(g, blk):
        kt_scr[g0 + g] = (blk * (MLSTM_DHEAD ** -0.5)).T

    _conv_silu(q_ref, cwq_ref[0], cbq_ref[0], pad_scr, t, emit_q)
    _conv_silu(k_ref, cwk_ref[0], cbk_ref[0], pad_scr, t, emit_k)
    lane = lax.broadcasted_iota(jnp.int32, (LANES, LANES), 1)
    ones_col = jnp.where(lane == 0, 1.0, 0.0).astype(BF16)
    for g in range(t // LANES):
        vx_scr[g0 + g, :, 0:LANES] = v_ref[0, g * LANES:(g + 1) * LANES, :]
        vx_scr[g0 + g, :, LANES:2 * LANES] = ones_col


def _groups(q_scr, kt_scr, vx_scr, c_scr, m_st, items, bwd, out_scr):
    d = 1 if bwd else 0
    order = (1, 0) if bwd else (0, 1)
    lane = lax.broadcasted_iota(jnp.int32, (1, LANES), 1)
    in_c = (lane < CHUNK, lane >= CHUNK)
    li = lax.broadcasted_iota(jnp.int32, (LANES, LANES), 0)
    si = lax.broadcasted_iota(jnp.int32, (LANES, LANES), 1)
    same = (li < CHUNK) == (si < CHUNK)
    mask = same & ((si >= li) if bwd else (si <= li))
    row0 = lax.broadcasted_iota(jnp.int32, (LANES, 1), 0) < CHUNK

    st = []
    for g, rt, og in items:
        r = rt[d:d + 1, :]
        cum = rt[2 + d:3 + d, :]
        totrow = rt[4 + d:5 + d, :]
        maxr = [jnp.max(jnp.where(in_c[c], r, NEG_INF), axis=1, keepdims=True) for c in (0, 1)]
        tot = [jnp.max(jnp.where(in_c[c], totrow, NEG_INF), axis=1, keepdims=True) for c in (0, 1)]
        m_b, mu_f = [None, None], [None, None]
        for c in order:
            m_b[c] = m_st
            mu_f[c] = jnp.maximum(m_st, maxr[c])
            m_st = tot[c] + mu_f[c]
        st.append(dict(g=g, og=og, r=r, cum=cum, m_b=m_b, mu_f=mu_f))
    for e in st:
        kt = kt_scr[e["g"]]
        vx = vx_scr[e["g"]]
        e["kt"], e["vx"] = kt, vx
        e["kv"] = [_dot((kt * jnp.where(in_c[c], jnp.exp(e["r"] - e["mu_f"][c]), 0.0)).astype(BF16), vx)
                   for c in (0, 1)]
        if out_scr is not None:
            e["q"] = q_scr[e["g"]]
            e["sc"] = _dot(e["q"], kt.astype(BF16))
    c_st = c_scr[...]
    for e in st:
        e["c_bf"] = [None, None]
        for c in order:
            e["c_bf"][c] = c_st.astype(BF16)
            c_st = jnp.exp(e["m_b"][c] - e["mu_f"][c]) * c_st + e["kv"][c]
    c_scr[...] = c_st
    if out_scr is not None:
        for e in st:
            m_col = jnp.where(row0, e["m_b"][0], e["m_b"][1])
            rm = jnp.where(mask, e["r"], NEG_INF)
            mu = jnp.maximum(m_col, jnp.max(rm, axis=1, keepdims=True))
            e["mu"], e["m_col"] = mu, m_col
            e["s"] = (e["sc"] * jnp.exp(rm - mu)).astype(BF16)
        for e in st:
            e["intra"] = _dot(e["s"], e["vx"])
            e["inter"] = [_dot(e["q"][c * CHUNK:(c + 1) * CHUNK], e["c_bf"][c]) for c in (0, 1)]
        for e in st:
            inter = jnp.concatenate(e["inter"], axis=0)
            tot_o = e["intra"] + jnp.exp(e["m_col"] - e["mu"]) * inter
            num = tot_o[:, 0:LANES]
            den = tot_o[:, LANES:LANES + 1]
            cum_col = jnp.sum(jnp.where(li == si, e["cum"], 0.0), axis=1, keepdims=True)
            floor = jnp.exp(-(cum_col + e["mu"]))
            out_scr[e["og"]] = num / jnp.maximum(jnp.abs(den), floor)
    return m_st


def _mlstm_kernel(ql_ref, kl_ref, vl_ref, ol_ref, qc_ref, kc_ref, vc_ref,
                  al_ref, bl_ref, ac_ref, bc_ref, ba_ref, bb_ref,
                  cwq_ref, cwk_ref, cbq_ref, cbk_ref, gn_ref, o_ref,
                  pad_scr, q_scr, kt_scr, vx_scr, rtc_scr, rtl_scr, cf_scr, cb_scr, hf_scr, hb_scr,
                  *, t, tc, gpi):
    _gate_rows(ac_ref.at[0], bc_ref.at[0], ba_ref, bb_ref, rtc_scr)
    _gate_rows(al_ref.at[0], bl_ref.at[0], ba_ref, bb_ref, rtl_scr)
    _mlstm_prologue(qc_ref, kc_ref, vc_ref, cwq_ref, cwk_ref, cbq_ref, cbk_ref,
                    pad_scr, q_scr, kt_scr, vx_scr, tc, 0)
    _mlstm_prologue(ql_ref, kl_ref, vl_ref, cwq_ref, cwk_ref, cbq_ref, cbk_ref,
                    pad_scr, q_scr, kt_scr, vx_scr, t, tc // LANES)
    cf_scr[...] = jnp.zeros_like(cf_scr)
    cb_scr[...] = jnp.zeros_like(cb_scr)
    m0 = jnp.zeros((1, 1), F32)

    def group(rt_scr, ng, g0, outs):
        def body(it, carry):
            mf, mb = carry
            items_f, items_b = [], []
            for u in range(gpi):
                gf = it * gpi + u
                gb = ng - 1 - gf
                items_f.append((g0 + gf, rt_scr[gf], gf))
                items_b.append((g0 + gb, rt_scr[gb], gb))
            args = (q_scr, kt_scr, vx_scr)
            mf = _groups(*args, cf_scr, mf, items_f, False, outs and hf_scr)
            mb = _groups(*args, cb_scr, mb, items_b, True, outs and hb_scr)
            return mf, mb
        return body

    ngc = tc // LANES
    ngl = t // LANES
    carry = lax.fori_loop(0, ngc // gpi, group(rtc_scr, ngc, 0, None), (m0, m0))
    lax.fori_loop(0, ngl // gpi, group(rtl_scr, ngl, ngc, True), carry)

    hm = (hf_scr[...] + hb_scr[...]).reshape(t, LANES)
    ms = jnp.mean(hm * hm, axis=1, keepdims=True)
    out = hm * lax.rsqrt(ms + RMS_EPS) * gn_ref[...] * jax.nn.sigmoid(ol_ref[0].astype(F32))
    o_ref[0] = out.astype(BF16)


def _mlstm(proj_l, proj_c, ga_l, gb_l, ga_c, gb_c, bias_a, bias_b, conv_w, conv_b, gn):
    b, t, _ = proj_l.shape
    tc = proj_c.shape[1]
    h = MLSTM_HEADS
    col = lambda off: (lambda bi, hi: (bi, 0, off + hi))
    q0 = (3 * DIFF_W) // LANES
    gspec = lambda ng: pl.BlockSpec((1, 1, ng, SUBLANES, LANES), lambda bi, hi: (bi, hi, 0, 0, 0))
    hspec = pl.BlockSpec((1, SUBLANES, LANES), lambda bi, hi: (hi, 0, 0))
    cw = jnp.pad(conv_w, ((0, SUBLANES - MLSTM_CONV), (0, 0)))
    cwspec = lambda off: pl.BlockSpec((1, SUBLANES, LANES), lambda bi, hi: (0, 0, off + hi))
    cbspec = lambda off: pl.BlockSpec((1, 1, LANES), lambda bi, hi: (0, 0, off + hi))
    tt = t + tc
    return pl.pallas_call(
        functools.partial(_mlstm_kernel, t=t, tc=tc, gpi=MLSTM_GROUPS_PER_STEP),
        grid=(b, h),
        in_specs=[pl.BlockSpec((1, t, LANES), col(q0)),
                  pl.BlockSpec((1, t, LANES), col(q0 + h)),
                  pl.BlockSpec((1, t, LANES), col(q0 + 2 * h)),
                  pl.BlockSpec((1, t, LANES), col(q0 + 3 * h)),
                  pl.BlockSpec((1, tc, LANES), col(q0)),
                  pl.BlockSpec((1, tc, LANES), col(q0 + h)),
                  pl.BlockSpec((1, tc, LANES), col(q0 + 2 * h)),
                  gspec(t // LANES), gspec(t // LANES), gspec(tc // LANES), gspec(tc // LANES),
                  hspec, hspec,
                  cwspec(0), cwspec(h), cbspec(0), cbspec(h),
                  pl.BlockSpec((1, LANES), lambda bi, hi: (0, 0))],
        out_specs=pl.BlockSpec((1, t, LANES), lambda bi, hi: (bi, 0, hi)),
        out_shape=jax.ShapeDtypeStruct((b, t, ML_W), BF16),
        scratch_shapes=[pltpu.VMEM((t + 2 * SUBLANES, LANES), F32),
                        pltpu.VMEM((tt // LANES, LANES, LANES), BF16),
                        pltpu.VMEM((tt // LANES, LANES, LANES), F32),
                        pltpu.VMEM((tt // LANES, LANES, 2 * LANES), BF16),
                        pltpu.VMEM((tc // LANES, SUBLANES, LANES), F32),
                        pltpu.VMEM((t // LANES, SUBLANES, LANES), F32),
                        pltpu.VMEM((LANES, 2 * LANES), F32),
                        pltpu.VMEM((LANES, 2 * LANES), F32),
                        pltpu.VMEM((t // LANES, LANES, LANES), F32),
                        pltpu.VMEM((t // LANES, LANES, LANES), F32)],
        compiler_params=_cparams(("arbitrary", "arbitrary")),
        name="mlstm",
    )(proj_l, proj_l, proj_l, proj_l, proj_c, proj_c, proj_c,
      ga_l, gb_l, ga_c, gb_c, bias_a, bias_b,
      cw.reshape(1, SUBLANES, 2 * ML_W), cw.reshape(1, SUBLANES, 2 * ML_W),
      conv_b.reshape(1, 1, 2 * ML_W), conv_b.reshape(1, 1, 2 * ML_W), gn)


def _layer_norm(v, g, b):
    mu = jnp.mean(v, axis=1, keepdims=True)
    c = v - mu
    var = jnp.mean(c * c, axis=1, keepdims=True)
    return c * lax.rsqrt(var + LN_EPS) * g + b


def _outproj_kernel(d_ref, m_ref, w_ref, x_ref, g1_ref, lg_ref, lb_ref, sh_ref, sc_ref,
                    x1_ref, h2_ref):
    y = _dot(d_ref[0], w_ref[0]) + _dot(m_ref[0], w_ref[1])
    v = DEEPNORM_ALPHA * x_ref[0] + g1_ref[0] * y
    x1 = _layer_norm(v, lg_ref[...], lb_ref[...])
    x1_ref[0] = x1
    h2_ref[0] = (x1 * (1.0 + sc_ref[0]) + sh_ref[0]).astype(BF16)


def _out_proj(d_lat, m_lat, w_out2, x, g1, ln_g, ln_b, sh2, sc2, *, tm):
    b, t, d = x.shape
    hw = d_lat.shape[2]
    row = pl.BlockSpec((1, 1, d), lambda bi, i: (bi, 0, 0))
    vec = pl.BlockSpec((1, d), lambda bi, i: (0, 0))
    return pl.pallas_call(
        _outproj_kernel,
        grid=(b, t // tm),
        in_specs=[pl.BlockSpec((1, tm, hw), lambda bi, i: (bi, i, 0)),
                  pl.BlockSpec((1, tm, hw), lambda bi, i: (bi, i, 0)),
                  pl.BlockSpec((2, hw, d), lambda bi, i: (0, 0, 0)),
                  pl.BlockSpec((1, tm, d), lambda bi, i: (bi, i, 0)),
                  row, vec, vec, row, row],
        out_specs=[pl.BlockSpec((1, tm, d), lambda bi, i: (bi, i, 0)),
                   pl.BlockSpec((1, tm, d), lambda bi, i: (bi, i, 0))],
        out_shape=[jax.ShapeDtypeStruct((b, t, d), F32),
                   jax.ShapeDtypeStruct((b, t, d), BF16)],
        compiler_params=_cparams(("arbitrary", "arbitrary")),
        name="out_proj",
    )(d_lat, m_lat, w_out2, x, g1, ln_g, ln_b, sh2, sc2)


def _top_rows(x, k, dst_scr):
    cur = x
    for r in range(k):
        m = jnp.max(cur, axis=0, keepdims=True)
        dst_scr[r:r + 1] = m
        if r + 1 < k:
            cur = jnp.where(cur == m, NEG_INF, cur)


def _sort16_pairs():
    def merge(lo, hi, r):
        step = r * 2
        if step < hi - lo:
            yield from merge(lo, hi, step)
            yield from merge(lo + r, hi, step)
            yield from [(i, i + r) for i in range(lo + r, hi - r, step)]
        else:
            yield (lo, lo + r)

    def sort(lo, hi):
        if hi - lo >= 1:
            mid = lo + (hi - lo) // 2
            yield from sort(lo, mid)
            yield from sort(mid + 1, hi)
            yield from merge(lo, hi, 1)

    return list(sort(0, 15))


def _top16_of_128(x, dst_scr):
    k = PEER_TOPK
    y = [x[r * SUBLANES:(r + 1) * SUBLANES] for r in range(k)]
    for a, b in _sort16_pairs():
        y[a], y[b] = jnp.maximum(y[a], y[b]), jnp.minimum(y[a], y[b])
    for r in range(k):
        m = jnp.max(y[0], axis=0, keepdims=True)
        dst_scr[r:r + 1] = m
        if r + 1 < k:
            hit = y[0] == m
            for i in range(k - 1 - r):
                y[i] = jnp.where(hit, y[i + 1], y[i])


def _peer_prep_kernel(h_ref, wq_ref, k1_ref, k2_ref, n1_ref, r2_ref, e1_ref, e2_ref,
                      v1_scr, v2_scr, tp_scr):
    q = _dot(h_ref[...], wq_ref[...])
    half = PEER_DQ // 2
    for h in range(PEER_HEADS):
        q1 = q[:, h * PEER_DQ:h * PEER_DQ + half].astype(BF16)
        q2 = q[:, h * PEER_DQ + half:(h + 1) * PEER_DQ].astype(BF16)
        s1 = _dot_nt(k1_ref[h], q1)
        s2 = _dot_nt(k2_ref[h], q2)
        _top16_of_128(s1, v1_scr)
        _top16_of_128(s2, v2_scr)
        v1 = v1_scr[...]
        v2 = v2_scr[...]
        pieces = [v1[0:1] + v2]
        pieces += [v1[a:a + 1] + v2[0:SUBLANES] for a in range(1, SUBLANES)]
        pieces += [v1[SUBLANES:PEER_TOPK] + v2[0:1]]
        cand = jnp.concatenate(pieces, axis=0)
        _top_rows(cand, PEER_TOPK, tp_scr)
        tp = tp_scr[...]
        top0 = tp[0:1]
        tau = tp[PEER_TOPK - 1:PEER_TOPK]
        z = jnp.sum(jnp.exp(tp - top0), axis=0, keepdims=True)
        cnt = jnp.zeros(s1.shape, F32)
        rank = jnp.zeros(s2.shape, F32)
        for bb in range(PEER_TOPK):
            n_a = jnp.sum(jnp.where(v1[bb:bb + 1] + v2 >= tau, 1.0, 0.0), axis=0, keepdims=True)
            cnt = jnp.where(s1 == v1[bb:bb + 1], n_a, cnt)
            rank = rank + jnp.where(s2 < v2[bb:bb + 1], 1.0, 0.0)
        n1_ref[h] = cnt
        r2_ref[h] = rank.astype(BF16)
        e1_ref[h] = jnp.exp(s1 - v1[0:1]) / z
        e2_ref[h] = jnp.exp(s2 - v2[0:1]).astype(BF16)


def _peer_prep(h2, wq, k1, k2, *, tt):
    n, d = h2.shape
    hp = PEER_HEADS
    big = jax.ShapeDtypeStruct((hp, PEER_NKEYS, n), F32)
    big16 = jax.ShapeDtypeStruct((hp, PEER_NKEYS, n), BF16)
    bspec = pl.BlockSpec((hp, PEER_NKEYS, tt), lambda i: (0, 0, i))
    kspec = pl.BlockSpec((hp, PEER_NKEYS, PEER_DQ // 2), lambda i: (0, 0, 0))
    return pl.pallas_call(
        _peer_prep_kernel,
        grid=(n // tt,),
        in_specs=[pl.BlockSpec((tt, d), lambda i: (i, 0)),
                  pl.BlockSpec((d, hp * PEER_DQ), lambda i: (0, 0)),
                  kspec, kspec],
        out_specs=[bspec, bspec, bspec, bspec],
        out_shape=[big, big16, big, big16],
        scratch_shapes=[pltpu.VMEM((PEER_TOPK, tt), F32),
                        pltpu.VMEM((PEER_TOPK, tt), F32),
                        pltpu.VMEM((PEER_TOPK, tt), F32)],
        compiler_params=_cparams(("arbitrary",)),
        name="peer_prep",
    )(h2, wq, k1, k2)


def _peer_dense_kernel(h_ref, u_ref, vt_ref, n1_ref, r2_ref, e1_ref, e2_ref, o_ref,
                       ht_scr, z_scr, acc_scr, *, te):
    j = pl.program_id(1)
    nj = pl.num_programs(1)
    nk = PEER_NKEYS

    @pl.when(j == 0)
    def _():
        ht_scr[...] = h_ref[...].astype(F32).T.astype(BF16)
        acc_scr[...] = jnp.zeros_like(acc_scr)
        z_scr[1] = jnp.zeros(z_scr.shape[1:], F32)

    prev = jnp.maximum(j - 1, 0)

    tt = ht_scr.shape[1]
    tn = 2 * LANES

    def body(wslot, rslot):
        for n0 in range(0, tt, tn):
            z_scr[wslot, :, n0:n0 + tn] = _dot(u_ref[...], ht_scr[:, n0:n0 + tn])
            ws = []
            for a in range(te // nk):
                i1 = prev * (te // nk) + a
                g = None
                for h in range(PEER_HEADS):
                    n1row = n1_ref[h, pl.ds(i1, 1), n0:n0 + tn].astype(BF16)
                    e1row = e1_ref[h, pl.ds(i1, 1), n0:n0 + tn].astype(BF16)
                    hit = r2_ref[h, :, n0:n0 + tn] < n1row
                    term = jnp.where(hit, e2_ref[h, :, n0:n0 + tn], jnp.zeros((), BF16)) * e1row
                    g = term if g is None else g + term
                z = z_scr[rslot, a * nk:(a + 1) * nk, n0:n0 + tn]
                gelu = 0.5 * z * (1.0 + lax.erf(z * math.sqrt(0.5)))
                ws.append(gelu.astype(BF16) * g)
            acc_scr[:, n0:n0 + tn] += _dot(vt_ref[...], jnp.concatenate(ws, axis=0))

    @pl.when(j % 2 == 0)
    def _():
        body(0, 1)

    @pl.when(j % 2 == 1)
    def _():
        body(1, 0)

    @pl.when(j == nj - 1)
    def _():
        o_ref[...] = acc_scr[...].T


def _peer_dense(h2, u_bf, vt_bf, n1, r2, e1, e2, *, tt, te):
    n, d = h2.shape
    ne = u_bf.shape[0]
    hp = PEER_HEADS
    nb = ne // te
    bspec = pl.BlockSpec((hp, PEER_NKEYS, tt), lambda i, j: (0, 0, i))
    return pl.pallas_call(
        functools.partial(_peer_dense_kernel, te=te),
        grid=(n // tt, nb + 1),
        in_specs=[pl.BlockSpec((tt, d), lambda i, j: (i, 0)),
                  pl.BlockSpec((te, d), lambda i, j: (jnp.minimum(j, nb - 1), 0)),
                  pl.BlockSpec((d, te), lambda i, j: (0, jnp.maximum(j - 1, 0))),
                  bspec, bspec, bspec, bspec],
        out_specs=pl.BlockSpec((tt, d), lambda i, j: (i, 0)),
        out_shape=jax.ShapeDtypeStruct((n, d), F32),
        scratch_shapes=[pltpu.VMEM((d, tt), BF16),
                        pltpu.VMEM((2, te, tt), F32),
                        pltpu.VMEM((d, tt), F32)],
        compiler_params=_cparams(("arbitrary", "arbitrary")),
        name="peer_dense",
    )(h2, u_bf, vt_bf, n1, r2, e1, e2)


def _final_kernel(x1_ref, p_ref, g2_ref, lg_ref, lb_ref, o_ref):
    v = DEEPNORM_ALPHA * x1_ref[0] + g2_ref[0] * p_ref[0]
    o_ref[0] = _layer_norm(v, lg_ref[...], lb_ref[...])


def _final_ln(x1, peer, g2, ln_g, ln_b, *, tm):
    b, t, d = x1.shape
    blk = pl.BlockSpec((1, tm, d), lambda bi, i: (bi, i, 0))
    vec = pl.BlockSpec((1, d), lambda bi, i: (0, 0))
    return pl.pallas_call(
        _final_kernel,
        grid=(b, t // tm),
        in_specs=[blk, blk, pl.BlockSpec((1, 1, d), lambda bi, i: (bi, 0, 0)), vec, vec],
        out_specs=blk,
        out_shape=jax.ShapeDtypeStruct((b, t, d), F32),
        compiler_params=_cparams(("arbitrary", "arbitrary")),
        name="final_ln",
    )(x1, peer, g2, ln_g, ln_b)


def _rope_tables(t):
    pos = jnp.arange(t, dtype=jnp.int32)
    row = (pos // GRID_W).astype(F32)
    col = (pos % GRID_W).astype(F32)
    inv = ROPE_BASE ** (-jnp.arange(ROPE_NF, dtype=F32) / ROPE_NF)
    lane = jnp.arange(LANES)
    use_col = ((lane // (2 * ROPE_NF)) % 2) == 1
    p = jnp.where(use_col[None, :], col[:, None], row[:, None])
    ang = p * inv[lane % ROPE_NF][None, :]
    sign = jnp.where((lane % (2 * ROPE_NF)) < ROPE_NF, -1.0, 1.0).astype(F32)
    return jnp.cos(ang), jnp.sin(ang) * sign[None, :]


def _gate_layout(g):
    b, t, _ = g.shape
    gt = g[:, :, :N_GATES * MLSTM_HEADS].reshape(b, t // LANES, LANES, N_GATES, MLSTM_HEADS)
    gt = gt.transpose(0, 4, 1, 3, 2)
    zeros = jnp.zeros_like(gt[:, :, :, 0:1])
    a = jnp.concatenate([gt[:, :, :, 0:1], gt[:, :, :, 2:3]] + [zeros] * 6, axis=3)
    fb = jnp.concatenate([gt[:, :, :, 1:2], gt[:, :, :, 3:4]], axis=3)
    return a, jnp.tile(fb, (1, 1, 1, 4, 1))


def _gate_bias(gate_b):
    gb = gate_b.astype(F32)
    z = jnp.zeros_like(gb[0])
    a = jnp.stack([gb[0], gb[2]] + [z] * 6, axis=1)
    bm = jnp.stack([gb[1], gb[3]] * 4, axis=1)
    bc = lambda v: jnp.broadcast_to(v[:, :, None], (MLSTM_HEADS, SUBLANES, LANES))
    return bc(a), bc(bm)


def kernel(x, c, ctx, c_ctx, w_ada, b_ada, w_in, conv_w, conv_b, gate_b, diff_lambda, diff_norm_g,
           mlstm_norm_g, w_out, ln1_g, ln1_b, ln2_g, ln2_b, peer_wq, peer_keys, peer_u, peer_v):
    b, t, d = x.shape
    tc = ctx.shape[1]
    l = 0

    c_rows = jnp.concatenate([c, c_ctx[None, :]], axis=0)
    mod = _adaln(c_rows, w_ada[l], b_ada[l])
    sh1, sc1, g1, sh2, sc2, g2 = [m[:, None, :] for m in jnp.split(mod[:b], 6, axis=1)]
    csh1, csc1 = [m[:, None, :] for m in jnp.split(mod[b:b + 1], 6, axis=1)[:2]]
    csh1 = jnp.broadcast_to(csh1, (b, 1, d))
    csc1 = jnp.broadcast_to(csc1, (b, 1, d))

    w = w_in[l]
    w_main = jnp.concatenate([w[:, :DIFF_W] * (DIFF_HALF ** -0.5), w[:, DIFF_W:MAIN_COLS]], axis=1).astype(BF16)
    wg = jnp.pad(w[:, MAIN_COLS:], ((0, 0), (0, LANES - N_GATES * MLSTM_HEADS)))
    wg_hi = wg.astype(BF16)
    wg_lo = (wg - wg_hi.astype(F32)).astype(BF16)
    wg2 = jnp.concatenate([wg_hi, wg_lo], axis=1)
    cos_t, sin_t = _rope_tables(t)
    proj_l, gates_l = _in_proj(x, sh1, sc1, w_main, wg2, cos_t, sin_t, tm=min(1024, t))
    proj_c, gates_c = _in_proj(ctx, csh1, csc1, w_main, wg2,
                               jnp.ones((tc, LANES), F32), jnp.zeros((tc, LANES), F32), tm=tc)

    lam_pad = jnp.pad(diff_lambda[l].astype(F32), ((0, SUBLANES - 4), (0, LANES - DIFF_HALF)))
    d_lat = _diff_attn(proj_l, proj_c, lam_pad, diff_norm_g[l].reshape(1, LANES),
                       tq=min(1024, t), tk=min(2048, t), rq=256)

    ga_l, gb_l = _gate_layout(gates_l)
    ga_c, gb_c = _gate_layout(gates_c)
    bias_a, bias_b = _gate_bias(gate_b[l])
    m_lat = _mlstm(proj_l, proj_c, ga_l, gb_l, ga_c, gb_c, bias_a, bias_b,
                   conv_w[l], conv_b[l], mlstm_norm_g[l].reshape(1, LANES))

    w_out2 = w_out[l].astype(BF16).reshape(2, DIFF_W, d)
    x1, h2 = _out_proj(d_lat, m_lat, w_out2, x, g1, ln1_g[l].reshape(1, d), ln1_b[l].reshape(1, d),
                       sh2, sc2, tm=min(512, t))

    n = b * t
    h2f = h2.reshape(n, d)
    keys = peer_keys[l].astype(BF16)
    n1, r2, e1, e2 = _peer_prep(h2f, peer_wq[l].astype(BF16), keys[0], keys[1], tt=min(512, n))
    peer = _peer_dense(h2f, peer_u[l].astype(BF16), peer_v[l].T.astype(BF16), n1, r2, e1, e2,
                       tt=min(512, n), te=512)

    return _final_ln(x1, peer.reshape(b, t, d), g2, ln2_g[l].reshape(1, d), ln2_b[l].reshape(1, d),
                     tm=min(512, t))
```

```python
import functools
import math

import jax
import jax.numpy as jnp
from jax import lax
from jax.experimental import pallas as pl
from jax.experimental.pallas import tpu as pltpu

F32 = jnp.float32
BF16 = jnp.bfloat16

LANES = 128
SUBLANES = 8
VMEM_LIMIT = 56 * 1024 * 1024

GRID_W = 64
DIFF_HEADS = 8
DIFF_HALF = 64
DIFF_W = DIFF_HEADS * 2 * DIFF_HALF
MLSTM_HEADS = 8
MLSTM_DHEAD = 128
ML_W = MLSTM_HEADS * MLSTM_DHEAD
MLSTM_CONV = 5
CHUNK = 64
MLSTM_GROUPS_PER_STEP = 4
N_GATES = 4
MAIN_COLS = 3 * DIFF_W + 4 * ML_W
ROPE_COLS = 2 * DIFF_W
ROPE_BASE = 10000.0
ROPE_NF = 16
PEER_HEADS = 8
PEER_NKEYS = 128
PEER_DQ = 256
PEER_TOPK = 16
DEPTH = 1
DEEPNORM_ALPHA = (2.0 * DEPTH) ** 0.25
LN_EPS = 1e-5
RMS_EPS = 1e-6
LAM_INIT = 0.8 - 0.6 * math.exp(-0.3 * 0)
NEG_INF = float("-inf")


def _cparams(sem):
    return pltpu.CompilerParams(dimension_semantics=sem, vmem_limit_bytes=VMEM_LIMIT)


def _dot(a, b):
    return jnp.dot(a, b, preferred_element_type=F32)


def _dot_nt(a, b):
    return lax.dot_general(a, b, (((1,), (1,)), ((), ())), preferred_element_type=F32)


def _split3(x):
    hi = x.astype(BF16)
    r1 = x - hi.astype(F32)
    mid = r1.astype(BF16)
    lo = (r1 - mid.astype(F32)).astype(BF16)
    return hi, mid, lo


def _adaln_kernel(cb_ref, w_ref, b_ref, o_ref, act_scr, *, n_rows):
    @pl.when(pl.program_id(0) == 0)
    def _():
        c = cb_ref[...]
        act_scr[...] = c * jax.nn.sigmoid(c)

    tn = w_ref.shape[1]
    o_ref[...] = jnp.zeros_like(o_ref)
    for cb in range(tn // LANES):
        w = w_ref[:, cb * LANES:(cb + 1) * LANES]
        for r in range(n_rows):
            s = jnp.sum(act_scr[r] * w, axis=0, keepdims=True)
            o_ref[r:r + 1, cb * LANES:(cb + 1) * LANES] = s + b_ref[:, cb * LANES:(cb + 1) * LANES]


def _adaln(c_rows, w_ada, b_ada):
    n_rows, k = c_rows.shape
    n = w_ada.shape[1]
    tn = 512
    cb = jnp.broadcast_to(c_rows[:, :, None], (n_rows, k, LANES))
    return pl.pallas_call(
        functools.partial(_adaln_kernel, n_rows=n_rows),
        grid=(n // tn,),
        in_specs=[pl.BlockSpec((n_rows, k, LANES), lambda j: (0, 0, 0)),
                  pl.BlockSpec((k, tn), lambda j: (0, j)),
                  pl.BlockSpec((1, tn), lambda j: (0, j))],
        out_specs=pl.BlockSpec((SUBLANES, tn), lambda j: (0, j)),
        out_shape=jax.ShapeDtypeStruct((SUBLANES, n), F32),
        scratch_shapes=[pltpu.VMEM((n_rows, k, LANES), F32)],
        compiler_params=_cparams(("arbitrary",)),
        name="adaln",
    )(cb, w_ada, b_ada.reshape(1, n))


def _inproj_kernel(x_ref, sh_ref, sc_ref, w_ref, wg_ref, cos_ref, sin_ref, o_ref, g_ref, h_scr,
                   *, n_rope_tiles):
    j = pl.program_id(2)

    @pl.when(j == 0)
    def _():
        h = x_ref[0] * (1.0 + sc_ref[0]) + sh_ref[0]
        hb = h.astype(BF16)
        h_scr[...] = hb
        hl = (h - hb.astype(F32)).astype(BF16)
        a = _dot(hb, wg_ref[...])
        g_ref[0] = a[:, 0:LANES] + a[:, LANES:2 * LANES] + _dot(hl, wg_ref[:, 0:LANES])

    acc = _dot(h_scr[...], w_ref[...])
    tn = acc.shape[1]

    @pl.when(j < n_rope_tiles)
    def _():
        cos = cos_ref[...]
        sin = sin_ref[...]
        lane = lax.broadcasted_iota(jnp.int32, cos.shape, 1)
        first = (lane % (2 * ROPE_NF)) < ROPE_NF
        for cb in range(tn // LANES):
            a = acc[:, cb * LANES:(cb + 1) * LANES]
            sw = jnp.where(first, pltpu.roll(a, LANES - ROPE_NF, 1), pltpu.roll(a, ROPE_NF, 1))
            o_ref[0, :, cb * LANES:(cb + 1) * LANES] = (a * cos + sw * sin).astype(BF16)

    @pl.when(j >= n_rope_tiles)
    def _():
        o_ref[0] = acc.astype(BF16)


def _in_proj(x, shift, scale, w_main, wg, cos_t, sin_t, *, tm):
    b, t, d = x.shape
    n = w_main.shape[1]
    tn = 512
    table = pl.BlockSpec((tm, LANES), lambda bi, i, j: (i, 0))
    return pl.pallas_call(
        functools.partial(_inproj_kernel, n_rope_tiles=ROPE_COLS // tn),
        grid=(b, t // tm, n // tn),
        in_specs=[pl.BlockSpec((1, tm, d), lambda bi, i, j: (bi, i, 0)),
                  pl.BlockSpec((1, 1, d), lambda bi, i, j: (bi, 0, 0)),
                  pl.BlockSpec((1, 1, d), lambda bi, i, j: (bi, 0, 0)),
                  pl.BlockSpec((d, tn), lambda bi, i, j: (0, j)),
                  pl.BlockSpec((d, 2 * LANES), lambda bi, i, j: (0, 0)),
                  table, table],
        out_specs=[pl.BlockSpec((1, tm, tn), lambda bi, i, j: (bi, i, j)),
                   pl.BlockSpec((1, tm, LANES), lambda bi, i, j: (bi, i, 0))],
        out_shape=[jax.ShapeDtypeStruct((b, t, n), BF16),
                   jax.ShapeDtypeStruct((b, t, LANES), F32)],
        scratch_shapes=[pltpu.VMEM((tm, d), BF16)],
        compiler_params=_cparams(("arbitrary", "arbitrary", "arbitrary")),
        name="in_proj",
    )(x, shift, scale, w_main, wg, cos_t, sin_t)


def _attn_kernel(lam_ref, gn_ref, q_ref, kc_ref, vc_ref, kl_ref, vl_ref, o_ref,
                 q2_scr, vx_scr, m_scr, acc_scr, *, tq, tk, rq):
    tc = kc_ref.shape[1]
    t = kl_ref.shape[1]

    @pl.when(pl.program_id(2) == 0)
    def _():
        lane = lax.broadcasted_iota(jnp.int32, (tc, LANES), 1)
        ones_c = jnp.where(lane == 0, 1.0, 0.0).astype(BF16)
        vx_scr[0:tc, 0:LANES] = vc_ref[0]
        vx_scr[0:tc, LANES:2 * LANES] = ones_c
        for r0 in range(0, t, tc):
            vx_scr[tc + r0:tc + r0 + tc, 0:LANES] = vl_ref[0, r0:r0 + tc, :]
            vx_scr[tc + r0:tc + r0 + tc, LANES:2 * LANES] = ones_c

    q = q_ref[0]
    qf = q.astype(F32)
    lane = lax.broadcasted_iota(jnp.int32, qf.shape, 1)
    q2_scr[0:tq] = jnp.where(lane < DIFF_HALF, qf, 0.0).astype(BF16)
    q2_scr[tq:2 * tq] = jnp.where(lane >= DIFF_HALF, qf, 0.0).astype(BF16)
    m_scr[...] = jnp.full_like(m_scr, NEG_INF)
    acc_scr[...] = jnp.zeros_like(acc_scr)

    def step(k, vx):
        nkc = k.shape[0] // LANES
        for r0 in range(0, 2 * tq, rq):
            sc = _dot_nt(q2_scr[r0:r0 + rq], k)
            cols = [sc[:, c * LANES:(c + 1) * LANES] for c in range(nkc)]
            mx = cols[0]
            for c in range(1, nkc):
                mx = jnp.maximum(mx, cols[c])
            m_prev = m_scr[r0:r0 + rq]
            m_new = jnp.maximum(m_prev, jnp.max(mx, axis=1, keepdims=True))
            alpha = jnp.exp(m_prev - m_new)
            p = jnp.concatenate([jnp.exp(cb - m_new).astype(BF16) for cb in cols], axis=1)
            alpha2 = jnp.concatenate([alpha, alpha], axis=1)
            acc_scr[r0:r0 + rq] = alpha2 * acc_scr[r0:r0 + rq] + _dot(p, vx)
            m_scr[r0:r0 + rq] = m_new

    step(kc_ref[0], vx_scr[0:tc])

    def body(j, carry):
        r = pl.multiple_of(j * tk, tk)
        step(kl_ref[0, pl.ds(r, tk), :], vx_scr[pl.ds(tc + r, tk)])
        return carry

    lax.fori_loop(0, t // tk, body, 0)

    lm = lam_ref[...]
    d1 = jnp.sum(lm[0:1] * lm[1:2], axis=1, keepdims=True)
    d2 = jnp.sum(lm[2:3] * lm[3:4], axis=1, keepdims=True)
    lam = jnp.exp(d1) - jnp.exp(d2) + LAM_INIT
    a1 = acc_scr[0:tq]
    a2 = acc_scr[tq:2 * tq]
    o1 = a1[:, 0:LANES] / a1[:, LANES:LANES + 1]
    o2 = a2[:, 0:LANES] / a2[:, LANES:LANES + 1]
    o = o1 - lam * o2
    ms = jnp.mean(o * o, axis=1, keepdims=True)
    o = o * lax.rsqrt(ms + RMS_EPS) * gn_ref[...] * (1.0 - LAM_INIT)
    o_ref[0] = o.astype(BF16)


def _diff_attn(proj_l, proj_c, lam_pad, gn, *, tq, tk, rq):
    b, t, _ = proj_l.shape
    tc = proj_c.shape[1]
    h = DIFF_HEADS
    kv = lambda off: (lambda bi, hi, qi: (bi, 0, off + hi))
    return pl.pallas_call(
        functools.partial(_attn_kernel, tq=tq, tk=tk, rq=rq),
        grid=(b, h, t // tq),
        in_specs=[pl.BlockSpec((SUBLANES, LANES), lambda bi, hi, qi: (0, 0)),
                  pl.BlockSpec((1, LANES), lambda bi, hi, qi: (0, 0)),
                  pl.BlockSpec((1, tq, LANES), lambda bi, hi, qi: (bi, qi, hi)),
                  pl.BlockSpec((1, tc, LANES), kv(h)),
                  pl.BlockSpec((1, tc, LANES), kv(2 * h)),
                  pl.BlockSpec((1, t, LANES), kv(h)),
                  pl.BlockSpec((1, t, LANES), kv(2 * h))],
        out_specs=pl.BlockSpec((1, tq, LANES), lambda bi, hi, qi: (bi, qi, hi)),
        out_shape=jax.ShapeDtypeStruct((b, t, DIFF_W), BF16),
        scratch_shapes=[pltpu.VMEM((2 * tq, LANES), BF16),
                        pltpu.VMEM((tc + t, 2 * LANES), BF16),
                        pltpu.VMEM((2 * tq, LANES), F32),
                        pltpu.VMEM((2 * tq, 2 * LANES), F32)],
        compiler_params=_cparams(("arbitrary",) * 3),
        name="diff_attn",
    )(lam_pad, gn, proj_l, proj_c, proj_c, proj_l, proj_l)


def _gate_rows(a_ref, b_ref, ba_ref, bb_ref, rt_scr):
    ng = a_ref.shape[1]
    x = (a_ref[0] + ba_ref[0]).reshape(ng * SUBLANES, LANES)
    ls = jax.nn.log_sigmoid(b_ref[0] + bb_ref[0]).reshape(ng * SUBLANES, LANES)
    ji = lax.broadcasted_iota(jnp.int32, (LANES, LANES), 0)
    si = lax.broadcasted_iota(jnp.int32, (LANES, LANES), 1)
    same = (ji // CHUNK) == (si // CHUNK)
    ones_where = lambda cond: jnp.where(cond, 1.0, 0.0).astype(BF16)
    m_pre = ones_where(same & (ji <= si))
    m_suf = ones_where(same & (ji >= si))
    m_tot = ones_where(same)
    hi, mid, lo = _split3(ls)
    mm = lambda m: _dot(hi, m) + _dot(mid, m) + _dot(lo, m)
    sub = lax.broadcasted_iota(jnp.int32, x.shape, 0) % SUBLANES
    cum = jnp.where(sub % 2 == 0, mm(m_pre), mm(m_suf))
    rt = jnp.where(sub < 2, x - cum, jnp.where(sub < 4, cum, mm(m_tot)))
    rt_scr[...] = rt.reshape(ng, SUBLANES, LANES)


def _conv_silu(raw_ref, w, bias, pad_scr, t, emit):
    pad_scr[0:SUBLANES] = jnp.zeros((SUBLANES, LANES), F32)
    pad_scr[SUBLANES + t:2 * SUBLANES + t] = jnp.zeros((SUBLANES, LANES), F32)
    pad_scr[SUBLANES:SUBLANES + t] = raw_ref[0].astype(F32)
    half = MLSTM_CONV // 2
    for g in range(t // LANES):
        acc = jnp.zeros((LANES, LANES), F32) + bias
        for j in range(MLSTM_CONV):
            off = SUBLANES + g * LANES + j - half
            acc = acc + pad_scr[off:off + LANES] * w[j:j + 1]
        emit(g, acc * jax.nn.sigmoid(acc))


def _mlstm_prologue(q_ref, k_ref, v_ref, cwq_ref, cwk_ref, cbq_ref, cbk_ref,
                    pad_scr, q_scr, kt_scr, vx_scr, t, g0):
    def emit_q(g, blk):
        q_scr[g0 + g] = blk.astype(BF16)

    def emit_k(g, blk):
        kt_scr[g0 + g] = (blk * (MLSTM_DHEAD ** -0.5)).T

    _conv_silu(q_ref, cwq_ref[0], cbq_ref[0], pad_scr, t, emit_q)
    _conv_silu(k_ref, cwk_ref[0], cbk_ref[0], pad_scr, t, emit_k)
    lane = lax.broadcasted_iota(jnp.int32, (LANES, LANES), 1)
    ones_col = jnp.where(lane == 0, 1.0, 0.0).astype(BF16)
    for g in range(t // LANES):
        vx_scr[g0 + g, :, 0:LANES] = v_ref[0, g * LANES:(g + 1) * LANES, :]
        vx_scr[g0 + g, :, LANES:2 * LANES] = ones_col


def _groups(q_scr, kt_scr, vx_scr, c_scr, m_st, items, bwd, out_scr):
    d = 1 if bwd else 0
    order = (1, 0) if bwd else (0, 1)
    lane = lax.broadcasted_iota(jnp.int32, (1, LANES), 1)
    in_c = (lane < CHUNK, lane >= CHUNK)
    li = lax.broadcasted_iota(jnp.int32, (LANES, LANES), 0)
    si = lax.broadcasted_iota(jnp.int32, (LANES, LANES), 1)
    same = (li < CHUNK) == (si < CHUNK)
    mask = same & ((si >= li) if bwd else (si <= li))
    row0 = lax.broadcasted_iota(jnp.int32, (LANES, 1), 0) < CHUNK

    st = []
    for g, rt, og in items:
        r = rt[d:d + 1, :]
        cum = rt[2 + d:3 + d, :]
        totrow = rt[4 + d:5 + d, :]
        maxr = [jnp.max(jnp.where(in_c[c], r, NEG_INF), axis=1, keepdims=True) for c in (0, 1)]
        tot = [jnp.max(jnp.where(in_c[c], totrow, NEG_INF), axis=1, keepdims=True) for c in (0, 1)]
        m_b, mu_f = [None, None], [None, None]
        for c in order:
            m_b[c] = m_st
            mu_f[c] = jnp.maximum(m_st, maxr[c])
            m_st = tot[c] + mu_f[c]
        st.append(dict(g=g, og=og, r=r, cum=cum, m_b=m_b, mu_f=mu_f))
    for e in st:
        kt = kt_scr[e["g"]]
        vx = vx_scr[e["g"]]
        e["kt"], e["vx"] = kt, vx
        e["kv"] = [_dot((kt * jnp.where(in_c[c], jnp.exp(e["r"] - e["mu_f"][c]), 0.0)).astype(BF16), vx)
                   for c in (0, 1)]
        if out_scr is not None:
            e["q"] = q_scr[e["g"]]
            e["sc"] = _dot(e["q"], kt.astype(BF16))
    c_st = c_scr[...]
    for e in st:
        e["c_bf"] = [None, None]
        for c in order:
            e["c_bf"][c] = c_st.astype(BF16)
            c_st = jnp.exp(e["m_b"][c] - e["mu_f"][c]) * c_st + e["kv"][c]
    c_scr[...] = c_st
    if out_scr is not None:
        for e in st:
            m_col = jnp.where(row0, e["m_b"][0], e["m_b"][1])
            rm = jnp.where(mask, e["r"], NEG_INF)
            mu = jnp.maximum(m_col, jnp.max(rm, axis=1, keepdims=True))
            e["mu"], e["m_col"] = mu, m_col
            e["s"] = (e["sc"] * jnp.exp(rm - mu)).astype(BF16)
        for e in st:
            e["intra"] = _dot(e["s"], e["vx"])
            e["inter"] = [_dot(e["q"][c * CHUNK:(c + 1) * CHUNK], e["c_bf"][c]) for c in (0, 1)]
        for e in st:
            inter = jnp.concatenate(e["inter"], axis=0)
            tot_o = e["intra"] + jnp.exp(e["m_col"] - e["mu"]) * inter
            num = tot_o[:, 0:LANES]
            den = tot_o[:, LANES:LANES + 1]
            cum_col = jnp.sum(jnp.where(li == si, e["cum"], 0.0), axis=1, keepdims=True)
            floor = jnp.exp(-(cum_col + e["mu"]))
            out_scr[e["og"]] = num / jnp.maximum(jnp.abs(den), floor)
    return m_st


def _mlstm_kernel(ql_ref, kl_ref, vl_ref, ol_ref, qc_ref, kc_ref, vc_ref,
                  al_ref, bl_ref, ac_ref, bc_ref, ba_ref, bb_ref,
                  cwq_ref, cwk_ref, cbq_ref, cbk_ref, gn_ref, o_ref,
                  pad_scr, q_scr, kt_scr, vx_scr, rtc_scr, rtl_scr, cf_scr, cb_scr, hf_scr, hb_scr,
                  *, t, tc, gpi):
    _gate_rows(ac_ref.at[0], bc_ref.at[0], ba_ref, bb_ref, rtc_scr)
    _gate_rows(al_ref.at[0], bl_ref.at[0], ba_ref, bb_ref, rtl_scr)
    _mlstm_prologue(qc_ref, kc_ref, vc_ref, cwq_ref, cwk_ref, cbq_ref, cbk_ref,
                    pad_scr, q_scr, kt_scr, vx_scr, tc, 0)
    _mlstm_prologue(ql_ref, kl_ref, vl_ref, cwq_ref, cwk_ref, cbq_ref, cbk_ref,
                    pad_scr, q_scr, kt_scr, vx_scr, t, tc // LANES)
    cf_scr[...] = jnp.zeros_like(cf_scr)
    cb_scr[...] = jnp.zeros_like(cb_scr)
    m0 = jnp.zeros((1, 1), F32)

    def group(rt_scr, ng, g0, outs, gpi):
        def body(it, carry):
            mf, mb = carry
            items_f, items_b = [], []
            for u in range(gpi):
                gf = it * gpi + u
                gb = ng - 1 - gf
                items_f.append((g0 + gf, rt_scr[gf], gf))
                items_b.append((g0 + gb, rt_scr[gb], gb))
            args = (q_scr, kt_scr, vx_scr)
            mf = _groups(*args, cf_scr, mf, items_f, False, outs and hf_scr)
            mb = _groups(*args, cb_scr, mb, items_b, True, outs and hb_scr)
            return mf, mb
        return body

    ngc = tc // LANES
    ngl = t // LANES
    gc = math.gcd(ngc, gpi)
    gl = math.gcd(ngl, gpi)
    carry = lax.fori_loop(0, ngc // gc, group(rtc_scr, ngc, 0, None, gc), (m0, m0))
    lax.fori_loop(0, ngl // gl, group(rtl_scr, ngl, ngc, True, gl), carry)

    hm = (hf_scr[...] + hb_scr[...]).reshape(t, LANES)
    ms = jnp.mean(hm * hm, axis=1, keepdims=True)
    out = hm * lax.rsqrt(ms + RMS_EPS) * gn_ref[...] * jax.nn.sigmoid(ol_ref[0].astype(F32))
    o_ref[0] = out.astype(BF16)


def _mlstm(proj_l, proj_c, ga_l, gb_l, ga_c, gb_c, bias_a, bias_b, conv_w, conv_b, gn):
    b, t, _ = proj_l.shape
    tc = proj_c.shape[1]
    h = MLSTM_HEADS
    col = lambda off: (lambda bi, hi: (bi, 0, off + hi))
    q0 = (3 * DIFF_W) // LANES
    gspec = lambda ng: pl.BlockSpec((1, 1, ng, SUBLANES, LANES), lambda bi, hi: (bi, hi, 0, 0, 0))
    hspec = pl.BlockSpec((1, SUBLANES, LANES), lambda bi, hi: (hi, 0, 0))
    cw = jnp.pad(conv_w, ((0, SUBLANES - MLSTM_CONV), (0, 0)))
    cwspec = lambda off: pl.BlockSpec((1, SUBLANES, LANES), lambda bi, hi: (0, 0, off + hi))
    cbspec = lambda off: pl.BlockSpec((1, 1, LANES), lambda bi, hi: (0, 0, off + hi))
    tt = t + tc
    return pl.pallas_call(
        functools.partial(_mlstm_kernel, t=t, tc=tc, gpi=MLSTM_GROUPS_PER_STEP),
        grid=(b, h),
        in_specs=[pl.BlockSpec((1, t, LANES), col(q0)),
                  pl.BlockSpec((1, t, LANES), col(q0 + h)),
                  pl.BlockSpec((1, t, LANES), col(q0 + 2 * h)),
                  pl.BlockSpec((1, t, LANES), col(q0 + 3 * h)),
                  pl.BlockSpec((1, tc, LANES), col(q0)),
                  pl.BlockSpec((1, tc, LANES), col(q0 + h)),
                  pl.BlockSpec((1, tc, LANES), col(q0 + 2 * h)),
                  gspec(t // LANES), gspec(t // LANES), gspec(tc // LANES), gspec(tc // LANES),
                  hspec, hspec,
                  cwspec(0), cwspec(h), cbspec(0), cbspec(h),
                  pl.BlockSpec((1, LANES), lambda bi, hi: (0, 0))],
        out_specs=pl.BlockSpec((1, t, LANES), lambda bi, hi: (bi, 0, hi)),
        out_shape=jax.ShapeDtypeStruct((b, t, ML_W), BF16),
        scratch_shapes=[pltpu.VMEM((t + 2 * SUBLANES, LANES), F32),
                        pltpu.VMEM((tt // LANES, LANES, LANES), BF16),
                        pltpu.VMEM((tt // LANES, LANES, LANES), F32),
                        pltpu.VMEM((tt // LANES, LANES, 2 * LANES), BF16),
                        pltpu.VMEM((tc // LANES, SUBLANES, LANES), F32),
                        pltpu.VMEM((t // LANES, SUBLANES, LANES), F32),
                        pltpu.VMEM((LANES, 2 * LANES), F32),
                        pltpu.VMEM((LANES, 2 * LANES), F32),
                        pltpu.VMEM((t // LANES, LANES, LANES), F32),
                        pltpu.VMEM((t // LANES, LANES, LANES), F32)],
        compiler_params=_cparams(("arbitrary", "arbitrary")),
        name="mlstm",
    )(proj_l, proj_l, proj_l, proj_l, proj_c, proj_c, proj_c,
      ga_l, gb_l, ga_c, gb_c, bias_a, bias_b,
      cw.reshape(1, SUBLANES, 2 * ML_W), cw.reshape(1, SUBLANES, 2 * ML_W),
      conv_b.reshape(1, 1, 2 * ML_W), conv_b.reshape(1, 1, 2 * ML_W), gn)


def _layer_norm(v, g, b):
    mu = jnp.mean(v, axis=1, keepdims=True)
    c = v - mu
    var = jnp.mean(c * c, axis=1, keepdims=True)
    return c * lax.rsqrt(var + LN_EPS) * g + b


def _outproj_kernel(d_ref, m_ref, w_ref, x_ref, g1_ref, lg_ref, lb_ref, sh_ref, sc_ref,
                    x1_ref, h2_ref):
    y = _dot(d_ref[0], w_ref[0]) + _dot(m_ref[0], w_ref[1])
    v = DEEPNORM_ALPHA * x_ref[0] + g1_ref[0] * y
    x1 = _layer_norm(v, lg_ref[...], lb_ref[...])
    x1_ref[0] = x1
    h2_ref[0] = (x1 * (1.0 + sc_ref[0]) + sh_ref[0]).astype(BF16)


def _out_proj(d_lat, m_lat, w_out2, x, g1, ln_g, ln_b, sh2, sc2, *, tm):
    b, t, d = x.shape
    hw = d_lat.shape[2]
    row = pl.BlockSpec((1, 1, d), lambda bi, i: (bi, 0, 0))
    vec = pl.BlockSpec((1, d), lambda bi, i: (0, 0))
    return pl.pallas_call(
        _outproj_kernel,
        grid=(b, t // tm),
        in_specs=[pl.BlockSpec((1, tm, hw), lambda bi, i: (bi, i, 0)),
                  pl.BlockSpec((1, tm, hw), lambda bi, i: (bi, i, 0)),
                  pl.BlockSpec((2, hw, d), lambda bi, i: (0, 0, 0)),
                  pl.BlockSpec((1, tm, d), lambda bi, i: (bi, i, 0)),
                  row, vec, vec, row, row],
        out_specs=[pl.BlockSpec((1, tm, d), lambda bi, i: (bi, i, 0)),
                   pl.BlockSpec((1, tm, d), lambda bi, i: (bi, i, 0))],
        out_shape=[jax.ShapeDtypeStruct((b, t, d), F32),
                   jax.ShapeDtypeStruct((b, t, d), BF16)],
        compiler_params=_cparams(("arbitrary", "arbitrary")),
        name="out_proj",
    )(d_lat, m_lat, w_out2, x, g1, ln_g, ln_b, sh2, sc2)


def _top_rows(x, k, dst_scr):
    cur = x
    for r in range(k):
        m = jnp.max(cur, axis=0, keepdims=True)
        dst_scr[r:r + 1] = m
        if r + 1 < k:
            cur = jnp.where(cur == m, NEG_INF, cur)


def _sort16_pairs():
    def merge(lo, hi, r):
        step = r * 2
        if step < hi - lo:
            yield from merge(lo, hi, step)
            yield from merge(lo + r, hi, step)
            yield from [(i, i + r) for i in range(lo + r, hi - r, step)]
        else:
            yield (lo, lo + r)

    def sort(lo, hi):
        if hi - lo >= 1:
            mid = lo + (hi - lo) // 2
            yield from sort(lo, mid)
            yield from sort(mid + 1, hi)
            yield from merge(lo, hi, 1)

    return list(sort(0, 15))


def _top16_of_128(x, dst_scr):
    k = PEER_TOPK
    y = [x[r * SUBLANES:(r + 1) * SUBLANES] for r in range(k)]
    for a, b in _sort16_pairs():
        y[a], y[b] = jnp.maximum(y[a], y[b]), jnp.minimum(y[a], y[b])
    for r in range(k):
        m = jnp.max(y[0], axis=0, keepdims=True)
        dst_scr[r:r + 1] = m
        if r + 1 < k:
            hit = y[0] == m
            for i in range(k - 1 - r):
                y[i] = jnp.where(hit, y[i + 1], y[i])


def _peer_prep_kernel(h_ref, wq_ref, k1_ref, k2_ref, n1_ref, r2_ref, e1_ref, e2_ref,
                      v1_scr, v2_scr, tp_scr):
    q = _dot(h_ref[...], wq_ref[...])
    half = PEER_DQ // 2
    for h in range(PEER_HEADS):
        q1 = q[:, h * PEER_DQ:h * PEER_DQ + half].astype(BF16)
        q2 = q[:, h * PEER_DQ + half:(h + 1) * PEER_DQ].astype(BF16)
        s1 = _dot_nt(k1_ref[h], q1)
        s2 = _dot_nt(k2_ref[h], q2)
        _top16_of_128(s1, v1_scr)
        _top16_of_128(s2, v2_scr)
        v1 = v1_scr[...]
        v2 = v2_scr[...]
        pieces = [v1[0:1] + v2]
        pieces += [v1[a:a + 1] + v2[0:SUBLANES] for a in range(1, SUBLANES)]
        pieces += [v1[SUBLANES:PEER_TOPK] + v2[0:1]]
        cand = jnp.concatenate(pieces, axis=0)
        _top_rows(cand, PEER_TOPK, tp_scr)
        tp = tp_scr[...]
        top0 = tp[0:1]
        tau = tp[PEER_TOPK - 1:PEER_TOPK]
        z = jnp.sum(jnp.exp(tp - top0), axis=0, keepdims=True)
        cnt = jnp.zeros(s1.shape, F32)
        rank = jnp.zeros(s2.shape, F32)
        for bb in range(PEER_TOPK):
            n_a = jnp.sum(jnp.where(v1[bb:bb + 1] + v2 >= tau, 1.0, 0.0), axis=0, keepdims=True)
            cnt = jnp.where(s1 == v1[bb:bb + 1], n_a, cnt)
            rank = rank + jnp.where(s2 < v2[bb:bb + 1], 1.0, 0.0)
        n1_ref[h] = cnt
        r2_ref[h] = rank.astype(BF16)
        e1_ref[h] = jnp.exp(s1 - v1[0:1]) / z
        e2_ref[h] = jnp.exp(s2 - v2[0:1]).astype(BF16)


def _peer_prep(h2, wq, k1, k2, *, tt):
    n, d = h2.shape
    hp = PEER_HEADS
    big = jax.ShapeDtypeStruct((hp, PEER_NKEYS, n), F32)
    big16 = jax.ShapeDtypeStruct((hp, PEER_NKEYS, n), BF16)
    bspec = pl.BlockSpec((hp, PEER_NKEYS, tt), lambda i: (0, 0, i))
    kspec = pl.BlockSpec((hp, PEER_NKEYS, PEER_DQ // 2), lambda i: (0, 0, 0))
    return pl.pallas_call(
        _peer_prep_kernel,
        grid=(n // tt,),
        in_specs=[pl.BlockSpec((tt, d), lambda i: (i, 0)),
                  pl.BlockSpec((d, hp * PEER_DQ), lambda i: (0, 0)),
                  kspec, kspec],
        out_specs=[bspec, bspec, bspec, bspec],
        out_shape=[big, big16, big, big16],
        scratch_shapes=[pltpu.VMEM((PEER_TOPK, tt), F32),
                        pltpu.VMEM((PEER_TOPK, tt), F32),
                        pltpu.VMEM((PEER_TOPK, tt), F32)],
        compiler_params=_cparams(("arbitrary",)),
        name="peer_prep",
    )(h2, wq, k1, k2)


def _peer_dense_kernel(h_ref, u_ref, vt_ref, n1_ref, r2_ref, e1_ref, e2_ref, x1_ref, g2_ref, lg_ref, lb_ref,
                       o_ref, ht_scr, z_scr, acc_scr, *, te):
    j = pl.program_id(1)
    nj = pl.num_programs(1)
    nk = PEER_NKEYS

    @pl.when(j == 0)
    def _():
        ht_scr[...] = h_ref[...].astype(F32).T.astype(BF16)
        acc_scr[...] = jnp.zeros_like(acc_scr)
        z_scr[1] = jnp.zeros(z_scr.shape[1:], F32)

    prev = jnp.maximum(j - 1, 0)

    tt = ht_scr.shape[1]
    tn = 2 * LANES

    def body(wslot, rslot):
        for n0 in range(0, tt, tn):
            z_scr[wslot, :, n0:n0 + tn] = _dot(u_ref[...], ht_scr[:, n0:n0 + tn])
            ws = []
            for a in range(te // nk):
                i1 = prev * (te // nk) + a
                g = None
                for h in range(PEER_HEADS):
                    n1row = n1_ref[h, pl.ds(i1, 1), n0:n0 + tn].astype(BF16)
                    e1row = e1_ref[h, pl.ds(i1, 1), n0:n0 + tn].astype(BF16)
                    hit = r2_ref[h, :, n0:n0 + tn] < n1row
                    term = jnp.where(hit, e2_ref[h, :, n0:n0 + tn], jnp.zeros((), BF16)) * e1row
                    g = term if g is None else g + term
                z = z_scr[rslot, a * nk:(a + 1) * nk, n0:n0 + tn]
                gelu = 0.5 * z * (1.0 + lax.erf(z * math.sqrt(0.5)))
                ws.append(gelu.astype(BF16) * g)
            acc_scr[:, n0:n0 + tn] += _dot(vt_ref[...], jnp.concatenate(ws, axis=0))

    @pl.when(j % 2 == 0)
    def _():
        body(0, 1)

    @pl.when(j % 2 == 1)
    def _():
        body(1, 0)

    @pl.when(j == nj - 1)
    def _():
        v = DEEPNORM_ALPHA * x1_ref[...] + g2_ref[0] * acc_scr[...].T
        o_ref[...] = _layer_norm(v, lg_ref[...], lb_ref[...])


def _peer_dense(h2, u_bf, vt_bf, n1, r2, e1, e2, x1, g2, ln_g, ln_b, *, tt, te):
    n, d = h2.shape
    ne = u_bf.shape[0]
    hp = PEER_HEADS
    nb = ne // te
    blocks_per_sample = n // g2.shape[0] // tt
    bspec = pl.BlockSpec((hp, PEER_NKEYS, tt), lambda i, j: (0, 0, i))
    vec = pl.BlockSpec((1, d), lambda i, j: (0, 0))
    return pl.pallas_call(
        functools.partial(_peer_dense_kernel, te=te),
        grid=(n // tt, nb + 1),
        in_specs=[pl.BlockSpec((tt, d), lambda i, j: (i, 0)),
                  pl.BlockSpec((te, d), lambda i, j: (jnp.minimum(j, nb - 1), 0)),
                  pl.BlockSpec((d, te), lambda i, j: (0, jnp.maximum(j - 1, 0))),
                  bspec, bspec, bspec, bspec,
                  pl.BlockSpec((tt, d), lambda i, j: (i, 0)),
                  pl.BlockSpec((1, 1, d), lambda i, j: (i // blocks_per_sample, 0, 0)),
                  vec, vec],
        out_specs=pl.BlockSpec((tt, d), lambda i, j: (i, 0)),
        out_shape=jax.ShapeDtypeStruct((n, d), F32),
        scratch_shapes=[pltpu.VMEM((d, tt), BF16),
                        pltpu.VMEM((2, te, tt), F32),
                        pltpu.VMEM((d, tt), F32)],
        compiler_params=_cparams(("arbitrary", "arbitrary")),
        name="peer_dense",
    )(h2, u_bf, vt_bf, n1, r2, e1, e2, x1, g2, ln_g, ln_b)


def _rope_tables(t):
    pos = jnp.arange(t, dtype=jnp.int32)
    row = (pos // GRID_W).astype(F32)
    col = (pos % GRID_W).astype(F32)
    inv = ROPE_BASE ** (-jnp.arange(ROPE_NF, dtype=F32) / ROPE_NF)
    lane = jnp.arange(LANES)
    use_col = ((lane // (2 * ROPE_NF)) % 2) == 1
    p = jnp.where(use_col[None, :], col[:, None], row[:, None])
    ang = p * inv[lane % ROPE_NF][None, :]
    sign = jnp.where((lane % (2 * ROPE_NF)) < ROPE_NF, -1.0, 1.0).astype(F32)
    return jnp.cos(ang), jnp.sin(ang) * sign[None, :]


def _gate_layout(g):
    b, t, _ = g.shape
    gt = g[:, :, :N_GATES * MLSTM_HEADS].reshape(b, t // LANES, LANES, N_GATES, MLSTM_HEADS)
    gt = gt.transpose(0, 4, 1, 3, 2)
    zeros = jnp.zeros_like(gt[:, :, :, 0:1])
    a = jnp.concatenate([gt[:, :, :, 0:1], gt[:, :, :, 2:3]] + [zeros] * 6, axis=3)
    fb = jnp.concatenate([gt[:, :, :, 1:2], gt[:, :, :, 3:4]], axis=3)
    return a, jnp.tile(fb, (1, 1, 1, 4, 1))


def _gate_bias(gate_b):
    gb = gate_b.astype(F32)
    z = jnp.zeros_like(gb[0])
    a = jnp.stack([gb[0], gb[2]] + [z] * 6, axis=1)
    bm = jnp.stack([gb[1], gb[3]] * 4, axis=1)
    bc = lambda v: jnp.broadcast_to(v[:, :, None], (MLSTM_HEADS, SUBLANES, LANES))
    return bc(a), bc(bm)


def kernel(x, c, ctx, c_ctx, w_ada, b_ada, w_in, conv_w, conv_b, gate_b, diff_lambda, diff_norm_g,
           mlstm_norm_g, w_out, ln1_g, ln1_b, ln2_g, ln2_b, peer_wq, peer_keys, peer_u, peer_v):
    b, t, d = x.shape
    tc = ctx.shape[1]
    l = 0

    c_rows = jnp.concatenate([c, c_ctx[None, :]], axis=0)
    mod = _adaln(c_rows, w_ada[l], b_ada[l])
    sh1, sc1, g1, sh2, sc2, g2 = [m[:, None, :] for m in jnp.split(mod[:b], 6, axis=1)]
    csh1, csc1 = [m[:, None, :] for m in jnp.split(mod[b:b + 1], 6, axis=1)[:2]]
    csh1 = jnp.broadcast_to(csh1, (b, 1, d))
    csc1 = jnp.broadcast_to(csc1, (b, 1, d))

    w = w_in[l]
    col_scale = jnp.where(jnp.arange(MAIN_COLS) < DIFF_W, DIFF_HALF ** -0.5, 1.0).astype(F32)
    w_main = (w[:, :MAIN_COLS] * col_scale[None, :]).astype(BF16)
    wg = jnp.pad(w[:, MAIN_COLS:], ((0, 0), (0, LANES - N_GATES * MLSTM_HEADS)))
    wg_hi = wg.astype(BF16)
    wg_lo = (wg - wg_hi.astype(F32)).astype(BF16)
    wg2 = jnp.concatenate([wg_hi, wg_lo], axis=1)
    cos_t, sin_t = _rope_tables(t)
    proj_l, gates_l = _in_proj(x, sh1, sc1, w_main, wg2, cos_t, sin_t, tm=min(1024, t))
    proj_c, gates_c = _in_proj(ctx, csh1, csc1, w_main, wg2,
                               jnp.ones((tc, LANES), F32), jnp.zeros((tc, LANES), F32), tm=tc)

    lam_pad = jnp.pad(diff_lambda[l].astype(F32), ((0, SUBLANES - 4), (0, LANES - DIFF_HALF)))
    d_lat = _diff_attn(proj_l, proj_c, lam_pad, diff_norm_g[l].reshape(1, LANES),
                       tq=min(2048, t), tk=min(2048, t), rq=256)

    ga_l, gb_l = _gate_layout(gates_l)
    ga_c, gb_c = _gate_layout(gates_c)
    bias_a, bias_b = _gate_bias(gate_b[l])
    m_lat = _mlstm(proj_l, proj_c, ga_l, gb_l, ga_c, gb_c, bias_a, bias_b,
                   conv_w[l], conv_b[l], mlstm_norm_g[l].reshape(1, LANES))

    w_out2 = w_out[l].astype(BF16).reshape(2, DIFF_W, d)
    x1, h2 = _out_proj(d_lat, m_lat, w_out2, x, g1, ln1_g[l].reshape(1, d), ln1_b[l].reshape(1, d),
                       sh2, sc2, tm=min(512, t))

    n = b * t
    h2f = h2.reshape(n, d)
    keys = peer_keys[l].astype(BF16)
    n1, r2, e1, e2 = _peer_prep(h2f, peer_wq[l].astype(BF16), keys[0], keys[1], tt=min(512, n))
    out = _peer_dense(h2f, peer_u[l].astype(BF16), peer_v[l].T.astype(BF16), n1, r2, e1, e2,
                      x1.reshape(n, d), g2, ln2_g[l].reshape(1, d), ln2_b[l].reshape(1, d),
                      tt=min(512, t), te=512)
    return out.reshape(b, t, d)
```

```python
import functools
import math

import jax
import jax.numpy as jnp
from jax import lax
from jax.experimental import pallas as pl
from jax.experimental.pallas import tpu as pltpu

F32 = jnp.float32
BF16 = jnp.bfloat16

LANES = 128
SUBLANES = 8
MXU_WIDTH = 256
VMEM_LIMIT = 56 * 1024 * 1024

TILES = dict(
    adaln_cols=512,
    in_proj_rows=1024, in_proj_cols=512,
    attn_q_rows=2048, attn_kv_rows=2048, attn_row_group=MXU_WIDTH,
    out_proj_rows=512,
    peer_tokens=512, peer_experts=512, peer_token_slice=MXU_WIDTH,
)

GRID_W = 64
DIFF_HEADS = 8
DIFF_HALF = 64
DIFF_W = DIFF_HEADS * 2 * DIFF_HALF
MLSTM_HEADS = 8
MLSTM_DHEAD = 128
ML_W = MLSTM_HEADS * MLSTM_DHEAD
MLSTM_CONV = 5
CHUNK = 64
MLSTM_GROUPS_PER_STEP = 4
N_GATES = 4
MAIN_COLS = 3 * DIFF_W + 4 * ML_W
ROPE_COLS = 2 * DIFF_W
ROPE_BASE = 10000.0
ROPE_NF = 16
PEER_HEADS = 8
PEER_NKEYS = 128
PEER_DQ = 256
PEER_TOPK = 16
DEPTH = 1
DEEPNORM_ALPHA = (2.0 * DEPTH) ** 0.25
LN_EPS = 1e-5
RMS_EPS = 1e-6
LAM_INIT = 0.8 - 0.6 * math.exp(-0.3 * 0)
NEG_INF = float("-inf")


def _cparams(sem):
    return pltpu.CompilerParams(dimension_semantics=sem, vmem_limit_bytes=VMEM_LIMIT)


def _dot(a, b):
    return jnp.dot(a, b, preferred_element_type=F32)


def _dot_nt(a, b):
    return lax.dot_general(a, b, (((1,), (1,)), ((), ())), preferred_element_type=F32)


def _split3(x):
    hi = x.astype(BF16)
    r1 = x - hi.astype(F32)
    mid = r1.astype(BF16)
    lo = (r1 - mid.astype(F32)).astype(BF16)
    return hi, mid, lo


def _adaln_kernel(cb_ref, w_ref, b_ref, o_ref, act_scr, *, n_rows):
    @pl.when(pl.program_id(0) == 0)
    def _():
        c = cb_ref[...]
        act_scr[...] = c * jax.nn.sigmoid(c)

    tn = w_ref.shape[1]
    o_ref[...] = jnp.zeros_like(o_ref)
    for cb in range(tn // LANES):
        w = w_ref[:, cb * LANES:(cb + 1) * LANES]
        for r in range(n_rows):
            s = jnp.sum(act_scr[r] * w, axis=0, keepdims=True)
            o_ref[r:r + 1, cb * LANES:(cb + 1) * LANES] = s + b_ref[:, cb * LANES:(cb + 1) * LANES]


def _adaln(c_rows, w_ada, b_ada):
    n_rows, k = c_rows.shape
    n = w_ada.shape[1]
    tn = TILES["adaln_cols"]
    cb = jnp.broadcast_to(c_rows[:, :, None], (n_rows, k, LANES))
    return pl.pallas_call(
        functools.partial(_adaln_kernel, n_rows=n_rows),
        grid=(n // tn,),
        in_specs=[pl.BlockSpec((n_rows, k, LANES), lambda j: (0, 0, 0)),
                  pl.BlockSpec((k, tn), lambda j: (0, j)),
                  pl.BlockSpec((1, tn), lambda j: (0, j))],
        out_specs=pl.BlockSpec((SUBLANES, tn), lambda j: (0, j)),
        out_shape=jax.ShapeDtypeStruct((SUBLANES, n), F32),
        scratch_shapes=[pltpu.VMEM((n_rows, k, LANES), F32)],
        compiler_params=_cparams(("arbitrary",)),
        name="adaln",
    )(cb, w_ada, b_ada.reshape(1, n))


def _inproj_kernel(x_ref, sh_ref, sc_ref, w_ref, wg_ref, cos_ref, sin_ref, o_ref, g_ref, h_scr,
                   *, n_rope_tiles):
    j = pl.program_id(2)

    @pl.when(j == 0)
    def _():
        h = x_ref[0] * (1.0 + sc_ref[0]) + sh_ref[0]
        hb = h.astype(BF16)
        h_scr[...] = hb
        hl = (h - hb.astype(F32)).astype(BF16)
        a = _dot(hb, wg_ref[...])
        g = a[:, 0:LANES] + a[:, LANES:2 * LANES] + _dot(hl, wg_ref[:, 0:LANES])
        g_ref[0] = g.T

    acc = _dot(h_scr[...], w_ref[...])
    tn = acc.shape[1]

    @pl.when(j < n_rope_tiles)
    def _():
        cos = cos_ref[...]
        sin = sin_ref[...]
        lane = lax.broadcasted_iota(jnp.int32, cos.shape, 1)
        first = (lane % (2 * ROPE_NF)) < ROPE_NF
        for cb in range(tn // LANES):
            a = acc[:, cb * LANES:(cb + 1) * LANES]
            sw = jnp.where(first, pltpu.roll(a, LANES - ROPE_NF, 1), pltpu.roll(a, ROPE_NF, 1))
            o_ref[0, :, cb * LANES:(cb + 1) * LANES] = (a * cos + sw * sin).astype(BF16)

    @pl.when(j >= n_rope_tiles)
    def _():
        o_ref[0] = acc.astype(BF16)


def _in_proj(x, shift, scale, w_main, wg, cos_t, sin_t, *, tm):
    b, t, d = x.shape
    n = w_main.shape[1]
    tn = TILES["in_proj_cols"]
    table = pl.BlockSpec((tm, LANES), lambda bi, i, j: (i, 0))
    return pl.pallas_call(
        functools.partial(_inproj_kernel, n_rope_tiles=ROPE_COLS // tn),
        grid=(b, t // tm, n // tn),
        in_specs=[pl.BlockSpec((1, tm, d), lambda bi, i, j: (bi, i, 0)),
                  pl.BlockSpec((1, 1, d), lambda bi, i, j: (bi, 0, 0)),
                  pl.BlockSpec((1, 1, d), lambda bi, i, j: (bi, 0, 0)),
                  pl.BlockSpec((d, tn), lambda bi, i, j: (0, j)),
                  pl.BlockSpec((d, 2 * LANES), lambda bi, i, j: (0, 0)),
                  table, table],
        out_specs=[pl.BlockSpec((1, tm, tn), lambda bi, i, j: (bi, i, j)),
                   pl.BlockSpec((1, LANES, tm), lambda bi, i, j: (bi, 0, i))],
        out_shape=[jax.ShapeDtypeStruct((b, t, n), BF16),
                   jax.ShapeDtypeStruct((b, LANES, t), F32)],
        scratch_shapes=[pltpu.VMEM((tm, d), BF16)],
        compiler_params=_cparams(("arbitrary", "arbitrary", "arbitrary")),
        name="in_proj",
    )(x, shift, scale, w_main, wg, cos_t, sin_t)


def _attn_kernel(lam_ref, gn_ref, q_ref, kc_ref, vc_ref, kl_ref, vl_ref, o_ref,
                 q2_scr, vx_scr, m_scr, acc_scr, *, tq, tk, rq):
    tc = kc_ref.shape[1]
    t = kl_ref.shape[1]

    @pl.when(pl.program_id(2) == 0)
    def _():
        lane = lax.broadcasted_iota(jnp.int32, (tc, LANES), 1)
        ones_c = jnp.where(lane == 0, 1.0, 0.0).astype(BF16)
        vx_scr[0:tc, 0:LANES] = vc_ref[0]
        vx_scr[0:tc, LANES:2 * LANES] = ones_c
        for r0 in range(0, t, tc):
            vx_scr[tc + r0:tc + r0 + tc, 0:LANES] = vl_ref[0, r0:r0 + tc, :]
            vx_scr[tc + r0:tc + r0 + tc, LANES:2 * LANES] = ones_c

    q = q_ref[0]
    qf = q.astype(F32)
    lane = lax.broadcasted_iota(jnp.int32, qf.shape, 1)
    q2_scr[0:tq] = jnp.where(lane < DIFF_HALF, qf, 0.0).astype(BF16)
    q2_scr[tq:2 * tq] = jnp.where(lane >= DIFF_HALF, qf, 0.0).astype(BF16)
    m_scr[...] = jnp.full_like(m_scr, NEG_INF)
    acc_scr[...] = jnp.zeros_like(acc_scr)

    def step(k, vx):
        nkc = k.shape[0] // LANES
        for r0 in range(0, 2 * tq, rq):
            sc = _dot_nt(q2_scr[r0:r0 + rq], k)
            cols = [sc[:, c * LANES:(c + 1) * LANES] for c in range(nkc)]
            mx = cols[0]
            for c in range(1, nkc):
                mx = jnp.maximum(mx, cols[c])
            m_prev = m_scr[r0:r0 + rq]
            m_new = jnp.maximum(m_prev, jnp.max(mx, axis=1, keepdims=True))
            alpha = jnp.exp(m_prev - m_new)
            p = jnp.concatenate([jnp.exp(cb - m_new).astype(BF16) for cb in cols], axis=1)
            alpha2 = jnp.concatenate([alpha, alpha], axis=1)
            acc_scr[r0:r0 + rq] = alpha2 * acc_scr[r0:r0 + rq] + _dot(p, vx)
            m_scr[r0:r0 + rq] = m_new

    step(kc_ref[0], vx_scr[0:tc])

    def body(j, carry):
        r = pl.multiple_of(j * tk, tk)
        step(kl_ref[0, pl.ds(r, tk), :], vx_scr[pl.ds(tc + r, tk)])
        return carry

    lax.fori_loop(0, t // tk, body, 0)

    lm = lam_ref[...]
    d1 = jnp.sum(lm[0:1] * lm[1:2], axis=1, keepdims=True)
    d2 = jnp.sum(lm[2:3] * lm[3:4], axis=1, keepdims=True)
    lam = jnp.exp(d1) - jnp.exp(d2) + LAM_INIT
    a1 = acc_scr[0:tq]
    a2 = acc_scr[tq:2 * tq]
    o1 = a1[:, 0:LANES] / a1[:, LANES:LANES + 1]
    o2 = a2[:, 0:LANES] / a2[:, LANES:LANES + 1]
    o = o1 - lam * o2
    ms = jnp.mean(o * o, axis=1, keepdims=True)
    o = o * lax.rsqrt(ms + RMS_EPS) * gn_ref[...] * (1.0 - LAM_INIT)
    o_ref[0] = o.astype(BF16)


def _diff_attn(proj_l, proj_c, lam_pad, gn, *, tq, tk, rq):
    b, t, _ = proj_l.shape
    tc = proj_c.shape[1]
    h = DIFF_HEADS
    kv = lambda off: (lambda bi, hi, qi: (bi, 0, off + hi))
    return pl.pallas_call(
        functools.partial(_attn_kernel, tq=tq, tk=tk, rq=rq),
        grid=(b, h, t // tq),
        in_specs=[pl.BlockSpec((SUBLANES, LANES), lambda bi, hi, qi: (0, 0)),
                  pl.BlockSpec((1, LANES), lambda bi, hi, qi: (0, 0)),
                  pl.BlockSpec((1, tq, LANES), lambda bi, hi, qi: (bi, qi, hi)),
                  pl.BlockSpec((1, tc, LANES), kv(h)),
                  pl.BlockSpec((1, tc, LANES), kv(2 * h)),
                  pl.BlockSpec((1, t, LANES), kv(h)),
                  pl.BlockSpec((1, t, LANES), kv(2 * h))],
        out_specs=pl.BlockSpec((1, tq, LANES), lambda bi, hi, qi: (bi, qi, hi)),
        out_shape=jax.ShapeDtypeStruct((b, t, DIFF_W), BF16),
        scratch_shapes=[pltpu.VMEM((2 * tq, LANES), BF16),
                        pltpu.VMEM((tc + t, 2 * LANES), BF16),
                        pltpu.VMEM((2 * tq, LANES), F32),
                        pltpu.VMEM((2 * tq, 2 * LANES), F32)],
        compiler_params=_cparams(("arbitrary",) * 3),
        name="diff_attn",
    )(lam_pad, gn, proj_l, proj_c, proj_c, proj_l, proj_l)


def _gate_rows(a_ref, b_ref, ba_ref, bb_ref, rt_scr):
    ng = a_ref.shape[1]
    x = (a_ref[0] + ba_ref[0]).reshape(ng * SUBLANES, LANES)
    ls = jax.nn.log_sigmoid(b_ref[0] + bb_ref[0]).reshape(ng * SUBLANES, LANES)
    ji = lax.broadcasted_iota(jnp.int32, (LANES, LANES), 0)
    si = lax.broadcasted_iota(jnp.int32, (LANES, LANES), 1)
    same = (ji // CHUNK) == (si // CHUNK)
    ones_where = lambda cond: jnp.where(cond, 1.0, 0.0).astype(BF16)
    m_pre = ones_where(same & (ji <= si))
    m_suf = ones_where(same & (ji >= si))
    m_tot = ones_where(same)
    hi, mid, lo = _split3(ls)
    mm = lambda m: _dot(hi, m) + _dot(mid, m) + _dot(lo, m)
    sub = lax.broadcasted_iota(jnp.int32, x.shape, 0) % SUBLANES
    cum = jnp.where(sub % 2 == 0, mm(m_pre), mm(m_suf))
    rt = jnp.where(sub < 2, x - cum, jnp.where(sub < 4, cum, mm(m_tot)))
    rt_scr[...] = rt.reshape(ng, SUBLANES, LANES)


def _conv_silu(raw_ref, w, bias, pad_scr, t, emit):
    pad_scr[0:SUBLANES] = jnp.zeros((SUBLANES, LANES), F32)
    pad_scr[SUBLANES + t:2 * SUBLANES + t] = jnp.zeros((SUBLANES, LANES), F32)
    pad_scr[SUBLANES:SUBLANES + t] = raw_ref[0].astype(F32)
    half = MLSTM_CONV // 2
    for g in range(t // LANES):
        acc = jnp.zeros((LANES, LANES), F32) + bias
        for j in range(MLSTM_CONV):
            off = SUBLANES + g * LANES + j - half
            acc = acc + pad_scr[off:off + LANES] * w[j:j + 1]
        emit(g, acc * jax.nn.sigmoid(acc))


def _mlstm_prologue(q_ref, k_ref, v_ref, cwq_ref, cwk_ref, cbq_ref, cbk_ref,
                    pad_scr, q_scr, kt_scr, vx_scr, t, g0):
    def emit_q(g, blk):
        q_scr[g0 + g] = blk.astype(BF16)

    def emit_k(g, blk):
        kt_scr[g0 + g] = (blk * (MLSTM_DHEAD ** -0.5)).T

    _conv_silu(q_ref, cwq_ref[0], cbq_ref[0], pad_scr, t, emit_q)
    _conv_silu(k_ref, cwk_ref[0], cbk_ref[0], pad_scr, t, emit_k)
    lane = lax.broadcasted_iota(jnp.int32, (LANES, LANES), 1)
    ones_col = jnp.where(lane == 0, 1.0, 0.0).astype(BF16)
    for g in range(t // LANES):
        vx_scr[g0 + g, :, 0:LANES] = v_ref[0, g * LANES:(g + 1) * LANES, :]
        vx_scr[g0 + g, :, LANES:2 * LANES] = ones_col


def _groups(q_scr, kt_scr, vx_scr, c_scr, m_st, items, bwd, out_scr):
    d = 1 if bwd else 0
    order = (1, 0) if bwd else (0, 1)
    lane = lax.broadcasted_iota(jnp.int32, (1, LANES), 1)
    in_c = (lane < CHUNK, lane >= CHUNK)
    li = lax.broadcasted_iota(jnp.int32, (LANES, LANES), 0)
    si = lax.broadcasted_iota(jnp.int32, (LANES, LANES), 1)
    same = (li < CHUNK) == (si < CHUNK)
    mask = same & ((si >= li) if bwd else (si <= li))
    row0 = lax.broadcasted_iota(jnp.int32, (LANES, 1), 0) < CHUNK

    st = []
    for g, rt, og in items:
        r = rt[d:d + 1, :]
        cum = rt[2 + d:3 + d, :]
        totrow = rt[4 + d:5 + d, :]
        maxr = [jnp.max(jnp.where(in_c[c], r, NEG_INF), axis=1, keepdims=True) for c in (0, 1)]
        tot = [jnp.max(jnp.where(in_c[c], totrow, NEG_INF), axis=1, keepdims=True) for c in (0, 1)]
        m_b, mu_f = [None, None], [None, None]
        for c in order:
            m_b[c] = m_st
            mu_f[c] = jnp.maximum(m_st, maxr[c])
            m_st = tot[c] + mu_f[c]
        st.append(dict(g=g, og=og, r=r, cum=cum, m_b=m_b, mu_f=mu_f))
    for e in st:
        kt = kt_scr[e["g"]]
        vx = vx_scr[e["g"]]
        e["kt"], e["vx"] = kt, vx
        e["kv"] = [_dot((kt * jnp.where(in_c[c], jnp.exp(e["r"] - e["mu_f"][c]), 0.0)).astype(BF16), vx)
                   for c in (0, 1)]
        if out_scr is not None:
            e["q"] = q_scr[e["g"]]
            e["sc"] = _dot(e["q"], kt.astype(BF16))
    c_st = c_scr[...]
    for e in st:
        e["c_bf"] = [None, None]
        for c in order:
            e["c_bf"][c] = c_st.astype(BF16)
            c_st = jnp.exp(e["m_b"][c] - e["mu_f"][c]) * c_st + e["kv"][c]
    c_scr[...] = c_st
    if out_scr is not None:
        for e in st:
            m_col = jnp.where(row0, e["m_b"][0], e["m_b"][1])
            rm = jnp.where(mask, e["r"], NEG_INF)
            mu = jnp.maximum(m_col, jnp.max(rm, axis=1, keepdims=True))
            e["mu"], e["m_col"] = mu, m_col
            e["s"] = (e["sc"] * jnp.exp(rm - mu)).astype(BF16)
        for e in st:
            e["intra"] = _dot(e["s"], e["vx"])
            e["inter"] = [_dot(e["q"][c * CHUNK:(c + 1) * CHUNK], e["c_bf"][c]) for c in (0, 1)]
        for e in st:
            inter = jnp.concatenate(e["inter"], axis=0)
            tot_o = e["intra"] + jnp.exp(e["m_col"] - e["mu"]) * inter
            num = tot_o[:, 0:LANES]
            den = tot_o[:, LANES:LANES + 1]
            cum_col = jnp.sum(jnp.where(li == si, e["cum"], 0.0), axis=1, keepdims=True)
            floor = jnp.exp(-(cum_col + e["mu"]))
            out_scr[e["og"]] = num / jnp.maximum(jnp.abs(den), floor)
    return m_st


def _mlstm_kernel(ql_ref, kl_ref, vl_ref, ol_ref, qc_ref, kc_ref, vc_ref,
                  al_ref, bl_ref, ac_ref, bc_ref, ba_ref, bb_ref,
                  cwq_ref, cwk_ref, cbq_ref, cbk_ref, gn_ref, o_ref,
                  pad_scr, q_scr, kt_scr, vx_scr, rtc_scr, rtl_scr, cf_scr, cb_scr, hf_scr, hb_scr,
                  *, t, tc, gpi):
    _gate_rows(ac_ref.at[0], bc_ref.at[0], ba_ref, bb_ref, rtc_scr)
    _gate_rows(al_ref.at[0], bl_ref.at[0], ba_ref, bb_ref, rtl_scr)
    _mlstm_prologue(qc_ref, kc_ref, vc_ref, cwq_ref, cwk_ref, cbq_ref, cbk_ref,
                    pad_scr, q_scr, kt_scr, vx_scr, tc, 0)
    _mlstm_prologue(ql_ref, kl_ref, vl_ref, cwq_ref, cwk_ref, cbq_ref, cbk_ref,
                    pad_scr, q_scr, kt_scr, vx_scr, t, tc // LANES)
    cf_scr[...] = jnp.zeros_like(cf_scr)
    cb_scr[...] = jnp.zeros_like(cb_scr)
    m0 = jnp.zeros((1, 1), F32)

    def group(rt_scr, ng, g0, outs, gpi):
        def body(it, carry):
            mf, mb = carry
            items_f, items_b = [], []
            for u in range(gpi):
                gf = it * gpi + u
                gb = ng - 1 - gf
                items_f.append((g0 + gf, rt_scr[gf], gf))
                items_b.append((g0 + gb, rt_scr[gb], gb))
            args = (q_scr, kt_scr, vx_scr)
            mf = _groups(*args, cf_scr, mf, items_f, False, outs and hf_scr)
            mb = _groups(*args, cb_scr, mb, items_b, True, outs and hb_scr)
            return mf, mb
        return body

    ngc = tc // LANES
    ngl = t // LANES
    gc = math.gcd(ngc, gpi)
    gl = math.gcd(ngl, gpi)
    carry = lax.fori_loop(0, ngc // gc, group(rtc_scr, ngc, 0, None, gc), (m0, m0))
    lax.fori_loop(0, ngl // gl, group(rtl_scr, ngl, ngc, True, gl), carry)

    hm = (hf_scr[...] + hb_scr[...]).reshape(t, LANES)
    ms = jnp.mean(hm * hm, axis=1, keepdims=True)
    out = hm * lax.rsqrt(ms + RMS_EPS) * gn_ref[...] * jax.nn.sigmoid(ol_ref[0].astype(F32))
    o_ref[0] = out.astype(BF16)


def _mlstm(proj_l, proj_c, ga_l, gb_l, ga_c, gb_c, bias_a, bias_b, conv_w, conv_b, gn):
    b, t, _ = proj_l.shape
    tc = proj_c.shape[1]
    h = MLSTM_HEADS
    col = lambda off: (lambda bi, hi: (bi, 0, off + hi))
    q0 = (3 * DIFF_W) // LANES
    gspec = lambda ng: pl.BlockSpec((1, 1, ng, SUBLANES, LANES), lambda bi, hi: (bi, hi, 0, 0, 0))
    hspec = pl.BlockSpec((1, SUBLANES, LANES), lambda bi, hi: (hi, 0, 0))
    cw = jnp.pad(conv_w, ((0, SUBLANES - MLSTM_CONV), (0, 0)))
    cwspec = lambda off: pl.BlockSpec((1, SUBLANES, LANES), lambda bi, hi: (0, 0, off + hi))
    cbspec = lambda off: pl.BlockSpec((1, 1, LANES), lambda bi, hi: (0, 0, off + hi))
    tt = t + tc
    return pl.pallas_call(
        functools.partial(_mlstm_kernel, t=t, tc=tc, gpi=MLSTM_GROUPS_PER_STEP),
        grid=(b, h),
        in_specs=[pl.BlockSpec((1, t, LANES), col(q0)),
                  pl.BlockSpec((1, t, LANES), col(q0 + h)),
                  pl.BlockSpec((1, t, LANES), col(q0 + 2 * h)),
                  pl.BlockSpec((1, t, LANES), col(q0 + 3 * h)),
                  pl.BlockSpec((1, tc, LANES), col(q0)),
                  pl.BlockSpec((1, tc, LANES), col(q0 + h)),
                  pl.BlockSpec((1, tc, LANES), col(q0 + 2 * h)),
                  gspec(t // LANES), gspec(t // LANES), gspec(tc // LANES), gspec(tc // LANES),
                  hspec, hspec,
                  cwspec(0), cwspec(h), cbspec(0), cbspec(h),
                  pl.BlockSpec((1, LANES), lambda bi, hi: (0, 0))],
        out_specs=pl.BlockSpec((1, t, LANES), lambda bi, hi: (bi, 0, hi)),
        out_shape=jax.ShapeDtypeStruct((b, t, ML_W), BF16),
        scratch_shapes=[pltpu.VMEM((t + 2 * SUBLANES, LANES), F32),
                        pltpu.VMEM((tt // LANES, LANES, LANES), BF16),
                        pltpu.VMEM((tt // LANES, LANES, LANES), F32),
                        pltpu.VMEM((tt // LANES, LANES, 2 * LANES), BF16),
                        pltpu.VMEM((tc // LANES, SUBLANES, LANES), F32),
                        pltpu.VMEM((t // LANES, SUBLANES, LANES), F32),
                        pltpu.VMEM((LANES, 2 * LANES), F32),
                        pltpu.VMEM((LANES, 2 * LANES), F32),
                        pltpu.VMEM((t // LANES, LANES, LANES), F32),
                        pltpu.VMEM((t // LANES, LANES, LANES), F32)],
        compiler_params=_cparams(("arbitrary", "arbitrary")),
        name="mlstm",
    )(proj_l, proj_l, proj_l, proj_l, proj_c, proj_c, proj_c,
      ga_l, gb_l, ga_c, gb_c, bias_a, bias_b,
      cw.reshape(1, SUBLANES, 2 * ML_W), cw.reshape(1, SUBLANES, 2 * ML_W),
      conv_b.reshape(1, 1, 2 * ML_W), conv_b.reshape(1, 1, 2 * ML_W), gn)


def _layer_norm(v, g, b):
    mu = jnp.mean(v, axis=1, keepdims=True)
    c = v - mu
    var = jnp.mean(c * c, axis=1, keepdims=True)
    return c * lax.rsqrt(var + LN_EPS) * g + b


def _outproj_kernel(d_ref, m_ref, w_ref, x_ref, g1_ref, lg_ref, lb_ref, sh_ref, sc_ref,
                    x1_ref, h2_ref):
    y = _dot(d_ref[0], w_ref[0]) + _dot(m_ref[0], w_ref[1])
    v = DEEPNORM_ALPHA * x_ref[0] + g1_ref[0] * y
    x1 = _layer_norm(v, lg_ref[...], lb_ref[...])
    x1_ref[0] = x1
    h2_ref[0] = (x1 * (1.0 + sc_ref[0]) + sh_ref[0]).astype(BF16)


def _out_proj(d_lat, m_lat, w_out2, x, g1, ln_g, ln_b, sh2, sc2, *, tm):
    b, t, d = x.shape
    hw = d_lat.shape[2]
    row = pl.BlockSpec((1, 1, d), lambda bi, i: (bi, 0, 0))
    vec = pl.BlockSpec((1, d), lambda bi, i: (0, 0))
    return pl.pallas_call(
        _outproj_kernel,
        grid=(b, t // tm),
        in_specs=[pl.BlockSpec((1, tm, hw), lambda bi, i: (bi, i, 0)),
                  pl.BlockSpec((1, tm, hw), lambda bi, i: (bi, i, 0)),
                  pl.BlockSpec((2, hw, d), lambda bi, i: (0, 0, 0)),
                  pl.BlockSpec((1, tm, d), lambda bi, i: (bi, i, 0)),
                  row, vec, vec, row, row],
        out_specs=[pl.BlockSpec((1, tm, d), lambda bi, i: (bi, i, 0)),
                   pl.BlockSpec((1, tm, d), lambda bi, i: (bi, i, 0))],
        out_shape=[jax.ShapeDtypeStruct((b, t, d), F32),
                   jax.ShapeDtypeStruct((b, t, d), BF16)],
        compiler_params=_cparams(("arbitrary", "arbitrary")),
        name="out_proj",
    )(d_lat, m_lat, w_out2, x, g1, ln_g, ln_b, sh2, sc2)


def _top_rows(x, k, dst_scr):
    cur = x
    for r in range(k):
        m = jnp.max(cur, axis=0, keepdims=True)
        dst_scr[r:r + 1] = m
        if r + 1 < k:
            cur = jnp.where(cur == m, NEG_INF, cur)


def _sort16_pairs():
    def merge(lo, hi, r):
        step = r * 2
        if step < hi - lo:
            yield from merge(lo, hi, step)
            yield from merge(lo + r, hi, step)
            yield from [(i, i + r) for i in range(lo + r, hi - r, step)]
        else:
            yield (lo, lo + r)

    def sort(lo, hi):
        if hi - lo >= 1:
            mid = lo + (hi - lo) // 2
            yield from sort(lo, mid)
            yield from sort(mid + 1, hi)
            yield from merge(lo, hi, 1)

    return list(sort(0, 15))


def _top16_of_128(x, dst_scr):
    k = PEER_TOPK
    y = [x[r * SUBLANES:(r + 1) * SUBLANES] for r in range(k)]
    for a, b in _sort16_pairs():
        y[a], y[b] = jnp.maximum(y[a], y[b]), jnp.minimum(y[a], y[b])
    for r in range(k):
        m = jnp.max(y[0], axis=0, keepdims=True)
        dst_scr[r:r + 1] = m
        if r + 1 < k:
            hit = y[0] == m
            for i in range(k - 1 - r):
                y[i] = jnp.where(hit, y[i + 1], y[i])


def _rank_among(x, v):
    row = lambda i: v[i:i + 1]
    pick = jnp.where
    c1 = x < row(7)
    c2 = x < pick(c1, row(11), row(3))
    c3 = x < pick(c1, pick(c2, row(13), row(9)), pick(c2, row(5), row(1)))
    c4 = x < pick(c1, pick(c2, pick(c3, row(14), row(12)), pick(c3, row(10), row(8))),
                  pick(c2, pick(c3, row(6), row(4)), pick(c3, row(2), row(0))))
    c5 = x < row(15)
    bit = lambda c, w: jnp.where(c, float(w), 0.0)
    return bit(c1, 8) + bit(c2, 4) + bit(c3, 2) + bit(c4, 1) + bit(c5, 1)


def _peer_prep_kernel(h_ref, wq_ref, k1_ref, k2_ref, n1_ref, r2_ref, e1_ref, e2_ref,
                      v1_scr, v2_scr, tp_scr):
    q = _dot(h_ref[...], wq_ref[...])
    half = PEER_DQ // 2
    for h in range(PEER_HEADS):
        q1 = q[:, h * PEER_DQ:h * PEER_DQ + half].astype(BF16)
        q2 = q[:, h * PEER_DQ + half:(h + 1) * PEER_DQ].astype(BF16)
        s1 = _dot_nt(k1_ref[h], q1)
        s2 = _dot_nt(k2_ref[h], q2)
        _top16_of_128(s1, v1_scr)
        _top16_of_128(s2, v2_scr)
        v1 = v1_scr[...]
        v2 = v2_scr[...]
        pieces = [v1[0:1] + v2]
        pieces += [v1[a:a + 1] + v2[0:SUBLANES] for a in range(1, SUBLANES)]
        pieces += [v1[SUBLANES:PEER_TOPK] + v2[0:1]]
        cand = jnp.concatenate(pieces, axis=0)
        _top_rows(cand, PEER_TOPK, tp_scr)
        tp = tp_scr[...]
        top0 = tp[0:1]
        tau = tp[PEER_TOPK - 1:PEER_TOPK]
        z = jnp.sum(jnp.exp(tp - top0), axis=0, keepdims=True)
        cnt = jnp.zeros(s1.shape, F32)
        for bb in range(PEER_TOPK):
            n_a = jnp.sum(jnp.where(v1[bb:bb + 1] + v2 >= tau, 1.0, 0.0), axis=0, keepdims=True)
            cnt = jnp.where(s1 == v1[bb:bb + 1], n_a, cnt)
        n1_ref[h] = cnt
        r2_ref[h] = _rank_among(s2, v2).astype(BF16)
        e1_ref[h] = jnp.exp(s1 - v1[0:1]) / z
        e2_ref[h] = jnp.exp(s2 - v2[0:1]).astype(BF16)


def _peer_prep(h2, wq, k1, k2, *, tt):
    n, d = h2.shape
    hp = PEER_HEADS
    big = jax.ShapeDtypeStruct((hp, PEER_NKEYS, n), F32)
    big16 = jax.ShapeDtypeStruct((hp, PEER_NKEYS, n), BF16)
    bspec = pl.BlockSpec((hp, PEER_NKEYS, tt), lambda i: (0, 0, i))
    kspec = pl.BlockSpec((hp, PEER_NKEYS, PEER_DQ // 2), lambda i: (0, 0, 0))
    return pl.pallas_call(
        _peer_prep_kernel,
        grid=(n // tt,),
        in_specs=[pl.BlockSpec((tt, d), lambda i: (i, 0)),
                  pl.BlockSpec((d, hp * PEER_DQ), lambda i: (0, 0)),
                  kspec, kspec],
        out_specs=[bspec, bspec, bspec, bspec],
        out_shape=[big, big16, big, big16],
        scratch_shapes=[pltpu.VMEM((PEER_TOPK, tt), F32),
                        pltpu.VMEM((PEER_TOPK, tt), F32),
                        pltpu.VMEM((PEER_TOPK, tt), F32)],
        compiler_params=_cparams(("arbitrary",)),
        name="peer_prep",
    )(h2, wq, k1, k2)


def _peer_dense_kernel(h_ref, u_ref, vt_ref, n1_ref, r2_ref, e1_ref, e2_ref, x1_ref, g2_ref, lg_ref, lb_ref,
                       o_ref, ht_scr, z_scr, acc_scr, *, te):
    j = pl.program_id(1)
    nj = pl.num_programs(1)
    nk = PEER_NKEYS

    @pl.when(j == 0)
    def _():
        ht_scr[...] = h_ref[...].astype(F32).T.astype(BF16)
        acc_scr[...] = jnp.zeros_like(acc_scr)
        z_scr[1] = jnp.zeros(z_scr.shape[1:], F32)

    prev = jnp.maximum(j - 1, 0)

    tt = ht_scr.shape[1]
    tn = TILES["peer_token_slice"]

    def body(wslot, rslot):
        for n0 in range(0, tt, tn):
            z_scr[wslot, :, n0:n0 + tn] = _dot(u_ref[...], ht_scr[:, n0:n0 + tn])
            ws = []
            for a in range(te // nk):
                i1 = prev * (te // nk) + a
                g = None
                for h in range(PEER_HEADS):
                    n1row = n1_ref[h, pl.ds(i1, 1), n0:n0 + tn].astype(BF16)
                    e1row = e1_ref[h, pl.ds(i1, 1), n0:n0 + tn].astype(BF16)
                    hit = r2_ref[h, :, n0:n0 + tn] < n1row
                    term = jnp.where(hit, e2_ref[h, :, n0:n0 + tn], jnp.zeros((), BF16)) * e1row
                    g = term if g is None else g + term
                z = z_scr[rslot, a * nk:(a + 1) * nk, n0:n0 + tn]
                gelu = 0.5 * z * (1.0 + lax.erf(z * math.sqrt(0.5)))
                ws.append(gelu.astype(BF16) * g)
            acc_scr[:, n0:n0 + tn] += _dot(vt_ref[...], jnp.concatenate(ws, axis=0))

    @pl.when(j % 2 == 0)
    def _():
        body(0, 1)

    @pl.when(j % 2 == 1)
    def _():
        body(1, 0)

    @pl.when(j == nj - 1)
    def _():
        v = DEEPNORM_ALPHA * x1_ref[...] + g2_ref[0] * acc_scr[...].T
        o_ref[...] = _layer_norm(v, lg_ref[...], lb_ref[...])


def _peer_dense(h2, u_bf, vt_bf, n1, r2, e1, e2, x1, g2, ln_g, ln_b, *, tt, te):
    n, d = h2.shape
    ne = u_bf.shape[0]
    hp = PEER_HEADS
    nb = ne // te
    blocks_per_sample = n // g2.shape[0] // tt
    bspec = pl.BlockSpec((hp, PEER_NKEYS, tt), lambda i, j: (0, 0, i))
    vec = pl.BlockSpec((1, d), lambda i, j: (0, 0))
    return pl.pallas_call(
        functools.partial(_peer_dense_kernel, te=te),
        grid=(n // tt, nb + 1),
        in_specs=[pl.BlockSpec((tt, d), lambda i, j: (i, 0)),
                  pl.BlockSpec((te, d), lambda i, j: (jnp.minimum(j, nb - 1), 0)),
                  pl.BlockSpec((d, te), lambda i, j: (0, jnp.maximum(j - 1, 0))),
                  bspec, bspec, bspec, bspec,
                  pl.BlockSpec((tt, d), lambda i, j: (i, 0)),
                  pl.BlockSpec((1, 1, d), lambda i, j: (i // blocks_per_sample, 0, 0)),
                  vec, vec],
        out_specs=pl.BlockSpec((tt, d), lambda i, j: (i, 0)),
        out_shape=jax.ShapeDtypeStruct((n, d), F32),
        scratch_shapes=[pltpu.VMEM((d, tt), BF16),
                        pltpu.VMEM((2, te, tt), F32),
                        pltpu.VMEM((d, tt), F32)],
        compiler_params=_cparams(("arbitrary", "arbitrary")),
        name="peer_dense",
    )(h2, u_bf, vt_bf, n1, r2, e1, e2, x1, g2, ln_g, ln_b)


def _rope_tables(t):
    pos = jnp.arange(t, dtype=jnp.int32)
    row = (pos // GRID_W).astype(F32)
    col = (pos % GRID_W).astype(F32)
    inv = ROPE_BASE ** (-jnp.arange(ROPE_NF, dtype=F32) / ROPE_NF)
    lane = jnp.arange(LANES)
    use_col = ((lane // (2 * ROPE_NF)) % 2) == 1
    p = jnp.where(use_col[None, :], col[:, None], row[:, None])
    ang = p * inv[lane % ROPE_NF][None, :]
    sign = jnp.where((lane % (2 * ROPE_NF)) < ROPE_NF, -1.0, 1.0).astype(F32)
    return jnp.cos(ang), jnp.sin(ang) * sign[None, :]


def _gate_layout(g):
    b, _, t = g.shape
    gt = g[:, :N_GATES * MLSTM_HEADS, :].reshape(b, N_GATES, MLSTM_HEADS, t // LANES, LANES)
    gt = gt.transpose(0, 2, 3, 1, 4)
    zeros = jnp.zeros_like(gt[:, :, :, 0:1])
    a = jnp.concatenate([gt[:, :, :, 0:1], gt[:, :, :, 2:3]] + [zeros] * 6, axis=3)
    fb = jnp.concatenate([gt[:, :, :, 1:2], gt[:, :, :, 3:4]], axis=3)
    return a, jnp.tile(fb, (1, 1, 1, 4, 1))


def _gate_bias(gate_b):
    gb = gate_b.astype(F32)
    z = jnp.zeros_like(gb[0])
    a = jnp.stack([gb[0], gb[2]] + [z] * 6, axis=1)
    bm = jnp.stack([gb[1], gb[3]] * 4, axis=1)
    bc = lambda v: jnp.broadcast_to(v[:, :, None], (MLSTM_HEADS, SUBLANES, LANES))
    return bc(a), bc(bm)


def kernel(x, c, ctx, c_ctx, w_ada, b_ada, w_in, conv_w, conv_b, gate_b, diff_lambda, diff_norm_g,
           mlstm_norm_g, w_out, ln1_g, ln1_b, ln2_g, ln2_b, peer_wq, peer_keys, peer_u, peer_v):
    b, t, d = x.shape
    tc = ctx.shape[1]
    l = 0

    c_rows = jnp.concatenate([c, c_ctx[None, :]], axis=0)
    mod = _adaln(c_rows, w_ada[l], b_ada[l])
    sh1, sc1, g1, sh2, sc2, g2 = [m[:, None, :] for m in jnp.split(mod[:b], 6, axis=1)]
    csh1, csc1 = [m[:, None, :] for m in jnp.split(mod[b:b + 1], 6, axis=1)[:2]]
    csh1 = jnp.broadcast_to(csh1, (b, 1, d))
    csc1 = jnp.broadcast_to(csc1, (b, 1, d))

    w = w_in[l]
    col_scale = jnp.where(jnp.arange(MAIN_COLS) < DIFF_W, DIFF_HALF ** -0.5, 1.0).astype(F32)
    w_main = (w[:, :MAIN_COLS] * col_scale[None, :]).astype(BF16)
    wg = jnp.pad(w[:, MAIN_COLS:], ((0, 0), (0, LANES - N_GATES * MLSTM_HEADS)))
    wg_hi = wg.astype(BF16)
    wg_lo = (wg - wg_hi.astype(F32)).astype(BF16)
    wg2 = jnp.concatenate([wg_hi, wg_lo], axis=1)
    cos_t, sin_t = _rope_tables(t)
    proj_l, gates_l = _in_proj(x, sh1, sc1, w_main, wg2, cos_t, sin_t, tm=min(TILES["in_proj_rows"], t))
    proj_c, gates_c = _in_proj(ctx, csh1, csc1, w_main, wg2,
                               jnp.ones((tc, LANES), F32), jnp.zeros((tc, LANES), F32), tm=tc)

    lam_pad = jnp.pad(diff_lambda[l].astype(F32), ((0, SUBLANES - 4), (0, LANES - DIFF_HALF)))
    d_lat = _diff_attn(proj_l, proj_c, lam_pad, diff_norm_g[l].reshape(1, LANES),
                       tq=min(TILES["attn_q_rows"], t), tk=min(TILES["attn_kv_rows"], t),
                       rq=TILES["attn_row_group"])

    ga_l, gb_l = _gate_layout(gates_l)
    ga_c, gb_c = _gate_layout(gates_c)
    bias_a, bias_b = _gate_bias(gate_b[l])
    m_lat = _mlstm(proj_l, proj_c, ga_l, gb_l, ga_c, gb_c, bias_a, bias_b,
                   conv_w[l], conv_b[l], mlstm_norm_g[l].reshape(1, LANES))

    w_out2 = w_out[l].astype(BF16).reshape(2, DIFF_W, d)
    x1, h2 = _out_proj(d_lat, m_lat, w_out2, x, g1, ln1_g[l].reshape(1, d), ln1_b[l].reshape(1, d),
                       sh2, sc2, tm=min(TILES["out_proj_rows"], t))

    n = b * t
    h2f = h2.reshape(n, d)
    keys = peer_keys[l].astype(BF16)
    tt = min(TILES["peer_tokens"], t)
    n1, r2, e1, e2 = _peer_prep(h2f, peer_wq[l].astype(BF16), keys[0], keys[1], tt=tt)
    out = _peer_dense(h2f, peer_u[l].astype(BF16), peer_v[l].T.astype(BF16), n1, r2, e1, e2,
                      x1.reshape(n, d), g2, ln2_g[l].reshape(1, d), ln2_b[l].reshape(1, d),
                      tt=tt, te=TILES["peer_experts"])
    return out.reshape(b, t, d)
```

```python
import functools
import math

import jax
import jax.numpy as jnp
from jax import lax
from jax.experimental import pallas as pl
from jax.experimental.pallas import tpu as pltpu

F32 = jnp.float32
BF16 = jnp.bfloat16

LANES = 128
SUBLANES = 8
MXU_WIDTH = 256
VMEM_LIMIT = 56 * 1024 * 1024

TILES = dict(
    adaln_cols=512,
    in_proj_rows=1024, in_proj_cols=512,
    attn_q_rows=2048, attn_kv_rows=2048, attn_row_group=MXU_WIDTH,
    out_proj_rows=512,
    peer_tokens=512, peer_experts=512, peer_token_slice=MXU_WIDTH,
)

GRID_W = 64
DIFF_HEADS = 8
DIFF_HALF = 64
DIFF_W = DIFF_HEADS * 2 * DIFF_HALF
MLSTM_HEADS = 8
MLSTM_DHEAD = 128
ML_W = MLSTM_HEADS * MLSTM_DHEAD
MLSTM_CONV = 5
CHUNK = 64
MLSTM_GROUPS_PER_STEP = 4
N_GATES = 4
MAIN_COLS = 3 * DIFF_W + 4 * ML_W
ROPE_COLS = 2 * DIFF_W
ROPE_BASE = 10000.0
ROPE_NF = 16
PEER_HEADS = 8
PEER_NKEYS = 128
PEER_DQ = 256
PEER_TOPK = 16
DEPTH = 1
DEEPNORM_ALPHA = (2.0 * DEPTH) ** 0.25
LN_EPS = 1e-5
RMS_EPS = 1e-6
LAM_INIT = 0.8 - 0.6 * math.exp(-0.3 * 0)
NEG_INF = float("-inf")


def _cparams(sem):
    return pltpu.CompilerParams(dimension_semantics=sem, vmem_limit_bytes=VMEM_LIMIT)


def _dot(a, b):
    return jnp.dot(a, b, preferred_element_type=F32)


def _dot_nt(a, b):
    return lax.dot_general(a, b, (((1,), (1,)), ((), ())), preferred_element_type=F32)


def _split3(x):
    hi = x.astype(BF16)
    r1 = x - hi.astype(F32)
    mid = r1.astype(BF16)
    lo = (r1 - mid.astype(F32)).astype(BF16)
    return hi, mid, lo


def _adaln_kernel(cb_ref, w_ref, b_ref, o_ref, act_scr, *, n_rows):
    @pl.when(pl.program_id(0) == 0)
    def _():
        c = cb_ref[...]
        act_scr[...] = c * jax.nn.sigmoid(c)

    tn = w_ref.shape[1]
    o_ref[...] = jnp.zeros_like(o_ref)
    for cb in range(tn // LANES):
        w = w_ref[:, cb * LANES:(cb + 1) * LANES]
        for r in range(n_rows):
            s = jnp.sum(act_scr[r] * w, axis=0, keepdims=True)
            o_ref[r:r + 1, cb * LANES:(cb + 1) * LANES] = s + b_ref[:, cb * LANES:(cb + 1) * LANES]


def _adaln(c_rows, w_ada, b_ada):
    n_rows, k = c_rows.shape
    n = w_ada.shape[1]
    tn = TILES["adaln_cols"]
    cb = jnp.broadcast_to(c_rows[:, :, None], (n_rows, k, LANES))
    return pl.pallas_call(
        functools.partial(_adaln_kernel, n_rows=n_rows),
        grid=(n // tn,),
        in_specs=[pl.BlockSpec((n_rows, k, LANES), lambda j: (0, 0, 0)),
                  pl.BlockSpec((k, tn), lambda j: (0, j)),
                  pl.BlockSpec((1, tn), lambda j: (0, j))],
        out_specs=pl.BlockSpec((SUBLANES, tn), lambda j: (0, j)),
        out_shape=jax.ShapeDtypeStruct((SUBLANES, n), F32),
        scratch_shapes=[pltpu.VMEM((n_rows, k, LANES), F32)],
        compiler_params=_cparams(("arbitrary",)),
        name="adaln",
    )(cb, w_ada, b_ada.reshape(1, n))


def _inproj_kernel(x_ref, sh_ref, sc_ref, w_ref, wg_ref, cos_ref, sin_ref, o_ref, g_ref, h_scr,
                   *, n_rope_tiles):
    j = pl.program_id(2)

    @pl.when(j == 0)
    def _():
        h = x_ref[0] * (1.0 + sc_ref[0]) + sh_ref[0]
        hb = h.astype(BF16)
        h_scr[...] = hb
        hl = (h - hb.astype(F32)).astype(BF16)
        a = _dot(hb, wg_ref[...])
        g = a[:, 0:LANES] + a[:, LANES:2 * LANES] + _dot(hl, wg_ref[:, 0:LANES])
        g_ref[0] = g.T

    acc = _dot(h_scr[...], w_ref[...])
    tn = acc.shape[1]

    @pl.when(j < n_rope_tiles)
    def _():
        cos = cos_ref[...]
        sin = sin_ref[...]
        lane = lax.broadcasted_iota(jnp.int32, cos.shape, 1)
        first = (lane % (2 * ROPE_NF)) < ROPE_NF
        for cb in range(tn // LANES):
            a = acc[:, cb * LANES:(cb + 1) * LANES]
            sw = jnp.where(first, pltpu.roll(a, LANES - ROPE_NF, 1), pltpu.roll(a, ROPE_NF, 1))
            o_ref[0, :, cb * LANES:(cb + 1) * LANES] = (a * cos + sw * sin).astype(BF16)

    @pl.when(j >= n_rope_tiles)
    def _():
        o_ref[0] = acc.astype(BF16)


def _in_proj(x, shift, scale, w_main, wg, cos_t, sin_t, *, tm):
    b, t, d = x.shape
    n = w_main.shape[1]
    tn = TILES["in_proj_cols"]
    table = pl.BlockSpec((tm, LANES), lambda bi, i, j: (i, 0))
    return pl.pallas_call(
        functools.partial(_inproj_kernel, n_rope_tiles=ROPE_COLS // tn),
        grid=(b, t // tm, n // tn),
        in_specs=[pl.BlockSpec((1, tm, d), lambda bi, i, j: (bi, i, 0)),
                  pl.BlockSpec((1, 1, d), lambda bi, i, j: (bi, 0, 0)),
                  pl.BlockSpec((1, 1, d), lambda bi, i, j: (bi, 0, 0)),
                  pl.BlockSpec((d, tn), lambda bi, i, j: (0, j)),
                  pl.BlockSpec((d, 2 * LANES), lambda bi, i, j: (0, 0)),
                  table, table],
        out_specs=[pl.BlockSpec((1, tm, tn), lambda bi, i, j: (bi, i, j)),
                   pl.BlockSpec((1, LANES, tm), lambda bi, i, j: (bi, 0, i))],
        out_shape=[jax.ShapeDtypeStruct((b, t, n), BF16),
                   jax.ShapeDtypeStruct((b, LANES, t), F32)],
        scratch_shapes=[pltpu.VMEM((tm, d), BF16)],
        compiler_params=_cparams(("arbitrary", "arbitrary", "arbitrary")),
        name="in_proj",
    )(x, shift, scale, w_main, wg, cos_t, sin_t)


def _attn_kernel(lam_ref, gn_ref, q_ref, kc_ref, vc_ref, kl_ref, vl_ref, o_ref,
                 q2_scr, vx_scr, m_scr, acc_scr, *, tq, tk, rq):
    tc = kc_ref.shape[1]
    t = kl_ref.shape[1]

    @pl.when(pl.program_id(2) == 0)
    def _():
        lane = lax.broadcasted_iota(jnp.int32, (tc, LANES), 1)
        ones_c = jnp.where(lane == 0, 1.0, 0.0).astype(BF16)
        vx_scr[0:tc, 0:LANES] = vc_ref[0]
        vx_scr[0:tc, LANES:2 * LANES] = ones_c
        for r0 in range(0, t, tc):
            vx_scr[tc + r0:tc + r0 + tc, 0:LANES] = vl_ref[0, r0:r0 + tc, :]
            vx_scr[tc + r0:tc + r0 + tc, LANES:2 * LANES] = ones_c

    q = q_ref[0]
    qf = q.astype(F32)
    lane = lax.broadcasted_iota(jnp.int32, qf.shape, 1)
    q2_scr[0:tq] = jnp.where(lane < DIFF_HALF, qf, 0.0).astype(BF16)
    q2_scr[tq:2 * tq] = jnp.where(lane >= DIFF_HALF, qf, 0.0).astype(BF16)
    m_scr[...] = jnp.full_like(m_scr, NEG_INF)
    acc_scr[...] = jnp.zeros_like(acc_scr)

    def step(k, vx):
        nkc = k.shape[0] // LANES
        for r0 in range(0, 2 * tq, rq):
            sc = _dot_nt(q2_scr[r0:r0 + rq], k)
            cols = [sc[:, c * LANES:(c + 1) * LANES] for c in range(nkc)]
            mx = cols[0]
            for c in range(1, nkc):
                mx = jnp.maximum(mx, cols[c])
            m_prev = m_scr[r0:r0 + rq]
            m_new = jnp.maximum(m_prev, jnp.max(mx, axis=1, keepdims=True))
            alpha = jnp.exp(m_prev - m_new)
            p = jnp.concatenate([jnp.exp(cb - m_new).astype(BF16) for cb in cols], axis=1)
            alpha2 = jnp.concatenate([alpha, alpha], axis=1)
            acc_scr[r0:r0 + rq] = alpha2 * acc_scr[r0:r0 + rq] + _dot(p, vx)
            m_scr[r0:r0 + rq] = m_new

    step(kc_ref[0], vx_scr[0:tc])

    def body(j, carry):
        r = pl.multiple_of(j * tk, tk)
        step(kl_ref[0, pl.ds(r, tk), :], vx_scr[pl.ds(tc + r, tk)])
        return carry

    lax.fori_loop(0, t // tk, body, 0)

    lm = lam_ref[...]
    d1 = jnp.sum(lm[0:1] * lm[1:2], axis=1, keepdims=True)
    d2 = jnp.sum(lm[2:3] * lm[3:4], axis=1, keepdims=True)
    lam = jnp.exp(d1) - jnp.exp(d2) + LAM_INIT
    a1 = acc_scr[0:tq]
    a2 = acc_scr[tq:2 * tq]
    o1 = a1[:, 0:LANES] / a1[:, LANES:LANES + 1]
    o2 = a2[:, 0:LANES] / a2[:, LANES:LANES + 1]
    o = o1 - lam * o2
    ms = jnp.mean(o * o, axis=1, keepdims=True)
    o = o * lax.rsqrt(ms + RMS_EPS) * gn_ref[...] * (1.0 - LAM_INIT)
    o_ref[0] = o.astype(BF16)


def _diff_attn(proj_l, proj_c, lam_pad, gn, *, tq, tk, rq):
    b, t, _ = proj_l.shape
    tc = proj_c.shape[1]
    h = DIFF_HEADS
    kv = lambda off: (lambda bi, hi, qi: (bi, 0, off + hi))
    return pl.pallas_call(
        functools.partial(_attn_kernel, tq=tq, tk=tk, rq=rq),
        grid=(b, h, t // tq),
        in_specs=[pl.BlockSpec((SUBLANES, LANES), lambda bi, hi, qi: (0, 0)),
                  pl.BlockSpec((1, LANES), lambda bi, hi, qi: (0, 0)),
                  pl.BlockSpec((1, tq, LANES), lambda bi, hi, qi: (bi, qi, hi)),
                  pl.BlockSpec((1, tc, LANES), kv(h)),
                  pl.BlockSpec((1, tc, LANES), kv(2 * h)),
                  pl.BlockSpec((1, t, LANES), kv(h)),
                  pl.BlockSpec((1, t, LANES), kv(2 * h))],
        out_specs=pl.BlockSpec((1, tq, LANES), lambda bi, hi, qi: (bi, qi, hi)),
        out_shape=jax.ShapeDtypeStruct((b, t, DIFF_W), BF16),
        scratch_shapes=[pltpu.VMEM((2 * tq, LANES), BF16),
                        pltpu.VMEM((tc + t, 2 * LANES), BF16),
                        pltpu.VMEM((2 * tq, LANES), F32),
                        pltpu.VMEM((2 * tq, 2 * LANES), F32)],
        compiler_params=_cparams(("arbitrary",) * 3),
        name="diff_attn",
    )(lam_pad, gn, proj_l, proj_c, proj_c, proj_l, proj_l)


def _gate_rows(a_ref, b_ref, ba_ref, bb_ref, rt_scr):
    ng = a_ref.shape[1]
    x = (a_ref[0] + ba_ref[0]).reshape(ng * SUBLANES, LANES)
    ls = jax.nn.log_sigmoid(b_ref[0] + bb_ref[0]).reshape(ng * SUBLANES, LANES)
    ji = lax.broadcasted_iota(jnp.int32, (LANES, LANES), 0)
    si = lax.broadcasted_iota(jnp.int32, (LANES, LANES), 1)
    same = (ji // CHUNK) == (si // CHUNK)
    ones_where = lambda cond: jnp.where(cond, 1.0, 0.0).astype(BF16)
    m_pre = ones_where(same & (ji <= si))
    m_suf = ones_where(same & (ji >= si))
    m_tot = ones_where(same)
    hi, mid, lo = _split3(ls)
    mm = lambda m: _dot(hi, m) + _dot(mid, m) + _dot(lo, m)
    sub = lax.broadcasted_iota(jnp.int32, x.shape, 0) % SUBLANES
    cum = jnp.where(sub % 2 == 0, mm(m_pre), mm(m_suf))
    rt = jnp.where(sub < 2, x - cum, jnp.where(sub < 4, cum, mm(m_tot)))
    rt_scr[...] = rt.reshape(ng, SUBLANES, LANES)


def _conv_silu(raw_ref, w, bias, pad_scr, t, emit):
    pad_scr[0:SUBLANES] = jnp.zeros((SUBLANES, LANES), F32)
    pad_scr[SUBLANES + t:2 * SUBLANES + t] = jnp.zeros((SUBLANES, LANES), F32)
    pad_scr[SUBLANES:SUBLANES + t] = raw_ref[0].astype(F32)
    half = MLSTM_CONV // 2
    for g in range(t // LANES):
        acc = jnp.zeros((LANES, LANES), F32) + bias
        for j in range(MLSTM_CONV):
            off = SUBLANES + g * LANES + j - half
            acc = acc + pad_scr[off:off + LANES] * w[j:j + 1]
        emit(g, acc * jax.nn.sigmoid(acc))


def _mlstm_prologue(q_ref, k_ref, v_ref, cwq_ref, cwk_ref, cbq_ref, cbk_ref,
                    pad_scr, q_scr, kt_scr, vx_scr, t, g0):
    def emit_q(g, blk):
        q_scr[g0 + g] = blk.astype(BF16)

    def emit_k(g, blk):
        kt_scr[g0 + g] = (blk * (MLSTM_DHEAD ** -0.5)).T

    _conv_silu(q_ref, cwq_ref[0], cbq_ref[0], pad_scr, t, emit_q)
    _conv_silu(k_ref, cwk_ref[0], cbk_ref[0], pad_scr, t, emit_k)
    lane = lax.broadcasted_iota(jnp.int32, (LANES, LANES), 1)
    ones_col = jnp.where(lane == 0, 1.0, 0.0).astype(BF16)
    for g in range(t // LANES):
        vx_scr[g0 + g, :, 0:LANES] = v_ref[0, g * LANES:(g + 1) * LANES, :]
        vx_scr[g0 + g, :, LANES:2 * LANES] = ones_col


def _groups(q_scr, kt_scr, vx_scr, c_scr, m_st, items, bwd, out_scr):
    d = 1 if bwd else 0
    order = (1, 0) if bwd else (0, 1)
    lane = lax.broadcasted_iota(jnp.int32, (1, LANES), 1)
    in_c = (lane < CHUNK, lane >= CHUNK)
    li = lax.broadcasted_iota(jnp.int32, (LANES, LANES), 0)
    si = lax.broadcasted_iota(jnp.int32, (LANES, LANES), 1)
    same = (li < CHUNK) == (si < CHUNK)
    mask = same & ((si >= li) if bwd else (si <= li))
    row0 = lax.broadcasted_iota(jnp.int32, (LANES, 1), 0) < CHUNK

    st = []
    for g, rt, og in items:
        r = rt[d:d + 1, :]
        cum = rt[2 + d:3 + d, :]
        totrow = rt[4 + d:5 + d, :]
        maxr = [jnp.max(jnp.where(in_c[c], r, NEG_INF), axis=1, keepdims=True) for c in (0, 1)]
        tot = [jnp.max(jnp.where(in_c[c], totrow, NEG_INF), axis=1, keepdims=True) for c in (0, 1)]
        m_b, mu_f = [None, None], [None, None]
        for c in order:
            m_b[c] = m_st
            mu_f[c] = jnp.maximum(m_st, maxr[c])
            m_st = tot[c] + mu_f[c]
        st.append(dict(g=g, og=og, r=r, cum=cum, m_b=m_b, mu_f=mu_f))
    for e in st:
        kt = kt_scr[e["g"]]
        vx = vx_scr[e["g"]]
        e["kt"], e["vx"] = kt, vx
        e["kv"] = [_dot((kt * jnp.where(in_c[c], jnp.exp(e["r"] - e["mu_f"][c]), 0.0)).astype(BF16), vx)
                   for c in (0, 1)]
        if out_scr is not None:
            e["q"] = q_scr[e["g"]]
            e["sc"] = _dot(e["q"], kt.astype(BF16))
    c_st = c_scr[...]
    for e in st:
        e["c_bf"] = [None, None]
        for c in order:
            e["c_bf"][c] = c_st.astype(BF16)
            c_st = jnp.exp(e["m_b"][c] - e["mu_f"][c]) * c_st + e["kv"][c]
    c_scr[...] = c_st
    if out_scr is not None:
        for e in st:
            m_col = jnp.where(row0, e["m_b"][0], e["m_b"][1])
            rm = jnp.where(mask, e["r"], NEG_INF)
            mu = jnp.maximum(m_col, jnp.max(rm, axis=1, keepdims=True))
            e["mu"], e["m_col"] = mu, m_col
            e["s"] = (e["sc"] * jnp.exp(rm - mu)).astype(BF16)
        for e in st:
            e["intra"] = _dot(e["s"], e["vx"])
            e["inter"] = [_dot(e["q"][c * CHUNK:(c + 1) * CHUNK], e["c_bf"][c]) for c in (0, 1)]
        for e in st:
            inter = jnp.concatenate(e["inter"], axis=0)
            tot_o = e["intra"] + jnp.exp(e["m_col"] - e["mu"]) * inter
            num = tot_o[:, 0:LANES]
            den = tot_o[:, LANES:LANES + 1]
            cum_col = jnp.sum(jnp.where(li == si, e["cum"], 0.0), axis=1, keepdims=True)
            floor = jnp.exp(-(cum_col + e["mu"]))
            out_scr[e["og"]] = num / jnp.maximum(jnp.abs(den), floor)
    return m_st


def _mlstm_kernel(ql_ref, kl_ref, vl_ref, ol_ref, qc_ref, kc_ref, vc_ref,
                  al_ref, bl_ref, ac_ref, bc_ref, ba_ref, bb_ref,
                  cwq_ref, cwk_ref, cbq_ref, cbk_ref, gn_ref, o_ref,
                  pad_scr, q_scr, kt_scr, vx_scr, rtc_scr, rtl_scr, cf_scr, cb_scr, hf_scr, hb_scr,
                  *, t, tc, gpi):
    _gate_rows(ac_ref.at[0], bc_ref.at[0], ba_ref, bb_ref, rtc_scr)
    _gate_rows(al_ref.at[0], bl_ref.at[0], ba_ref, bb_ref, rtl_scr)
    _mlstm_prologue(qc_ref, kc_ref, vc_ref, cwq_ref, cwk_ref, cbq_ref, cbk_ref,
                    pad_scr, q_scr, kt_scr, vx_scr, tc, 0)
    _mlstm_prologue(ql_ref, kl_ref, vl_ref, cwq_ref, cwk_ref, cbq_ref, cbk_ref,
                    pad_scr, q_scr, kt_scr, vx_scr, t, tc // LANES)
    cf_scr[...] = jnp.zeros_like(cf_scr)
    cb_scr[...] = jnp.zeros_like(cb_scr)
    m0 = jnp.zeros((1, 1), F32)

    def group(rt_scr, ng, g0, outs, gpi):
        def body(it, carry):
            mf, mb = carry
            items_f, items_b = [], []
            for u in range(gpi):
                gf = it * gpi + u
                gb = ng - 1 - gf
                items_f.append((g0 + gf, rt_scr[gf], gf))
                items_b.append((g0 + gb, rt_scr[gb], gb))
            args = (q_scr, kt_scr, vx_scr)
            mf = _groups(*args, cf_scr, mf, items_f, False, outs and hf_scr)
            mb = _groups(*args, cb_scr, mb, items_b, True, outs and hb_scr)
            return mf, mb
        return body

    ngc = tc // LANES
    ngl = t // LANES
    gc = math.gcd(ngc, gpi)
    gl = math.gcd(ngl, gpi)
    carry = lax.fori_loop(0, ngc // gc, group(rtc_scr, ngc, 0, None, gc), (m0, m0))
    lax.fori_loop(0, ngl // gl, group(rtl_scr, ngl, ngc, True, gl), carry)

    hm = (hf_scr[...] + hb_scr[...]).reshape(t, LANES)
    ms = jnp.mean(hm * hm, axis=1, keepdims=True)
    out = hm * lax.rsqrt(ms + RMS_EPS) * gn_ref[...] * jax.nn.sigmoid(ol_ref[0].astype(F32))
    o_ref[0] = out.astype(BF16)


def _mlstm(proj_l, proj_c, ga_l, gb_l, ga_c, gb_c, bias_a, bias_b, conv_w, conv_b, gn):
    b, t, _ = proj_l.shape
    tc = proj_c.shape[1]
    h = MLSTM_HEADS
    col = lambda off: (lambda bi, hi: (bi, 0, off + hi))
    q0 = (3 * DIFF_W) // LANES
    gspec = lambda ng: pl.BlockSpec((1, 1, ng, SUBLANES, LANES), lambda bi, hi: (bi, hi, 0, 0, 0))
    hspec = pl.BlockSpec((1, SUBLANES, LANES), lambda bi, hi: (hi, 0, 0))
    cw = jnp.pad(conv_w, ((0, SUBLANES - MLSTM_CONV), (0, 0)))
    cwspec = lambda off: pl.BlockSpec((1, SUBLANES, LANES), lambda bi, hi: (0, 0, off + hi))
    cbspec = lambda off: pl.BlockSpec((1, 1, LANES), lambda bi, hi: (0, 0, off + hi))
    tt = t + tc
    return pl.pallas_call(
        functools.partial(_mlstm_kernel, t=t, tc=tc, gpi=MLSTM_GROUPS_PER_STEP),
        grid=(b, h),
        in_specs=[pl.BlockSpec((1, t, LANES), col(q0)),
                  pl.BlockSpec((1, t, LANES), col(q0 + h)),
                  pl.BlockSpec((1, t, LANES), col(q0 + 2 * h)),
                  pl.BlockSpec((1, t, LANES), col(q0 + 3 * h)),
                  pl.BlockSpec((1, tc, LANES), col(q0)),
                  pl.BlockSpec((1, tc, LANES), col(q0 + h)),
                  pl.BlockSpec((1, tc, LANES), col(q0 + 2 * h)),
                  gspec(t // LANES), gspec(t // LANES), gspec(tc // LANES), gspec(tc // LANES),
                  hspec, hspec,
                  cwspec(0), cwspec(h), cbspec(0), cbspec(h),
                  pl.BlockSpec((1, LANES), lambda bi, hi: (0, 0))],
        out_specs=pl.BlockSpec((1, t, LANES), lambda bi, hi: (bi, 0, hi)),
        out_shape=jax.ShapeDtypeStruct((b, t, ML_W), BF16),
        scratch_shapes=[pltpu.VMEM((t + 2 * SUBLANES, LANES), F32),
                        pltpu.VMEM((tt // LANES, LANES, LANES), BF16),
                        pltpu.VMEM((tt // LANES, LANES, LANES), F32),
                        pltpu.VMEM((tt // LANES, LANES, 2 * LANES), BF16),
                        pltpu.VMEM((tc // LANES, SUBLANES, LANES), F32),
                        pltpu.VMEM((t // LANES, SUBLANES, LANES), F32),
                        pltpu.VMEM((LANES, 2 * LANES), F32),
                        pltpu.VMEM((LANES, 2 * LANES), F32),
                        pltpu.VMEM((t // LANES, LANES, LANES), F32),
                        pltpu.VMEM((t // LANES, LANES, LANES), F32)],
        compiler_params=_cparams(("arbitrary", "arbitrary")),
        name="mlstm",
    )(proj_l, proj_l, proj_l, proj_l, proj_c, proj_c, proj_c,
      ga_l, gb_l, ga_c, gb_c, bias_a, bias_b,
      cw.reshape(1, SUBLANES, 2 * ML_W), cw.reshape(1, SUBLANES, 2 * ML_W),
      conv_b.reshape(1, 1, 2 * ML_W), conv_b.reshape(1, 1, 2 * ML_W), gn)


def _layer_norm(v, g, b):
    mu = jnp.mean(v, axis=1, keepdims=True)
    c = v - mu
    var = jnp.mean(c * c, axis=1, keepdims=True)
    return c * lax.rsqrt(var + LN_EPS) * g + b


def _outproj_kernel(d_ref, m_ref, w_ref, x_ref, g1_ref, lg_ref, lb_ref, sh_ref, sc_ref,
                    x1_ref, h2_ref):
    y = _dot(d_ref[0], w_ref[0]) + _dot(m_ref[0], w_ref[1])
    v = DEEPNORM_ALPHA * x_ref[0] + g1_ref[0] * y
    x1 = _layer_norm(v, lg_ref[...], lb_ref[...])
    x1_ref[0] = x1
    h2_ref[0] = (x1 * (1.0 + sc_ref[0]) + sh_ref[0]).astype(BF16)


def _out_proj(d_lat, m_lat, w_out2, x, g1, ln_g, ln_b, sh2, sc2, *, tm):
    b, t, d = x.shape
    hw = d_lat.shape[2]
    row = pl.BlockSpec((1, 1, d), lambda bi, i: (bi, 0, 0))
    vec = pl.BlockSpec((1, d), lambda bi, i: (0, 0))
    return pl.pallas_call(
        _outproj_kernel,
        grid=(b, t // tm),
        in_specs=[pl.BlockSpec((1, tm, hw), lambda bi, i: (bi, i, 0)),
                  pl.BlockSpec((1, tm, hw), lambda bi, i: (bi, i, 0)),
                  pl.BlockSpec((2, hw, d), lambda bi, i: (0, 0, 0)),
                  pl.BlockSpec((1, tm, d), lambda bi, i: (bi, i, 0)),
                  row, vec, vec, row, row],
        out_specs=[pl.BlockSpec((1, tm, d), lambda bi, i: (bi, i, 0)),
                   pl.BlockSpec((1, tm, d), lambda bi, i: (bi, i, 0))],
        out_shape=[jax.ShapeDtypeStruct((b, t, d), F32),
                   jax.ShapeDtypeStruct((b, t, d), BF16)],
        compiler_params=_cparams(("arbitrary", "arbitrary")),
        name="out_proj",
    )(d_lat, m_lat, w_out2, x, g1, ln_g, ln_b, sh2, sc2)


def _top_rows(x, k, dst_scr):
    cur = x
    for r in range(k):
        m = jnp.max(cur, axis=0, keepdims=True)
        dst_scr[r:r + 1] = m
        if r + 1 < k:
            cur = jnp.where(cur == m, NEG_INF, cur)


def _sort16_pairs():
    def merge(lo, hi, r):
        step = r * 2
        if step < hi - lo:
            yield from merge(lo, hi, step)
            yield from merge(lo + r, hi, step)
            yield from [(i, i + r) for i in range(lo + r, hi - r, step)]
        else:
            yield (lo, lo + r)

    def sort(lo, hi):
        if hi - lo >= 1:
            mid = lo + (hi - lo) // 2
            yield from sort(lo, mid)
            yield from sort(mid + 1, hi)
            yield from merge(lo, hi, 1)

    return list(sort(0, 15))


def _top16_of_128(x, dst_scr):
    k = PEER_TOPK
    y = [x[r * SUBLANES:(r + 1) * SUBLANES] for r in range(k)]
    for a, b in _sort16_pairs():
        y[a], y[b] = jnp.maximum(y[a], y[b]), jnp.minimum(y[a], y[b])
    for r in range(k):
        m = jnp.max(y[0], axis=0, keepdims=True)
        dst_scr[r:r + 1] = m
        if r + 1 < k:
            hit = y[0] == m
            for i in range(k - 1 - r):
                y[i] = jnp.where(hit, y[i + 1], y[i])


def _rank_among(x, v):
    row = lambda i: v[i:i + 1]
    pick = jnp.where
    c1 = x < row(7)
    c2 = x < pick(c1, row(11), row(3))
    c3 = x < pick(c1, pick(c2, row(13), row(9)), pick(c2, row(5), row(1)))
    c4 = x < pick(c1, pick(c2, pick(c3, row(14), row(12)), pick(c3, row(10), row(8))),
                  pick(c2, pick(c3, row(6), row(4)), pick(c3, row(2), row(0))))
    c5 = x < row(15)
    bit = lambda c, w: jnp.where(c, float(w), 0.0)
    return bit(c1, 8) + bit(c2, 4) + bit(c3, 2) + bit(c4, 1) + bit(c5, 1)


def _peer_prep_kernel(h_ref, wq_ref, k1_ref, k2_ref, n1_ref, r2_ref, e1_ref, e2_ref,
                      v1_scr, v2_scr, tp_scr):
    q = _dot(h_ref[...], wq_ref[...])
    half = PEER_DQ // 2
    for h in range(PEER_HEADS):
        q1 = q[:, h * PEER_DQ:h * PEER_DQ + half].astype(BF16)
        q2 = q[:, h * PEER_DQ + half:(h + 1) * PEER_DQ].astype(BF16)
        s1 = _dot_nt(k1_ref[h], q1)
        s2 = _dot_nt(k2_ref[h], q2)
        _top16_of_128(s1, v1_scr)
        _top16_of_128(s2, v2_scr)
        v1 = v1_scr[...]
        v2 = v2_scr[...]
        pieces = [v1[0:1] + v2]
        pieces += [v1[a:a + 1] + v2[0:SUBLANES] for a in range(1, SUBLANES)]
        pieces += [v1[SUBLANES:PEER_TOPK] + v2[0:1]]
        cand = jnp.concatenate(pieces, axis=0)
        _top_rows(cand, PEER_TOPK, tp_scr)
        tp = tp_scr[...]
        top0 = tp[0:1]
        tau = tp[PEER_TOPK - 1:PEER_TOPK]
        z = jnp.sum(jnp.exp(tp - top0), axis=0, keepdims=True)
        cnt = jnp.zeros(s1.shape, F32)
        for bb in range(PEER_TOPK):
            n_a = jnp.sum(jnp.where(v1[bb:bb + 1] + v2 >= tau, 1.0, 0.0), axis=0, keepdims=True)
            cnt = jnp.where(s1 == v1[bb:bb + 1], n_a, cnt)
        n1_ref[h] = cnt
        r2_ref[h] = _rank_among(s2, v2).astype(BF16)
        e1_ref[h] = jnp.exp(s1 - v1[0:1]) / z
        e2_ref[h] = jnp.exp(s2 - v2[0:1]).astype(BF16)


def _peer_prep(h2, wq, k1, k2, *, tt):
    n, d = h2.shape
    hp = PEER_HEADS
    big = jax.ShapeDtypeStruct((hp, PEER_NKEYS, n), F32)
    big16 = jax.ShapeDtypeStruct((hp, PEER_NKEYS, n), BF16)
    bspec = pl.BlockSpec((hp, PEER_NKEYS, tt), lambda i: (0, 0, i))
    kspec = pl.BlockSpec((hp, PEER_NKEYS, PEER_DQ // 2), lambda i: (0, 0, 0))
    return pl.pallas_call(
        _peer_prep_kernel,
        grid=(n // tt,),
        in_specs=[pl.BlockSpec((tt, d), lambda i: (i, 0)),
                  pl.BlockSpec((d, hp * PEER_DQ), lambda i: (0, 0)),
                  kspec, kspec],
        out_specs=[bspec, bspec, bspec, bspec],
        out_shape=[big, big16, big, big16],
        scratch_shapes=[pltpu.VMEM((PEER_TOPK, tt), F32),
                        pltpu.VMEM((PEER_TOPK, tt), F32),
                        pltpu.VMEM((PEER_TOPK, tt), F32)],
        compiler_params=_cparams(("arbitrary",)),
        name="peer_prep",
    )(h2, wq, k1, k2)


def _transpose_cast_kernel(v_ref, o_ref):
    o_ref[...] = v_ref[...].T.astype(BF16)


def _transpose_cast(v, *, te):
    ne, d = v.shape
    return pl.pallas_call(
        _transpose_cast_kernel,
        grid=(ne // te,),
        in_specs=[pl.BlockSpec((te, d), lambda j: (j, 0))],
        out_specs=pl.BlockSpec((d, te), lambda j: (0, j)),
        out_shape=jax.ShapeDtypeStruct((d, ne), BF16),
        compiler_params=_cparams(("arbitrary",)),
        name="peer_v_layout",
    )(v)


def _peer_dense_kernel(h_ref, u_ref, vt_ref, n1_ref, r2_ref, e1_ref, e2_ref, x1_ref, g2_ref, lg_ref, lb_ref,
                       o_ref, ht_scr, z_scr, acc_scr, *, te):
    j = pl.program_id(1)
    nj = pl.num_programs(1)
    nk = PEER_NKEYS

    @pl.when(j == 0)
    def _():
        ht_scr[...] = h_ref[...].astype(F32).T.astype(BF16)
        acc_scr[...] = jnp.zeros_like(acc_scr)
        z_scr[1] = jnp.zeros(z_scr.shape[1:], F32)

    prev = jnp.maximum(j - 1, 0)

    tt = ht_scr.shape[1]
    tn = TILES["peer_token_slice"]

    def body(wslot, rslot):
        for n0 in range(0, tt, tn):
            z_scr[wslot, :, n0:n0 + tn] = _dot(u_ref[...], ht_scr[:, n0:n0 + tn])
            ws = []
            for a in range(te // nk):
                i1 = prev * (te // nk) + a
                g = None
                for h in range(PEER_HEADS):
                    n1row = n1_ref[h, pl.ds(i1, 1), n0:n0 + tn].astype(BF16)
                    e1row = e1_ref[h, pl.ds(i1, 1), n0:n0 + tn].astype(BF16)
                    hit = r2_ref[h, :, n0:n0 + tn] < n1row
                    term = jnp.where(hit, e2_ref[h, :, n0:n0 + tn], jnp.zeros((), BF16)) * e1row
                    g = term if g is None else g + term
                z = z_scr[rslot, a * nk:(a + 1) * nk, n0:n0 + tn]
                gelu = 0.5 * z * (1.0 + lax.erf(z * math.sqrt(0.5)))
                ws.append(gelu.astype(BF16) * g)
            acc_scr[:, n0:n0 + tn] += _dot(vt_ref[...], jnp.concatenate(ws, axis=0))

    @pl.when(j % 2 == 0)
    def _():
        body(0, 1)

    @pl.when(j % 2 == 1)
    def _():
        body(1, 0)

    @pl.when(j == nj - 1)
    def _():
        v = DEEPNORM_ALPHA * x1_ref[...] + g2_ref[0] * acc_scr[...].T
        o_ref[...] = _layer_norm(v, lg_ref[...], lb_ref[...])


def _peer_dense(h2, u_bf, vt_bf, n1, r2, e1, e2, x1, g2, ln_g, ln_b, *, tt, te):
    n, d = h2.shape
    ne = u_bf.shape[0]
    hp = PEER_HEADS
    nb = ne // te
    blocks_per_sample = n // g2.shape[0] // tt
    bspec = pl.BlockSpec((hp, PEER_NKEYS, tt), lambda i, j: (0, 0, i))
    vec = pl.BlockSpec((1, d), lambda i, j: (0, 0))
    return pl.pallas_call(
        functools.partial(_peer_dense_kernel, te=te),
        grid=(n // tt, nb + 1),
        in_specs=[pl.BlockSpec((tt, d), lambda i, j: (i, 0)),
                  pl.BlockSpec((te, d), lambda i, j: (jnp.minimum(j, nb - 1), 0)),
                  pl.BlockSpec((d, te), lambda i, j: (0, jnp.maximum(j - 1, 0))),
                  bspec, bspec, bspec, bspec,
                  pl.BlockSpec((tt, d), lambda i, j: (i, 0)),
                  pl.BlockSpec((1, 1, d), lambda i, j: (i // blocks_per_sample, 0, 0)),
                  vec, vec],
        out_specs=pl.BlockSpec((tt, d), lambda i, j: (i, 0)),
        out_shape=jax.ShapeDtypeStruct((n, d), F32),
        scratch_shapes=[pltpu.VMEM((d, tt), BF16),
                        pltpu.VMEM((2, te, tt), F32),
                        pltpu.VMEM((d, tt), F32)],
        compiler_params=_cparams(("arbitrary", "arbitrary")),
        name="peer_dense",
    )(h2, u_bf, vt_bf, n1, r2, e1, e2, x1, g2, ln_g, ln_b)


def _rope_tables(t):
    pos = jnp.arange(t, dtype=jnp.int32)
    row = (pos // GRID_W).astype(F32)
    col = (pos % GRID_W).astype(F32)
    inv = ROPE_BASE ** (-jnp.arange(ROPE_NF, dtype=F32) / ROPE_NF)
    lane = jnp.arange(LANES)
    use_col = ((lane // (2 * ROPE_NF)) % 2) == 1
    p = jnp.where(use_col[None, :], col[:, None], row[:, None])
    ang = p * inv[lane % ROPE_NF][None, :]
    sign = jnp.where((lane % (2 * ROPE_NF)) < ROPE_NF, -1.0, 1.0).astype(F32)
    return jnp.cos(ang), jnp.sin(ang) * sign[None, :]


def _gate_layout(g):
    b, _, t = g.shape
    gt = g[:, :N_GATES * MLSTM_HEADS, :].reshape(b, N_GATES, MLSTM_HEADS, t // LANES, LANES)
    gt = gt.transpose(0, 2, 3, 1, 4)
    zeros = jnp.zeros_like(gt[:, :, :, 0:1])
    a = jnp.concatenate([gt[:, :, :, 0:1], gt[:, :, :, 2:3]] + [zeros] * 6, axis=3)
    fb = jnp.concatenate([gt[:, :, :, 1:2], gt[:, :, :, 3:4]], axis=3)
    return a, jnp.tile(fb, (1, 1, 1, 4, 1))


def _gate_bias(gate_b):
    gb = gate_b.astype(F32)
    z = jnp.zeros_like(gb[0])
    a = jnp.stack([gb[0], gb[2]] + [z] * 6, axis=1)
    bm = jnp.stack([gb[1], gb[3]] * 4, axis=1)
    bc = lambda v: jnp.broadcast_to(v[:, :, None], (MLSTM_HEADS, SUBLANES, LANES))
    return bc(a), bc(bm)


def kernel(x, c, ctx, c_ctx, w_ada, b_ada, w_in, conv_w, conv_b, gate_b, diff_lambda, diff_norm_g,
           mlstm_norm_g, w_out, ln1_g, ln1_b, ln2_g, ln2_b, peer_wq, peer_keys, peer_u, peer_v):
    b, t, d = x.shape
    tc = ctx.shape[1]
    l = 0

    c_rows = jnp.concatenate([c, c_ctx[None, :]], axis=0)
    mod = _adaln(c_rows, w_ada[l], b_ada[l])
    sh1, sc1, g1, sh2, sc2, g2 = [m[:, None, :] for m in jnp.split(mod[:b], 6, axis=1)]
    csh1, csc1 = [m[:, None, :] for m in jnp.split(mod[b:b + 1], 6, axis=1)[:2]]

    w = w_in[l]
    col_scale = jnp.where(jnp.arange(MAIN_COLS) < DIFF_W, DIFF_HALF ** -0.5, 1.0).astype(F32)
    w_main = (w[:, :MAIN_COLS] * col_scale[None, :]).astype(BF16)
    wg = jnp.pad(w[:, MAIN_COLS:], ((0, 0), (0, LANES - N_GATES * MLSTM_HEADS)))
    wg_hi = wg.astype(BF16)
    wg_lo = (wg - wg_hi.astype(F32)).astype(BF16)
    wg2 = jnp.concatenate([wg_hi, wg_lo], axis=1)
    cos_t, sin_t = _rope_tables(t)
    proj_l, gates_l = _in_proj(x, sh1, sc1, w_main, wg2, cos_t, sin_t, tm=min(TILES["in_proj_rows"], t))
    nc = b * tc
    proj_c, gates_c = _in_proj(ctx.reshape(1, nc, d), csh1, csc1, w_main, wg2,
                               jnp.ones((nc, LANES), F32), jnp.zeros((nc, LANES), F32), tm=nc)
    proj_c = proj_c.reshape(b, tc, MAIN_COLS)
    gates_c = gates_c.reshape(LANES, b, tc).transpose(1, 0, 2)

    lam_pad = jnp.pad(diff_lambda[l].astype(F32), ((0, SUBLANES - 4), (0, LANES - DIFF_HALF)))
    d_lat = _diff_attn(proj_l, proj_c, lam_pad, diff_norm_g[l].reshape(1, LANES),
                       tq=min(TILES["attn_q_rows"], t), tk=min(TILES["attn_kv_rows"], t),
                       rq=TILES["attn_row_group"])

    ga_l, gb_l = _gate_layout(gates_l)
    ga_c, gb_c = _gate_layout(gates_c)
    bias_a, bias_b = _gate_bias(gate_b[l])
    m_lat = _mlstm(proj_l, proj_c, ga_l, gb_l, ga_c, gb_c, bias_a, bias_b,
                   conv_w[l], conv_b[l], mlstm_norm_g[l].reshape(1, LANES))

    w_out2 = w_out[l].astype(BF16).reshape(2, DIFF_W, d)
    x1, h2 = _out_proj(d_lat, m_lat, w_out2, x, g1, ln1_g[l].reshape(1, d), ln1_b[l].reshape(1, d),
                       sh2, sc2, tm=min(TILES["out_proj_rows"], t))

    n = b * t
    h2f = h2.reshape(n, d)
    keys = peer_keys[l].astype(BF16)
    tt = min(TILES["peer_tokens"], t)
    n1, r2, e1, e2 = _peer_prep(h2f, peer_wq[l].astype(BF16), keys[0], keys[1], tt=tt)
    vt_bf = _transpose_cast(peer_v[l], te=TILES["peer_experts"])
    out = _peer_dense(h2f, peer_u[l].astype(BF16), vt_bf, n1, r2, e1, e2,
                      x1.reshape(n, d), g2, ln2_g[l].reshape(1, d), ln2_b[l].reshape(1, d),
                      tt=tt, te=TILES["peer_experts"])
    return out.reshape(b, t, d)
```

```python
import functools
import math

import jax
import jax.numpy as jnp
from jax import lax
from jax.experimental import pallas as pl
from jax.experimental.pallas import tpu as pltpu

F32 = jnp.float32
BF16 = jnp.bfloat16

LANES = 128
SUBLANES = 8
MXU_WIDTH = 256
VMEM_LIMIT = 56 * 1024 * 1024

TILES = dict(
    adaln_cols=512,
    in_proj_rows=1024, in_proj_cols=512,
    attn_q_rows=2048, attn_kv_rows=2048, attn_row_group=MXU_WIDTH,
    out_proj_rows=512,
    peer_tokens=512, peer_experts=512, peer_token_slice=MXU_WIDTH,
)

GRID_W = 64
DIFF_HEADS = 8
DIFF_HALF = 64
DIFF_W = DIFF_HEADS * 2 * DIFF_HALF
MLSTM_HEADS = 8
MLSTM_DHEAD = 128
ML_W = MLSTM_HEADS * MLSTM_DHEAD
MLSTM_CONV = 5
CHUNK = 64
MLSTM_GROUPS_PER_STEP = 4
N_GATES = 4
MAIN_COLS = 3 * DIFF_W + 4 * ML_W
ROPE_COLS = 2 * DIFF_W
ROPE_BASE = 10000.0
ROPE_NF = 16
PEER_HEADS = 8
PEER_NKEYS = 128
PEER_DQ = 256
PEER_TOPK = 16
DEPTH = 1
DEEPNORM_ALPHA = (2.0 * DEPTH) ** 0.25
LN_EPS = 1e-5
RMS_EPS = 1e-6
LAM_INIT = 0.8 - 0.6 * math.exp(-0.3 * 0)
NEG_INF = float("-inf")


def _cparams(sem):
    return pltpu.CompilerParams(dimension_semantics=sem, vmem_limit_bytes=VMEM_LIMIT)


def _dot(a, b):
    return jnp.dot(a, b, preferred_element_type=F32)


def _dot_nt(a, b):
    return lax.dot_general(a, b, (((1,), (1,)), ((), ())), preferred_element_type=F32)


def _split3(x):
    hi = x.astype(BF16)
    r1 = x - hi.astype(F32)
    mid = r1.astype(BF16)
    lo = (r1 - mid.astype(F32)).astype(BF16)
    return hi, mid, lo


def _adaln_kernel(cb_ref, w_ref, b_ref, o_ref, act_scr, *, n_rows):
    @pl.when(pl.program_id(0) == 0)
    def _():
        c = cb_ref[...]
        act_scr[...] = c * jax.nn.sigmoid(c)

    tn = w_ref.shape[1]
    o_ref[...] = jnp.zeros_like(o_ref)
    for cb in range(tn // LANES):
        w = w_ref[:, cb * LANES:(cb + 1) * LANES]
        for r in range(n_rows):
            s = jnp.sum(act_scr[r] * w, axis=0, keepdims=True)
            o_ref[r:r + 1, cb * LANES:(cb + 1) * LANES] = s + b_ref[:, cb * LANES:(cb + 1) * LANES]


def _adaln(c_rows, w_ada, b_ada):
    n_rows, k = c_rows.shape
    n = w_ada.shape[1]
    tn = TILES["adaln_cols"]
    cb = jnp.broadcast_to(c_rows[:, :, None], (n_rows, k, LANES))
    return pl.pallas_call(
        functools.partial(_adaln_kernel, n_rows=n_rows),
        grid=(n // tn,),
        in_specs=[pl.BlockSpec((n_rows, k, LANES), lambda j: (0, 0, 0)),
                  pl.BlockSpec((k, tn), lambda j: (0, j)),
                  pl.BlockSpec((1, tn), lambda j: (0, j))],
        out_specs=pl.BlockSpec((SUBLANES, tn), lambda j: (0, j)),
        out_shape=jax.ShapeDtypeStruct((SUBLANES, n), F32),
        scratch_shapes=[pltpu.VMEM((n_rows, k, LANES), F32)],
        compiler_params=_cparams(("arbitrary",)),
        name="adaln",
    )(cb, w_ada, b_ada.reshape(1, n))


def _inproj_kernel(x_ref, sh_ref, sc_ref, w_ref, wg_ref, cos_ref, sin_ref, o_ref, g_ref, h_scr,
                   *, n_rope_tiles):
    j = pl.program_id(2)

    @pl.when(j == 0)
    def _():
        h = x_ref[0] * (1.0 + sc_ref[0]) + sh_ref[0]
        hb = h.astype(BF16)
        h_scr[...] = hb
        hl = (h - hb.astype(F32)).astype(BF16)
        a = _dot(hb, wg_ref[...])
        g = a[:, 0:LANES] + a[:, LANES:2 * LANES] + _dot(hl, wg_ref[:, 0:LANES])
        g_ref[0] = g.T

    acc = _dot_nt(h_scr[...], w_ref[...])
    tn = acc.shape[1]

    @pl.when(j < n_rope_tiles)
    def _():
        cos = cos_ref[...]
        sin = sin_ref[...]
        lane = lax.broadcasted_iota(jnp.int32, cos.shape, 1)
        first = (lane % (2 * ROPE_NF)) < ROPE_NF
        for cb in range(tn // LANES):
            a = acc[:, cb * LANES:(cb + 1) * LANES]
            sw = jnp.where(first, pltpu.roll(a, LANES - ROPE_NF, 1), pltpu.roll(a, ROPE_NF, 1))
            o_ref[0, :, cb * LANES:(cb + 1) * LANES] = (a * cos + sw * sin).astype(BF16)

    @pl.when(j >= n_rope_tiles)
    def _():
        o_ref[0] = acc.astype(BF16)


def _in_proj(x, shift, scale, w_main, wg, cos_t, sin_t, *, tm):
    b, t, d = x.shape
    n = w_main.shape[0]
    tn = TILES["in_proj_cols"]
    table = pl.BlockSpec((tm, LANES), lambda bi, i, j: (i, 0))
    return pl.pallas_call(
        functools.partial(_inproj_kernel, n_rope_tiles=ROPE_COLS // tn),
        grid=(b, t // tm, n // tn),
        in_specs=[pl.BlockSpec((1, tm, d), lambda bi, i, j: (bi, i, 0)),
                  pl.BlockSpec((1, 1, d), lambda bi, i, j: (bi, 0, 0)),
                  pl.BlockSpec((1, 1, d), lambda bi, i, j: (bi, 0, 0)),
                  pl.BlockSpec((tn, d), lambda bi, i, j: (j, 0)),
                  pl.BlockSpec((d, 2 * LANES), lambda bi, i, j: (0, 0)),
                  table, table],
        out_specs=[pl.BlockSpec((1, tm, tn), lambda bi, i, j: (bi, i, j)),
                   pl.BlockSpec((1, LANES, tm), lambda bi, i, j: (bi, 0, i))],
        out_shape=[jax.ShapeDtypeStruct((b, t, n), BF16),
                   jax.ShapeDtypeStruct((b, LANES, t), F32)],
        scratch_shapes=[pltpu.VMEM((tm, d), BF16)],
        compiler_params=_cparams(("arbitrary", "arbitrary", "arbitrary")),
        name="in_proj",
    )(x, shift, scale, w_main, wg, cos_t, sin_t)


def _attn_kernel(lam_ref, gn_ref, q_ref, kc_ref, vc_ref, kl_ref, vl_ref, o_ref,
                 q2_scr, vx_scr, m_scr, acc_scr, *, tq, tk, rq):
    tc = kc_ref.shape[1]
    t = kl_ref.shape[1]

    @pl.when(pl.program_id(2) == 0)
    def _():
        lane = lax.broadcasted_iota(jnp.int32, (tc, LANES), 1)
        ones_c = jnp.where(lane == 0, 1.0, 0.0).astype(BF16)
        vx_scr[0:tc, 0:LANES] = vc_ref[0]
        vx_scr[0:tc, LANES:2 * LANES] = ones_c
        for r0 in range(0, t, tc):
            vx_scr[tc + r0:tc + r0 + tc, 0:LANES] = vl_ref[0, r0:r0 + tc, :]
            vx_scr[tc + r0:tc + r0 + tc, LANES:2 * LANES] = ones_c

    q = q_ref[0]
    qf = q.astype(F32)
    lane = lax.broadcasted_iota(jnp.int32, qf.shape, 1)
    q2_scr[0:tq] = jnp.where(lane < DIFF_HALF, qf, 0.0).astype(BF16)
    q2_scr[tq:2 * tq] = jnp.where(lane >= DIFF_HALF, qf, 0.0).astype(BF16)
    m_scr[...] = jnp.full_like(m_scr, NEG_INF)
    acc_scr[...] = jnp.zeros_like(acc_scr)

    def step(k, vx):
        nkc = k.shape[0] // LANES
        for r0 in range(0, 2 * tq, rq):
            sc = _dot_nt(q2_scr[r0:r0 + rq], k)
            cols = [sc[:, c * LANES:(c + 1) * LANES] for c in range(nkc)]
            mx = cols[0]
            for c in range(1, nkc):
                mx = jnp.maximum(mx, cols[c])
            m_prev = m_scr[r0:r0 + rq]
            m_new = jnp.maximum(m_prev, jnp.max(mx, axis=1, keepdims=True))
            alpha = jnp.exp(m_prev - m_new)
            p = jnp.concatenate([jnp.exp(cb - m_new).astype(BF16) for cb in cols], axis=1)
            alpha2 = jnp.concatenate([alpha, alpha], axis=1)
            acc_scr[r0:r0 + rq] = alpha2 * acc_scr[r0:r0 + rq] + _dot(p, vx)
            m_scr[r0:r0 + rq] = m_new

    step(kc_ref[0], vx_scr[0:tc])

    def body(j, carry):
        r = pl.multiple_of(j * tk, tk)
        step(kl_ref[0, pl.ds(r, tk), :], vx_scr[pl.ds(tc + r, tk)])
        return carry

    lax.fori_loop(0, t // tk, body, 0)

    lm = lam_ref[...]
    d1 = jnp.sum(lm[0:1] * lm[1:2], axis=1, keepdims=True)
    d2 = jnp.sum(lm[2:3] * lm[3:4], axis=1, keepdims=True)
    lam = jnp.exp(d1) - jnp.exp(d2) + LAM_INIT
    a1 = acc_scr[0:tq]
    a2 = acc_scr[tq:2 * tq]
    o1 = a1[:, 0:LANES] / a1[:, LANES:LANES + 1]
    o2 = a2[:, 0:LANES] / a2[:, LANES:LANES + 1]
    o = o1 - lam * o2
    ms = jnp.mean(o * o, axis=1, keepdims=True)
    o = o * lax.rsqrt(ms + RMS_EPS) * gn_ref[...] * (1.0 - LAM_INIT)
    o_ref[0] = o.astype(BF16)


def _diff_attn(proj_l, proj_c, lam_pad, gn, *, tq, tk, rq):
    b, t, _ = proj_l.shape
    tc = proj_c.shape[1]
    h = DIFF_HEADS
    kv = lambda off: (lambda bi, hi, qi: (bi, 0, off + hi))
    return pl.pallas_call(
        functools.partial(_attn_kernel, tq=tq, tk=tk, rq=rq),
        grid=(b, h, t // tq),
        in_specs=[pl.BlockSpec((SUBLANES, LANES), lambda bi, hi, qi: (0, 0)),
                  pl.BlockSpec((1, LANES), lambda bi, hi, qi: (0, 0)),
                  pl.BlockSpec((1, tq, LANES), lambda bi, hi, qi: (bi, qi, hi)),
                  pl.BlockSpec((1, tc, LANES), kv(h)),
                  pl.BlockSpec((1, tc, LANES), kv(2 * h)),
                  pl.BlockSpec((1, t, LANES), kv(h)),
                  pl.BlockSpec((1, t, LANES), kv(2 * h))],
        out_specs=pl.BlockSpec((1, tq, LANES), lambda bi, hi, qi: (bi, qi, hi)),
        out_shape=jax.ShapeDtypeStruct((b, t, DIFF_W), BF16),
        scratch_shapes=[pltpu.VMEM((2 * tq, LANES), BF16),
                        pltpu.VMEM((tc + t, 2 * LANES), BF16),
                        pltpu.VMEM((2 * tq, LANES), F32),
                        pltpu.VMEM((2 * tq, 2 * LANES), F32)],
        compiler_params=_cparams(("arbitrary",) * 3),
        name="diff_attn",
    )(lam_pad, gn, proj_l, proj_c, proj_c, proj_l, proj_l)


def _gate_rows(a_ref, b_ref, ba_ref, bb_ref, rt_scr):
    ng = a_ref.shape[1]
    x = (a_ref[0] + ba_ref[0]).reshape(ng * SUBLANES, LANES)
    ls = jax.nn.log_sigmoid(b_ref[0] + bb_ref[0]).reshape(ng * SUBLANES, LANES)
    ji = lax.broadcasted_iota(jnp.int32, (LANES, LANES), 0)
    si = lax.broadcasted_iota(jnp.int32, (LANES, LANES), 1)
    same = (ji // CHUNK) == (si // CHUNK)
    ones_where = lambda cond: jnp.where(cond, 1.0, 0.0).astype(BF16)
    m_pre = ones_where(same & (ji <= si))
    m_suf = ones_where(same & (ji >= si))
    m_tot = ones_where(same)
    hi, mid, lo = _split3(ls)
    mm = lambda m: _dot(hi, m) + _dot(mid, m) + _dot(lo, m)
    sub = lax.broadcasted_iota(jnp.int32, x.shape, 0) % SUBLANES
    cum = jnp.where(sub % 2 == 0, mm(m_pre), mm(m_suf))
    rt = jnp.where(sub < 2, x - cum, jnp.where(sub < 4, cum, mm(m_tot)))
    rt_scr[...] = rt.reshape(ng, SUBLANES, LANES)


def _conv_silu(raw_ref, w, bias, pad_scr, t, emit):
    pad_scr[0:SUBLANES] = jnp.zeros((SUBLANES, LANES), F32)
    pad_scr[SUBLANES + t:2 * SUBLANES + t] = jnp.zeros((SUBLANES, LANES), F32)
    pad_scr[SUBLANES:SUBLANES + t] = raw_ref[0].astype(F32)
    half = MLSTM_CONV // 2
    for g in range(t // LANES):
        acc = jnp.zeros((LANES, LANES), F32) + bias
        for j in range(MLSTM_CONV):
            off = SUBLANES + g * LANES + j - half
            acc = acc + pad_scr[off:off + LANES] * w[j:j + 1]
        emit(g, acc * jax.nn.sigmoid(acc))


def _mlstm_prologue(q_ref, k_ref, v_ref, cwq_ref, cwk_ref, cbq_ref, cbk_ref,
                    pad_scr, q_scr, kt_scr, vx_scr, t, g0):
    def emit_q(g, blk):
        q_scr[g0 + g] = blk.astype(BF16)

    def emit_k(g, blk):
        kt_scr[g0 + g] = (blk * (MLSTM_DHEAD ** -0.5)).T

    _conv_silu(q_ref, cwq_ref[0], cbq_ref[0], pad_scr, t, emit_q)
    _conv_silu(k_ref, cwk_ref[0], cbk_ref[0], pad_scr, t, emit_k)
    lane = lax.broadcasted_iota(jnp.int32, (LANES, LANES), 1)
    ones_col = jnp.where(lane == 0, 1.0, 0.0).astype(BF16)
    for g in range(t // LANES):
        vx_scr[g0 + g, :, 0:LANES] = v_ref[0, g * LANES:(g + 1) * LANES, :]
        vx_scr[g0 + g, :, LANES:2 * LANES] = ones_col


def _groups(q_scr, kt_scr, vx_scr, c_scr, m_st, items, bwd, out_scr):
    d = 1 if bwd else 0
    order = (1, 0) if bwd else (0, 1)
    lane = lax.broadcasted_iota(jnp.int32, (1, LANES), 1)
    in_c = (lane < CHUNK, lane >= CHUNK)
    li = lax.broadcasted_iota(jnp.int32, (LANES, LANES), 0)
    si = lax.broadcasted_iota(jnp.int32, (LANES, LANES), 1)
    same = (li < CHUNK) == (si < CHUNK)
    mask = same & ((si >= li) if bwd else (si <= li))
    row0 = lax.broadcasted_iota(jnp.int32, (LANES, 1), 0) < CHUNK

    st = []
    for g, rt, og in items:
        r = rt[d:d + 1, :]
        cum = rt[2 + d:3 + d, :]
        totrow = rt[4 + d:5 + d, :]
        maxr = [jnp.max(jnp.where(in_c[c], r, NEG_INF), axis=1, keepdims=True) for c in (0, 1)]
        tot = [jnp.max(jnp.where(in_c[c], totrow, NEG_INF), axis=1, keepdims=True) for c in (0, 1)]
        m_b, mu_f = [None, None], [None, None]
        for c in order:
            m_b[c] = m_st
            mu_f[c] = jnp.maximum(m_st, maxr[c])
            m_st = tot[c] + mu_f[c]
        st.append(dict(g=g, og=og, r=r, cum=cum, m_b=m_b, mu_f=mu_f))
    for e in st:
        kt = kt_scr[e["g"]]
        vx = vx_scr[e["g"]]
        e["kt"], e["vx"] = kt, vx
        e["kv"] = [_dot((kt * jnp.where(in_c[c], jnp.exp(e["r"] - e["mu_f"][c]), 0.0)).astype(BF16), vx)
                   for c in (0, 1)]
        if out_scr is not None:
            e["q"] = q_scr[e["g"]]
            e["sc"] = _dot(e["q"], kt.astype(BF16))
    c_st = c_scr[...]
    for e in st:
        e["c_bf"] = [None, None]
        for c in order:
            e["c_bf"][c] = c_st.astype(BF16)
            c_st = jnp.exp(e["m_b"][c] - e["mu_f"][c]) * c_st + e["kv"][c]
    c_scr[...] = c_st
    if out_scr is not None:
        for e in st:
            m_col = jnp.where(row0, e["m_b"][0], e["m_b"][1])
            rm = jnp.where(mask, e["r"], NEG_INF)
            mu = jnp.maximum(m_col, jnp.max(rm, axis=1, keepdims=True))
            e["mu"], e["m_col"] = mu, m_col
            e["s"] = (e["sc"] * jnp.exp(rm - mu)).astype(BF16)
        for e in st:
            e["intra"] = _dot(e["s"], e["vx"])
            e["inter"] = [_dot(e["q"][c * CHUNK:(c + 1) * CHUNK], e["c_bf"][c]) for c in (0, 1)]
        for e in st:
            inter = jnp.concatenate(e["inter"], axis=0)
            tot_o = e["intra"] + jnp.exp(e["m_col"] - e["mu"]) * inter
            num = tot_o[:, 0:LANES]
            den = tot_o[:, LANES:LANES + 1]
            cum_col = jnp.sum(jnp.where(li == si, e["cum"], 0.0), axis=1, keepdims=True)
            floor = jnp.exp(-(cum_col + e["mu"]))
            out_scr[e["og"]] = num / jnp.maximum(jnp.abs(den), floor)
    return m_st


def _mlstm_kernel(ql_ref, kl_ref, vl_ref, ol_ref, qc_ref, kc_ref, vc_ref,
                  al_ref, bl_ref, ac_ref, bc_ref, ba_ref, bb_ref,
                  cwq_ref, cwk_ref, cbq_ref, cbk_ref, gn_ref, o_ref,
                  pad_scr, q_scr, kt_scr, vx_scr, rtc_scr, rtl_scr, cf_scr, cb_scr, hf_scr, hb_scr,
                  *, t, tc, gpi):
    _gate_rows(ac_ref.at[0], bc_ref.at[0], ba_ref, bb_ref, rtc_scr)
    _gate_rows(al_ref.at[0], bl_ref.at[0], ba_ref, bb_ref, rtl_scr)
    _mlstm_prologue(qc_ref, kc_ref, vc_ref, cwq_ref, cwk_ref, cbq_ref, cbk_ref,
                    pad_scr, q_scr, kt_scr, vx_scr, tc, 0)
    _mlstm_prologue(ql_ref, kl_ref, vl_ref, cwq_ref, cwk_ref, cbq_ref, cbk_ref,
                    pad_scr, q_scr, kt_scr, vx_scr, t, tc // LANES)
    cf_scr[...] = jnp.zeros_like(cf_scr)
    cb_scr[...] = jnp.zeros_like(cb_scr)
    m0 = jnp.zeros((1, 1), F32)

    def group(rt_scr, ng, g0, outs, gpi):
        def body(it, carry):
            mf, mb = carry
            items_f, items_b = [], []
            for u in range(gpi):
                gf = it * gpi + u
                gb = ng - 1 - gf
                items_f.append((g0 + gf, rt_scr[gf], gf))
                items_b.append((g0 + gb, rt_scr[gb], gb))
            args = (q_scr, kt_scr, vx_scr)
            mf = _groups(*args, cf_scr, mf, items_f, False, outs and hf_scr)
            mb = _groups(*args, cb_scr, mb, items_b, True, outs and hb_scr)
            return mf, mb
        return body

    ngc = tc // LANES
    ngl = t // LANES
    gc = math.gcd(ngc, gpi)
    gl = math.gcd(ngl, gpi)
    carry = lax.fori_loop(0, ngc // gc, group(rtc_scr, ngc, 0, None, gc), (m0, m0))
    lax.fori_loop(0, ngl // gl, group(rtl_scr, ngl, ngc, True, gl), carry)

    hm = (hf_scr[...] + hb_scr[...]).reshape(t, LANES)
    ms = jnp.mean(hm * hm, axis=1, keepdims=True)
    out = hm * lax.rsqrt(ms + RMS_EPS) * gn_ref[...] * jax.nn.sigmoid(ol_ref[0].astype(F32))
    o_ref[0] = out.astype(BF16)


def _mlstm(proj_l, proj_c, ga_l, gb_l, ga_c, gb_c, bias_a, bias_b, conv_w, conv_b, gn):
    b, t, _ = proj_l.shape
    tc = proj_c.shape[1]
    h = MLSTM_HEADS
    col = lambda off: (lambda bi, hi: (bi, 0, off + hi))
    q0 = (3 * DIFF_W) // LANES
    gspec = lambda ng: pl.BlockSpec((1, 1, ng, SUBLANES, LANES), lambda bi, hi: (bi, hi, 0, 0, 0))
    hspec = pl.BlockSpec((1, SUBLANES, LANES), lambda bi, hi: (hi, 0, 0))
    cw = jnp.pad(conv_w, ((0, SUBLANES - MLSTM_CONV), (0, 0)))
    cwspec = lambda off: pl.BlockSpec((1, SUBLANES, LANES), lambda bi, hi: (0, 0, off + hi))
    cbspec = lambda off: pl.BlockSpec((1, 1, LANES), lambda bi, hi: (0, 0, off + hi))
    tt = t + tc
    return pl.pallas_call(
        functools.partial(_mlstm_kernel, t=t, tc=tc, gpi=MLSTM_GROUPS_PER_STEP),
        grid=(b, h),
        in_specs=[pl.BlockSpec((1, t, LANES), col(q0)),
                  pl.BlockSpec((1, t, LANES), col(q0 + h)),
                  pl.BlockSpec((1, t, LANES), col(q0 + 2 * h)),
                  pl.BlockSpec((1, t, LANES), col(q0 + 3 * h)),
                  pl.BlockSpec((1, tc, LANES), col(q0)),
                  pl.BlockSpec((1, tc, LANES), col(q0 + h)),
                  pl.BlockSpec((1, tc, LANES), col(q0 + 2 * h)),
                  gspec(t // LANES), gspec(t // LANES), gspec(tc // LANES), gspec(tc // LANES),
                  hspec, hspec,
                  cwspec(0), cwspec(h), cbspec(0), cbspec(h),
                  pl.BlockSpec((1, LANES), lambda bi, hi: (0, 0))],
        out_specs=pl.BlockSpec((1, t, LANES), lambda bi, hi: (bi, 0, hi)),
        out_shape=jax.ShapeDtypeStruct((b, t, ML_W), BF16),
        scratch_shapes=[pltpu.VMEM((t + 2 * SUBLANES, LANES), F32),
                        pltpu.VMEM((tt // LANES, LANES, LANES), BF16),
                        pltpu.VMEM((tt // LANES, LANES, LANES), F32),
                        pltpu.VMEM((tt // LANES, LANES, 2 * LANES), BF16),
                        pltpu.VMEM((tc // LANES, SUBLANES, LANES), F32),
                        pltpu.VMEM((t // LANES, SUBLANES, LANES), F32),
                        pltpu.VMEM((LANES, 2 * LANES), F32),
                        pltpu.VMEM((LANES, 2 * LANES), F32),
                        pltpu.VMEM((t // LANES, LANES, LANES), F32),
                        pltpu.VMEM((t // LANES, LANES, LANES), F32)],
        compiler_params=_cparams(("arbitrary", "arbitrary")),
        name="mlstm",
    )(proj_l, proj_l, proj_l, proj_l, proj_c, proj_c, proj_c,
      ga_l, gb_l, ga_c, gb_c, bias_a, bias_b,
      cw.reshape(1, SUBLANES, 2 * ML_W), cw.reshape(1, SUBLANES, 2 * ML_W),
      conv_b.reshape(1, 1, 2 * ML_W), conv_b.reshape(1, 1, 2 * ML_W), gn)


def _layer_norm(v, g, b):
    mu = jnp.mean(v, axis=1, keepdims=True)
    c = v - mu
    var = jnp.mean(c * c, axis=1, keepdims=True)
    return c * lax.rsqrt(var + LN_EPS) * g + b


def _outproj_kernel(d_ref, m_ref, w_ref, x_ref, g1_ref, lg_ref, lb_ref, sh_ref, sc_ref,
                    x1_ref, h2_ref):
    tm = x_ref.shape[1]
    rows = min(tm, MXU_WIDTH)
    for r0 in range(0, tm, rows):
        rs = slice(r0, r0 + rows)
        y = _dot(d_ref[0, rs, :], w_ref[0]) + _dot(m_ref[0, rs, :], w_ref[1])
        v = DEEPNORM_ALPHA * x_ref[0, rs, :] + g1_ref[0] * y
        x1 = _layer_norm(v, lg_ref[...], lb_ref[...])
        x1_ref[0, rs, :] = x1
        h2_ref[0, rs, :] = (x1 * (1.0 + sc_ref[0]) + sh_ref[0]).astype(BF16)


def _out_proj(d_lat, m_lat, w_out2, x, g1, ln_g, ln_b, sh2, sc2, *, tm):
    b, t, d = x.shape
    hw = d_lat.shape[2]
    row = pl.BlockSpec((1, 1, d), lambda bi, i: (bi, 0, 0))
    vec = pl.BlockSpec((1, d), lambda bi, i: (0, 0))
    return pl.pallas_call(
        _outproj_kernel,
        grid=(b, t // tm),
        in_specs=[pl.BlockSpec((1, tm, hw), lambda bi, i: (bi, i, 0)),
                  pl.BlockSpec((1, tm, hw), lambda bi, i: (bi, i, 0)),
                  pl.BlockSpec((2, hw, d), lambda bi, i: (0, 0, 0)),
                  pl.BlockSpec((1, tm, d), lambda bi, i: (bi, i, 0)),
                  row, vec, vec, row, row],
        out_specs=[pl.BlockSpec((1, tm, d), lambda bi, i: (bi, i, 0)),
                   pl.BlockSpec((1, tm, d), lambda bi, i: (bi, i, 0))],
        out_shape=[jax.ShapeDtypeStruct((b, t, d), F32),
                   jax.ShapeDtypeStruct((b, t, d), BF16)],
        compiler_params=_cparams(("arbitrary", "arbitrary")),
        name="out_proj",
    )(d_lat, m_lat, w_out2, x, g1, ln_g, ln_b, sh2, sc2)


def _top_rows(x, k, dst_scr):
    cur = x
    for r in range(k):
        m = jnp.max(cur, axis=0, keepdims=True)
        dst_scr[r:r + 1] = m
        if r + 1 < k:
            cur = jnp.where(cur == m, NEG_INF, cur)


def _sort16_pairs():
    def merge(lo, hi, r):
        step = r * 2
        if step < hi - lo:
            yield from merge(lo, hi, step)
            yield from merge(lo + r, hi, step)
            yield from [(i, i + r) for i in range(lo + r, hi - r, step)]
        else:
            yield (lo, lo + r)

    def sort(lo, hi):
        if hi - lo >= 1:
            mid = lo + (hi - lo) // 2
            yield from sort(lo, mid)
            yield from sort(mid + 1, hi)
            yield from merge(lo, hi, 1)

    return list(sort(0, 15))


def _top16_of_128(x, dst_scr):
    k = PEER_TOPK
    y = [x[r * SUBLANES:(r + 1) * SUBLANES] for r in range(k)]
    for a, b in _sort16_pairs():
        y[a], y[b] = jnp.maximum(y[a], y[b]), jnp.minimum(y[a], y[b])
    for r in range(k):
        m = jnp.max(y[0], axis=0, keepdims=True)
        dst_scr[r:r + 1] = m
        if r + 1 < k:
            hit = y[0] == m
            for i in range(k - 1 - r):
                y[i] = jnp.where(hit, y[i + 1], y[i])


def _rank_among(x, v):
    row = lambda i: v[i:i + 1]
    pick = jnp.where
    c1 = x < row(7)
    c2 = x < pick(c1, row(11), row(3))
    c3 = x < pick(c1, pick(c2, row(13), row(9)), pick(c2, row(5), row(1)))
    c4 = x < pick(c1, pick(c2, pick(c3, row(14), row(12)), pick(c3, row(10), row(8))),
                  pick(c2, pick(c3, row(6), row(4)), pick(c3, row(2), row(0))))
    c5 = x < row(15)
    bit = lambda c, w: jnp.where(c, float(w), 0.0)
    return bit(c1, 8) + bit(c2, 4) + bit(c3, 2) + bit(c4, 1) + bit(c5, 1)


def _peer_prep_kernel(h_ref, wq_ref, k1_ref, k2_ref, n1_ref, r2_ref, e1_ref, e2_ref,
                      v1_scr, v2_scr, tp_scr):
    q = _dot(h_ref[...], wq_ref[...])
    half = PEER_DQ // 2
    for h in range(PEER_HEADS):
        q1 = q[:, h * PEER_DQ:h * PEER_DQ + half].astype(BF16)
        q2 = q[:, h * PEER_DQ + half:(h + 1) * PEER_DQ].astype(BF16)
        s1 = _dot_nt(k1_ref[h], q1)
        s2 = _dot_nt(k2_ref[h], q2)
        _top16_of_128(s1, v1_scr)
        _top16_of_128(s2, v2_scr)
        v1 = v1_scr[...]
        v2 = v2_scr[...]
        pieces = [v1[0:1] + v2]
        pieces += [v1[a:a + 1] + v2[0:SUBLANES] for a in range(1, SUBLANES)]
        pieces += [v1[SUBLANES:PEER_TOPK] + v2[0:1]]
        cand = jnp.concatenate(pieces, axis=0)
        _top_rows(cand, PEER_TOPK, tp_scr)
        tp = tp_scr[...]
        top0 = tp[0:1]
        tau = tp[PEER_TOPK - 1:PEER_TOPK]
        z = jnp.sum(jnp.exp(tp - top0), axis=0, keepdims=True)
        cnt = jnp.zeros(s1.shape, F32)
        for bb in range(PEER_TOPK):
            n_a = jnp.sum(jnp.where(v1[bb:bb + 1] + v2 >= tau, 1.0, 0.0), axis=0, keepdims=True)
            cnt = jnp.where(s1 == v1[bb:bb + 1], n_a, cnt)
        n1_ref[h] = cnt
        r2_ref[h] = _rank_among(s2, v2).astype(BF16)
        e1_ref[h] = jnp.exp(s1 - v1[0:1]) / z
        e2_ref[h] = jnp.exp(s2 - v2[0:1]).astype(BF16)


def _peer_prep(h2, wq, k1, k2, *, tt):
    n, d = h2.shape
    hp = PEER_HEADS
    big = jax.ShapeDtypeStruct((hp, PEER_NKEYS, n), F32)
    big16 = jax.ShapeDtypeStruct((hp, PEER_NKEYS, n), BF16)
    bspec = pl.BlockSpec((hp, PEER_NKEYS, tt), lambda i: (0, 0, i))
    kspec = pl.BlockSpec((hp, PEER_NKEYS, PEER_DQ // 2), lambda i: (0, 0, 0))
    return pl.pallas_call(
        _peer_prep_kernel,
        grid=(n // tt,),
        in_specs=[pl.BlockSpec((tt, d), lambda i: (i, 0)),
                  pl.BlockSpec((d, hp * PEER_DQ), lambda i: (0, 0)),
                  kspec, kspec],
        out_specs=[bspec, bspec, bspec, bspec],
        out_shape=[big, big16, big, big16],
        scratch_shapes=[pltpu.VMEM((PEER_TOPK, tt), F32),
                        pltpu.VMEM((PEER_TOPK, tt), F32),
                        pltpu.VMEM((PEER_TOPK, tt), F32)],
        compiler_params=_cparams(("arbitrary",)),
        name="peer_prep",
    )(h2, wq, k1, k2)


def _transpose_cast_kernel(v_ref, o_ref):
    o_ref[...] = v_ref[...].T.astype(BF16)


def _transpose_cast(v, *, te):
    ne, d = v.shape
    return pl.pallas_call(
        _transpose_cast_kernel,
        grid=(ne // te,),
        in_specs=[pl.BlockSpec((te, d), lambda j: (j, 0))],
        out_specs=pl.BlockSpec((d, te), lambda j: (0, j)),
        out_shape=jax.ShapeDtypeStruct((d, ne), BF16),
        compiler_params=_cparams(("arbitrary",)),
        name="peer_v_layout",
    )(v)


def _peer_dense_kernel(h_ref, u_ref, vt_ref, n1_ref, r2_ref, e1_ref, e2_ref, x1_ref, g2_ref, lg_ref, lb_ref,
                       o_ref, ht_scr, z_scr, acc_scr, *, te):
    j = pl.program_id(1)
    nj = pl.num_programs(1)
    nk = PEER_NKEYS

    @pl.when(j == 0)
    def _():
        ht_scr[...] = h_ref[...].astype(F32).T.astype(BF16)
        acc_scr[...] = jnp.zeros_like(acc_scr)
        z_scr[1] = jnp.zeros(z_scr.shape[1:], F32)

    prev = jnp.maximum(j - 1, 0)

    tt = ht_scr.shape[1]
    tn = TILES["peer_token_slice"]

    def body(wslot, rslot):
        for n0 in range(0, tt, tn):
            z_scr[wslot, :, n0:n0 + tn] = _dot(u_ref[...], ht_scr[:, n0:n0 + tn])
            ws = []
            for a in range(te // nk):
                i1 = prev * (te // nk) + a
                g = None
                for h in range(PEER_HEADS):
                    n1row = n1_ref[h, pl.ds(i1, 1), n0:n0 + tn].astype(BF16)
                    e1row = e1_ref[h, pl.ds(i1, 1), n0:n0 + tn].astype(BF16)
                    hit = r2_ref[h, :, n0:n0 + tn] < n1row
                    term = jnp.where(hit, e2_ref[h, :, n0:n0 + tn], jnp.zeros((), BF16)) * e1row
                    g = term if g is None else g + term
                z = z_scr[rslot, a * nk:(a + 1) * nk, n0:n0 + tn]
                gelu = 0.5 * z * (1.0 + lax.erf(z * math.sqrt(0.5)))
                ws.append(gelu.astype(BF16) * g)
            acc_scr[:, n0:n0 + tn] += _dot(vt_ref[...], jnp.concatenate(ws, axis=0))

    @pl.when(j % 2 == 0)
    def _():
        body(0, 1)

    @pl.when(j % 2 == 1)
    def _():
        body(1, 0)

    @pl.when(j == nj - 1)
    def _():
        v = DEEPNORM_ALPHA * x1_ref[...] + g2_ref[0] * acc_scr[...].T
        o_ref[...] = _layer_norm(v, lg_ref[...], lb_ref[...])


def _peer_dense(h2, u_bf, vt_bf, n1, r2, e1, e2, x1, g2, ln_g, ln_b, *, tt, te):
    n, d = h2.shape
    ne = u_bf.shape[0]
    hp = PEER_HEADS
    nb = ne // te
    blocks_per_sample = n // g2.shape[0] // tt
    bspec = pl.BlockSpec((hp, PEER_NKEYS, tt), lambda i, j: (0, 0, i))
    vec = pl.BlockSpec((1, d), lambda i, j: (0, 0))
    return pl.pallas_call(
        functools.partial(_peer_dense_kernel, te=te),
        grid=(n // tt, nb + 1),
        in_specs=[pl.BlockSpec((tt, d), lambda i, j: (i, 0)),
                  pl.BlockSpec((te, d), lambda i, j: (jnp.minimum(j, nb - 1), 0)),
                  pl.BlockSpec((d, te), lambda i, j: (0, jnp.maximum(j - 1, 0))),
                  bspec, bspec, bspec, bspec,
                  pl.BlockSpec((tt, d), lambda i, j: (i, 0)),
                  pl.BlockSpec((1, 1, d), lambda i, j: (i // blocks_per_sample, 0, 0)),
                  vec, vec],
        out_specs=pl.BlockSpec((tt, d), lambda i, j: (i, 0)),
        out_shape=jax.ShapeDtypeStruct((n, d), F32),
        scratch_shapes=[pltpu.VMEM((d, tt), BF16),
                        pltpu.VMEM((2, te, tt), F32),
                        pltpu.VMEM((d, tt), F32)],
        compiler_params=_cparams(("arbitrary", "arbitrary")),
        name="peer_dense",
    )(h2, u_bf, vt_bf, n1, r2, e1, e2, x1, g2, ln_g, ln_b)


def _rope_tables(t):
    pos = jnp.arange(t, dtype=jnp.int32)
    row = (pos // GRID_W).astype(F32)
    col = (pos % GRID_W).astype(F32)
    inv = ROPE_BASE ** (-jnp.arange(ROPE_NF, dtype=F32) / ROPE_NF)
    lane = jnp.arange(LANES)
    use_col = ((lane // (2 * ROPE_NF)) % 2) == 1
    p = jnp.where(use_col[None, :], col[:, None], row[:, None])
    ang = p * inv[lane % ROPE_NF][None, :]
    sign = jnp.where((lane % (2 * ROPE_NF)) < ROPE_NF, -1.0, 1.0).astype(F32)
    return jnp.cos(ang), jnp.sin(ang) * sign[None, :]


def _gate_layout(g):
    b, _, t = g.shape
    gt = g[:, :N_GATES * MLSTM_HEADS, :].reshape(b, N_GATES, MLSTM_HEADS, t // LANES, LANES)
    gt = gt.transpose(0, 2, 3, 1, 4)
    zeros = jnp.zeros_like(gt[:, :, :, 0:1])
    a = jnp.concatenate([gt[:, :, :, 0:1], gt[:, :, :, 2:3]] + [zeros] * 6, axis=3)
    fb = jnp.concatenate([gt[:, :, :, 1:2], gt[:, :, :, 3:4]], axis=3)
    return a, jnp.tile(fb, (1, 1, 1, 4, 1))


def _gate_bias(gate_b):
    gb = gate_b.astype(F32)
    z = jnp.zeros_like(gb[0])
    a = jnp.stack([gb[0], gb[2]] + [z] * 6, axis=1)
    bm = jnp.stack([gb[1], gb[3]] * 4, axis=1)
    bc = lambda v: jnp.broadcast_to(v[:, :, None], (MLSTM_HEADS, SUBLANES, LANES))
    return bc(a), bc(bm)


def kernel(x, c, ctx, c_ctx, w_ada, b_ada, w_in, conv_w, conv_b, gate_b, diff_lambda, diff_norm_g,
           mlstm_norm_g, w_out, ln1_g, ln1_b, ln2_g, ln2_b, peer_wq, peer_keys, peer_u, peer_v):
    b, t, d = x.shape
    tc = ctx.shape[1]
    l = 0

    c_rows = jnp.concatenate([c, c_ctx[None, :]], axis=0)
    mod = _adaln(c_rows, w_ada[l], b_ada[l])
    sh1, sc1, g1, sh2, sc2, g2 = [m[:, None, :] for m in jnp.split(mod[:b], 6, axis=1)]
    csh1, csc1 = [m[:, None, :] for m in jnp.split(mod[b:b + 1], 6, axis=1)[:2]]

    w = w_in[l]
    col_scale = jnp.where(jnp.arange(MAIN_COLS) < DIFF_W, DIFF_HALF ** -0.5, 1.0).astype(F32)
    w_main = (w.T[:MAIN_COLS] * col_scale[:, None]).astype(BF16)
    wg = jnp.pad(w[:, MAIN_COLS:], ((0, 0), (0, LANES - N_GATES * MLSTM_HEADS)))
    wg_hi = wg.astype(BF16)
    wg_lo = (wg - wg_hi.astype(F32)).astype(BF16)
    wg2 = jnp.concatenate([wg_hi, wg_lo], axis=1)
    cos_t, sin_t = _rope_tables(t)
    proj_l, gates_l = _in_proj(x, sh1, sc1, w_main, wg2, cos_t, sin_t, tm=min(TILES["in_proj_rows"], t))
    nc = b * tc
    proj_c, gates_c = _in_proj(ctx.reshape(1, nc, d), csh1, csc1, w_main, wg2,
                               jnp.ones((nc, LANES), F32), jnp.zeros((nc, LANES), F32), tm=nc)
    proj_c = proj_c.reshape(b, tc, MAIN_COLS)
    gates_c = gates_c.reshape(LANES, b, tc).transpose(1, 0, 2)

    lam_pad = jnp.pad(diff_lambda[l].astype(F32), ((0, SUBLANES - 4), (0, LANES - DIFF_HALF)))
    d_lat = _diff_attn(proj_l, proj_c, lam_pad, diff_norm_g[l].reshape(1, LANES),
                       tq=min(TILES["attn_q_rows"], t), tk=min(TILES["attn_kv_rows"], t),
                       rq=TILES["attn_row_group"])

    ga_l, gb_l = _gate_layout(gates_l)
    ga_c, gb_c = _gate_layout(gates_c)
    bias_a, bias_b = _gate_bias(gate_b[l])
    m_lat = _mlstm(proj_l, proj_c, ga_l, gb_l, ga_c, gb_c, bias_a, bias_b,
                   conv_w[l], conv_b[l], mlstm_norm_g[l].reshape(1, LANES))

    w_out2 = w_out[l].astype(BF16).reshape(2, DIFF_W, d)
    x1, h2 = _out_proj(d_lat, m_lat, w_out2, x, g1, ln1_g[l].reshape(1, d), ln1_b[l].reshape(1, d),
                       sh2, sc2, tm=min(TILES["out_proj_rows"], t))

    n = b * t
    h2f = h2.reshape(n, d)
    keys = peer_keys[l].astype(BF16)
    tt = min(TILES["peer_tokens"], t)
    n1, r2, e1, e2 = _peer_prep(h2f, peer_wq[l].astype(BF16), keys[0], keys[1], tt=tt)
    vt_bf = _transpose_cast(peer_v[l], te=TILES["peer_experts"])
    out = _peer_dense(h2f, peer_u[l].astype(BF16), vt_bf, n1, r2, e1, e2,
                      x1.reshape(n, d), g2, ln2_g[l].reshape(1, d), ln2_b[l].reshape(1, d),
                      tt=tt, te=TILES["peer_experts"])
    return out.reshape(b, t, d)
```

```python
import functools
import math

import jax
import jax.numpy as jnp
from jax import lax
from jax.experimental import pallas as pl
from jax.experimental.pallas import tpu as pltpu

F32 = jnp.float32
BF16 = jnp.bfloat16

LANES = 128
SUBLANES = 8
MXU_WIDTH = 256
VMEM_LIMIT = 56 * 1024 * 1024

TILES = dict(
    adaln_cols=512,
    in_proj_rows=1024, in_proj_cols=512,
    attn_q_rows=4096, attn_kv_rows=2048, attn_row_group=MXU_WIDTH,
    out_proj_rows=512,
    peer_tokens=512, peer_experts=512, peer_token_slice=MXU_WIDTH,
)

GRID_W = 64
DIFF_HEADS = 8
DIFF_HALF = 64
DIFF_W = DIFF_HEADS * 2 * DIFF_HALF
MLSTM_HEADS = 8
MLSTM_DHEAD = 128
ML_W = MLSTM_HEADS * MLSTM_DHEAD
MLSTM_CONV = 5
CHUNK = 64
MLSTM_GROUPS_PER_STEP = 4
N_GATES = 4
MAIN_COLS = 3 * DIFF_W + 4 * ML_W
ROPE_COLS = 2 * DIFF_W
ROPE_BASE = 10000.0
ROPE_NF = 16
PEER_HEADS = 8
PEER_NKEYS = 128
PEER_DQ = 256
PEER_TOPK = 16
DEPTH = 1
DEEPNORM_ALPHA = (2.0 * DEPTH) ** 0.25
LN_EPS = 1e-5
RMS_EPS = 1e-6
LAM_INIT = 0.8 - 0.6 * math.exp(-0.3 * 0)
NEG_INF = float("-inf")


def _cparams(sem):
    return pltpu.CompilerParams(dimension_semantics=sem, vmem_limit_bytes=VMEM_LIMIT)


def _dot(a, b):
    return jnp.dot(a, b, preferred_element_type=F32)


def _dot_nt(a, b):
    return lax.dot_general(a, b, (((1,), (1,)), ((), ())), preferred_element_type=F32)


def _split3(x):
    hi = x.astype(BF16)
    r1 = x - hi.astype(F32)
    mid = r1.astype(BF16)
    lo = (r1 - mid.astype(F32)).astype(BF16)
    return hi, mid, lo


def _adaln_kernel(cb_ref, w_ref, b_ref, o_ref, act_scr, *, n_rows):
    @pl.when(pl.program_id(0) == 0)
    def _():
        c = cb_ref[...]
        act_scr[...] = c * jax.nn.sigmoid(c)

    tn = w_ref.shape[1]
    o_ref[...] = jnp.zeros_like(o_ref)
    for cb in range(tn // LANES):
        w = w_ref[:, cb * LANES:(cb + 1) * LANES]
        for r in range(n_rows):
            s = jnp.sum(act_scr[r] * w, axis=0, keepdims=True)
            o_ref[r:r + 1, cb * LANES:(cb + 1) * LANES] = s + b_ref[:, cb * LANES:(cb + 1) * LANES]


def _adaln(c_rows, w_ada, b_ada):
    n_rows, k = c_rows.shape
    n = w_ada.shape[1]
    tn = TILES["adaln_cols"]
    cb = jnp.broadcast_to(c_rows[:, :, None], (n_rows, k, LANES))
    return pl.pallas_call(
        functools.partial(_adaln_kernel, n_rows=n_rows),
        grid=(n // tn,),
        in_specs=[pl.BlockSpec((n_rows, k, LANES), lambda j: (0, 0, 0)),
                  pl.BlockSpec((k, tn), lambda j: (0, j)),
                  pl.BlockSpec((1, tn), lambda j: (0, j))],
        out_specs=pl.BlockSpec((SUBLANES, tn), lambda j: (0, j)),
        out_shape=jax.ShapeDtypeStruct((SUBLANES, n), F32),
        scratch_shapes=[pltpu.VMEM((n_rows, k, LANES), F32)],
        compiler_params=_cparams(("arbitrary",)),
        name="adaln",
    )(cb, w_ada, b_ada.reshape(1, n))


def _inproj_kernel(x_ref, sh_ref, sc_ref, w_ref, wg_ref, cos_ref, sin_ref, o_ref, g_ref, h_scr,
                   *, n_rope_tiles):
    j = pl.program_id(2)

    @pl.when(j == 0)
    def _():
        h = x_ref[0] * (1.0 + sc_ref[0]) + sh_ref[0]
        hb = h.astype(BF16)
        h_scr[...] = hb
        hl = (h - hb.astype(F32)).astype(BF16)
        a = _dot(hb, wg_ref[...])
        g = a[:, 0:LANES] + a[:, LANES:2 * LANES] + _dot(hl, wg_ref[:, 0:LANES])
        g_ref[0] = g.T

    acc = _dot(h_scr[...], w_ref[...])
    tn = acc.shape[1]

    @pl.when(j < n_rope_tiles)
    def _():
        cos = cos_ref[...]
        sin = sin_ref[...]
        lane = lax.broadcasted_iota(jnp.int32, cos.shape, 1)
        first = (lane % (2 * ROPE_NF)) < ROPE_NF
        for cb in range(tn // LANES):
            a = acc[:, cb * LANES:(cb + 1) * LANES]
            sw = jnp.where(first, pltpu.roll(a, LANES - ROPE_NF, 1), pltpu.roll(a, ROPE_NF, 1))
            o_ref[0, :, cb * LANES:(cb + 1) * LANES] = (a * cos + sw * sin).astype(BF16)

    @pl.when(j >= n_rope_tiles)
    def _():
        o_ref[0] = acc.astype(BF16)


def _in_proj(x, shift, scale, w_main, wg, cos_t, sin_t, *, tm):
    b, t, d = x.shape
    n = w_main.shape[1]
    tn = TILES["in_proj_cols"]
    table = pl.BlockSpec((tm, LANES), lambda bi, i, j: (i, 0))
    return pl.pallas_call(
        functools.partial(_inproj_kernel, n_rope_tiles=ROPE_COLS // tn),
        grid=(b, t // tm, n // tn),
        in_specs=[pl.BlockSpec((1, tm, d), lambda bi, i, j: (bi, i, 0)),
                  pl.BlockSpec((1, 1, d), lambda bi, i, j: (bi, 0, 0)),
                  pl.BlockSpec((1, 1, d), lambda bi, i, j: (bi, 0, 0)),
                  pl.BlockSpec((d, tn), lambda bi, i, j: (0, j)),
                  pl.BlockSpec((d, 2 * LANES), lambda bi, i, j: (0, 0)),
                  table, table],
        out_specs=[pl.BlockSpec((1, tm, tn), lambda bi, i, j: (bi, i, j)),
                   pl.BlockSpec((1, LANES, tm), lambda bi, i, j: (bi, 0, i))],
        out_shape=[jax.ShapeDtypeStruct((b, t, n), BF16),
                   jax.ShapeDtypeStruct((b, LANES, t), F32)],
        scratch_shapes=[pltpu.VMEM((tm, d), BF16)],
        compiler_params=_cparams(("arbitrary", "arbitrary", "arbitrary")),
        name="in_proj",
    )(x, shift, scale, w_main, wg, cos_t, sin_t)


def _attn_kernel(lam_ref, gn_ref, q_ref, kc_ref, vc_ref, kl_ref, vl_ref, o_ref,
                 q2_scr, vx_scr, m_scr, acc_scr, *, tq, tk, rq):
    tc = kc_ref.shape[1]
    t = kl_ref.shape[1]

    @pl.when(pl.program_id(2) == 0)
    def _():
        lane = lax.broadcasted_iota(jnp.int32, (tc, LANES), 1)
        ones_c = jnp.where(lane == 0, 1.0, 0.0).astype(BF16)
        vx_scr[0:tc, 0:LANES] = vc_ref[0]
        vx_scr[0:tc, LANES:2 * LANES] = ones_c
        for r0 in range(0, t, tc):
            vx_scr[tc + r0:tc + r0 + tc, 0:LANES] = vl_ref[0, r0:r0 + tc, :]
            vx_scr[tc + r0:tc + r0 + tc, LANES:2 * LANES] = ones_c

    q = q_ref[0]
    qf = q.astype(F32)
    lane = lax.broadcasted_iota(jnp.int32, qf.shape, 1)
    q2_scr[0:tq] = jnp.where(lane < DIFF_HALF, qf, 0.0).astype(BF16)
    q2_scr[tq:2 * tq] = jnp.where(lane >= DIFF_HALF, qf, 0.0).astype(BF16)
    m_scr[...] = jnp.full_like(m_scr, NEG_INF)
    acc_scr[...] = jnp.zeros_like(acc_scr)

    def step(k, vx):
        nkc = k.shape[0] // LANES
        for r0 in range(0, 2 * tq, rq):
            sc = _dot_nt(q2_scr[r0:r0 + rq], k)
            cols = [sc[:, c * LANES:(c + 1) * LANES] for c in range(nkc)]
            mx = cols[0]
            for c in range(1, nkc):
                mx = jnp.maximum(mx, cols[c])
            m_prev = m_scr[r0:r0 + rq]
            m_new = jnp.maximum(m_prev, jnp.max(mx, axis=1, keepdims=True))
            alpha = jnp.exp(m_prev - m_new)
            p = jnp.concatenate([jnp.exp(cb - m_new).astype(BF16) for cb in cols], axis=1)
            alpha2 = jnp.concatenate([alpha, alpha], axis=1)
            acc_scr[r0:r0 + rq] = alpha2 * acc_scr[r0:r0 + rq] + _dot(p, vx)
            m_scr[r0:r0 + rq] = m_new

    step(kc_ref[0], vx_scr[0:tc])

    def body(j, carry):
        r = pl.multiple_of(j * tk, tk)
        step(kl_ref[0, pl.ds(r, tk), :], vx_scr[pl.ds(tc + r, tk)])
        return carry

    lax.fori_loop(0, t // tk, body, 0)

    lm = lam_ref[...]
    d1 = jnp.sum(lm[0:1] * lm[1:2], axis=1, keepdims=True)
    d2 = jnp.sum(lm[2:3] * lm[3:4], axis=1, keepdims=True)
    lam = jnp.exp(d1) - jnp.exp(d2) + LAM_INIT
    a1 = acc_scr[0:tq]
    a2 = acc_scr[tq:2 * tq]
    o1 = a1[:, 0:LANES] / a1[:, LANES:LANES + 1]
    o2 = a2[:, 0:LANES] / a2[:, LANES:LANES + 1]
    o = o1 - lam * o2
    ms = jnp.mean(o * o, axis=1, keepdims=True)
    o = o * lax.rsqrt(ms + RMS_EPS) * gn_ref[...] * (1.0 - LAM_INIT)
    o_ref[0] = o.astype(BF16)


def _diff_attn(proj_l, proj_c, lam_pad, gn, *, tq, tk, rq):
    b, t, _ = proj_l.shape
    tc = proj_c.shape[1]
    h = DIFF_HEADS
    kv = lambda off: (lambda bi, hi, qi: (bi, 0, off + hi))
    return pl.pallas_call(
        functools.partial(_attn_kernel, tq=tq, tk=tk, rq=rq),
        grid=(b, h, t // tq),
        in_specs=[pl.BlockSpec((SUBLANES, LANES), lambda bi, hi, qi: (0, 0)),
                  pl.BlockSpec((1, LANES), lambda bi, hi, qi: (0, 0)),
                  pl.BlockSpec((1, tq, LANES), lambda bi, hi, qi: (bi, qi, hi)),
                  pl.BlockSpec((1, tc, LANES), kv(h)),
                  pl.BlockSpec((1, tc, LANES), kv(2 * h)),
                  pl.BlockSpec((1, t, LANES), kv(h)),
                  pl.BlockSpec((1, t, LANES), kv(2 * h))],
        out_specs=pl.BlockSpec((1, tq, LANES), lambda bi, hi, qi: (bi, qi, hi)),
        out_shape=jax.ShapeDtypeStruct((b, t, DIFF_W), BF16),
        scratch_shapes=[pltpu.VMEM((2 * tq, LANES), BF16),
                        pltpu.VMEM((tc + t, 2 * LANES), BF16),
                        pltpu.VMEM((2 * tq, LANES), F32),
                        pltpu.VMEM((2 * tq, 2 * LANES), F32)],
        compiler_params=_cparams(("arbitrary",) * 3),
        name="diff_attn",
    )(lam_pad, gn, proj_l, proj_c, proj_c, proj_l, proj_l)


def _gate_rows(a_ref, b_ref, ba_ref, bb_ref, rt_scr):
    ng = a_ref.shape[1]
    x = (a_ref[0] + ba_ref[0]).reshape(ng * SUBLANES, LANES)
    ls = jax.nn.log_sigmoid(b_ref[0] + bb_ref[0]).reshape(ng * SUBLANES, LANES)
    ji = lax.broadcasted_iota(jnp.int32, (LANES, LANES), 0)
    si = lax.broadcasted_iota(jnp.int32, (LANES, LANES), 1)
    same = (ji // CHUNK) == (si // CHUNK)
    ones_where = lambda cond: jnp.where(cond, 1.0, 0.0).astype(BF16)
    m_pre = ones_where(same & (ji <= si))
    m_suf = ones_where(same & (ji >= si))
    m_tot = ones_where(same)
    hi, mid, lo = _split3(ls)
    mm = lambda m: _dot(hi, m) + _dot(mid, m) + _dot(lo, m)
    sub = lax.broadcasted_iota(jnp.int32, x.shape, 0) % SUBLANES
    cum = jnp.where(sub % 2 == 0, mm(m_pre), mm(m_suf))
    rt = jnp.where(sub < 2, x - cum, jnp.where(sub < 4, cum, mm(m_tot)))
    rt_scr[...] = rt.reshape(ng, SUBLANES, LANES)


def _conv_silu(raw_ref, w, bias, pad_scr, t, emit):
    pad_scr[0:SUBLANES] = jnp.zeros((SUBLANES, LANES), F32)
    pad_scr[SUBLANES + t:2 * SUBLANES + t] = jnp.zeros((SUBLANES, LANES), F32)
    pad_scr[SUBLANES:SUBLANES + t] = raw_ref[0].astype(F32)
    half = MLSTM_CONV // 2
    for g in range(t // LANES):
        acc = jnp.zeros((LANES, LANES), F32) + bias
        for j in range(MLSTM_CONV):
            off = SUBLANES + g * LANES + j - half
            acc = acc + pad_scr[off:off + LANES] * w[j:j + 1]
        emit(g, acc * jax.nn.sigmoid(acc))


def _mlstm_prologue(q_ref, k_ref, v_ref, cwq_ref, cwk_ref, cbq_ref, cbk_ref,
                    pad_scr, q_scr, kt_scr, vx_scr, t, g0):
    def emit_q(g, blk):
        q_scr[g0 + g] = blk.astype(BF16)

    def emit_k(g, blk):
        kt_scr[g0 + g] = (blk * (MLSTM_DHEAD ** -0.5)).T

    _conv_silu(q_ref, cwq_ref[0], cbq_ref[0], pad_scr, t, emit_q)
    _conv_silu(k_ref, cwk_ref[0], cbk_ref[0], pad_scr, t, emit_k)
    lane = lax.broadcasted_iota(jnp.int32, (LANES, LANES), 1)
    ones_col = jnp.where(lane == 0, 1.0, 0.0).astype(BF16)
    for g in range(t // LANES):
        vx_scr[g0 + g, :, 0:LANES] = v_ref[0, g * LANES:(g + 1) * LANES, :]
        vx_scr[g0 + g, :, LANES:2 * LANES] = ones_col


def _groups(q_scr, kt_scr, vx_scr, c_scr, m_st, items, bwd, out_scr):
    d = 1 if bwd else 0
    order = (1, 0) if bwd else (0, 1)
    lane = lax.broadcasted_iota(jnp.int32, (1, LANES), 1)
    in_c = (lane < CHUNK, lane >= CHUNK)
    li = lax.broadcasted_iota(jnp.int32, (LANES, LANES), 0)
    si = lax.broadcasted_iota(jnp.int32, (LANES, LANES), 1)
    same = (li < CHUNK) == (si < CHUNK)
    mask = same & ((si >= li) if bwd else (si <= li))
    row0 = lax.broadcasted_iota(jnp.int32, (LANES, 1), 0) < CHUNK

    st = []
    for g, rt, og in items:
        r = rt[d:d + 1, :]
        cum = rt[2 + d:3 + d, :]
        totrow = rt[4 + d:5 + d, :]
        maxr = [jnp.max(jnp.where(in_c[c], r, NEG_INF), axis=1, keepdims=True) for c in (0, 1)]
        tot = [jnp.max(jnp.where(in_c[c], totrow, NEG_INF), axis=1, keepdims=True) for c in (0, 1)]
        m_b, mu_f = [None, None], [None, None]
        for c in order:
            m_b[c] = m_st
            mu_f[c] = jnp.maximum(m_st, maxr[c])
            m_st = tot[c] + mu_f[c]
        st.append(dict(g=g, og=og, r=r, cum=cum, m_b=m_b, mu_f=mu_f))
    for e in st:
        kt = kt_scr[e["g"]]
        vx = vx_scr[e["g"]]
        e["kt"], e["vx"] = kt, vx
        e["kv"] = [_dot((kt * jnp.where(in_c[c], jnp.exp(e["r"] - e["mu_f"][c]), 0.0)).astype(BF16), vx)
                   for c in (0, 1)]
        if out_scr is not None:
            e["q"] = q_scr[e["g"]]
            e["sc"] = _dot(e["q"], kt.astype(BF16))
    c_st = c_scr[...]
    for e in st:
        e["c_bf"] = [None, None]
        for c in order:
            e["c_bf"][c] = c_st.astype(BF16)
            c_st = jnp.exp(e["m_b"][c] - e["mu_f"][c]) * c_st + e["kv"][c]
    c_scr[...] = c_st
    if out_scr is not None:
        for e in st:
            m_col = jnp.where(row0, e["m_b"][0], e["m_b"][1])
            rm = jnp.where(mask, e["r"], NEG_INF)
            mu = jnp.maximum(m_col, jnp.max(rm, axis=1, keepdims=True))
            e["mu"], e["m_col"] = mu, m_col
            e["s"] = (e["sc"] * jnp.exp(rm - mu)).astype(BF16)
        for e in st:
            e["intra"] = _dot(e["s"], e["vx"])
            e["inter"] = [_dot(e["q"][c * CHUNK:(c + 1) * CHUNK], e["c_bf"][c]) for c in (0, 1)]
        for e in st:
            inter = jnp.concatenate(e["inter"], axis=0)
            tot_o = e["intra"] + jnp.exp(e["m_col"] - e["mu"]) * inter
            num = tot_o[:, 0:LANES]
            den = tot_o[:, LANES:LANES + 1]
            cum_col = jnp.sum(jnp.where(li == si, e["cum"], 0.0), axis=1, keepdims=True)
            floor = jnp.exp(-(cum_col + e["mu"]))
            out_scr[e["og"]] = num / jnp.maximum(jnp.abs(den), floor)
    return m_st


def _mlstm_kernel(ql_ref, kl_ref, vl_ref, ol_ref, qc_ref, kc_ref, vc_ref,
                  al_ref, bl_ref, ac_ref, bc_ref, ba_ref, bb_ref,
                  cwq_ref, cwk_ref, cbq_ref, cbk_ref, gn_ref, o_ref,
                  pad_scr, q_scr, kt_scr, vx_scr, rtc_scr, rtl_scr, cf_scr, cb_scr, hf_scr, hb_scr,
                  *, t, tc, gpi):
    _gate_rows(ac_ref.at[0], bc_ref.at[0], ba_ref, bb_ref, rtc_scr)
    _gate_rows(al_ref.at[0], bl_ref.at[0], ba_ref, bb_ref, rtl_scr)
    _mlstm_prologue(qc_ref, kc_ref, vc_ref, cwq_ref, cwk_ref, cbq_ref, cbk_ref,
                    pad_scr, q_scr, kt_scr, vx_scr, tc, 0)
    _mlstm_prologue(ql_ref, kl_ref, vl_ref, cwq_ref, cwk_ref, cbq_ref, cbk_ref,
                    pad_scr, q_scr, kt_scr, vx_scr, t, tc // LANES)
    cf_scr[...] = jnp.zeros_like(cf_scr)
    cb_scr[...] = jnp.zeros_like(cb_scr)
    m0 = jnp.zeros((1, 1), F32)

    def group(rt_scr, ng, g0, outs, gpi):
        def body(it, carry):
            mf, mb = carry
            items_f, items_b = [], []
            for u in range(gpi):
                gf = it * gpi + u
                gb = ng - 1 - gf
                items_f.append((g0 + gf, rt_scr[gf], gf))
                items_b.append((g0 + gb, rt_scr[gb], gb))
            args = (q_scr, kt_scr, vx_scr)
            mf = _groups(*args, cf_scr, mf, items_f, False, outs and hf_scr)
            mb = _groups(*args, cb_scr, mb, items_b, True, outs and hb_scr)
            return mf, mb
        return body

    ngc = tc // LANES
    ngl = t // LANES
    gc = math.gcd(ngc, gpi)
    gl = math.gcd(ngl, gpi)
    carry = lax.fori_loop(0, ngc // gc, group(rtc_scr, ngc, 0, None, gc), (m0, m0))
    lax.fori_loop(0, ngl // gl, group(rtl_scr, ngl, ngc, True, gl), carry)

    hm = (hf_scr[...] + hb_scr[...]).reshape(t, LANES)
    ms = jnp.mean(hm * hm, axis=1, keepdims=True)
    out = hm * lax.rsqrt(ms + RMS_EPS) * gn_ref[...] * jax.nn.sigmoid(ol_ref[0].astype(F32))
    o_ref[0] = out.astype(BF16)


def _mlstm(proj_l, proj_c, ga_l, gb_l, ga_c, gb_c, bias_a, bias_b, conv_w, conv_b, gn):
    b, t, _ = proj_l.shape
    tc = proj_c.shape[1]
    h = MLSTM_HEADS
    col = lambda off: (lambda bi, hi: (bi, 0, off + hi))
    q0 = (3 * DIFF_W) // LANES
    gspec = lambda ng: pl.BlockSpec((1, 1, ng, SUBLANES, LANES), lambda bi, hi: (bi, hi, 0, 0, 0))
    hspec = pl.BlockSpec((1, SUBLANES, LANES), lambda bi, hi: (hi, 0, 0))
    cw = jnp.pad(conv_w, ((0, SUBLANES - MLSTM_CONV), (0, 0)))
    cwspec = lambda off: pl.BlockSpec((1, SUBLANES, LANES), lambda bi, hi: (0, 0, off + hi))
    cbspec = lambda off: pl.BlockSpec((1, 1, LANES), lambda bi, hi: (0, 0, off + hi))
    tt = t + tc
    return pl.pallas_call(
        functools.partial(_mlstm_kernel, t=t, tc=tc, gpi=MLSTM_GROUPS_PER_STEP),
        grid=(b, h),
        in_specs=[pl.BlockSpec((1, t, LANES), col(q0)),
                  pl.BlockSpec((1, t, LANES), col(q0 + h)),
                  pl.BlockSpec((1, t, LANES), col(q0 + 2 * h)),
                  pl.BlockSpec((1, t, LANES), col(q0 + 3 * h)),
                  pl.BlockSpec((1, tc, LANES), col(q0)),
                  pl.BlockSpec((1, tc, LANES), col(q0 + h)),
                  pl.BlockSpec((1, tc, LANES), col(q0 + 2 * h)),
                  gspec(t // LANES), gspec(t // LANES), gspec(tc // LANES), gspec(tc // LANES),
                  hspec, hspec,
                  cwspec(0), cwspec(h), cbspec(0), cbspec(h),
                  pl.BlockSpec((1, LANES), lambda bi, hi: (0, 0))],
        out_specs=pl.BlockSpec((1, t, LANES), lambda bi, hi: (bi, 0, hi)),
        out_shape=jax.ShapeDtypeStruct((b, t, ML_W), BF16),
        scratch_shapes=[pltpu.VMEM((t + 2 * SUBLANES, LANES), F32),
                        pltpu.VMEM((tt // LANES, LANES, LANES), BF16),
                        pltpu.VMEM((tt // LANES, LANES, LANES), F32),
                        pltpu.VMEM((tt // LANES, LANES, 2 * LANES), BF16),
                        pltpu.VMEM((tc // LANES, SUBLANES, LANES), F32),
                        pltpu.VMEM((t // LANES, SUBLANES, LANES), F32),
                        pltpu.VMEM((LANES, 2 * LANES), F32),
                        pltpu.VMEM((LANES, 2 * LANES), F32),
                        pltpu.VMEM((t // LANES, LANES, LANES), F32),
                        pltpu.VMEM((t // LANES, LANES, LANES), F32)],
        compiler_params=_cparams(("arbitrary", "arbitrary")),
        name="mlstm",
    )(proj_l, proj_l, proj_l, proj_l, proj_c, proj_c, proj_c,
      ga_l, gb_l, ga_c, gb_c, bias_a, bias_b,
      cw.reshape(1, SUBLANES, 2 * ML_W), cw.reshape(1, SUBLANES, 2 * ML_W),
      conv_b.reshape(1, 1, 2 * ML_W), conv_b.reshape(1, 1, 2 * ML_W), gn)


def _layer_norm(v, g, b):
    mu = jnp.mean(v, axis=1, keepdims=True)
    c = v - mu
    var = jnp.mean(c * c, axis=1, keepdims=True)
    return c * lax.rsqrt(var + LN_EPS) * g + b


def _outproj_kernel(d_ref, m_ref, w_ref, x_ref, g1_ref, lg_ref, lb_ref, sh_ref, sc_ref,
                    x1_ref, h2_ref):
    y = _dot(d_ref[0], w_ref[0]) + _dot(m_ref[0], w_ref[1])
    v = DEEPNORM_ALPHA * x_ref[0] + g1_ref[0] * y
    x1 = _layer_norm(v, lg_ref[...], lb_ref[...])
    x1_ref[0] = x1
    h2_ref[0] = (x1 * (1.0 + sc_ref[0]) + sh_ref[0]).astype(BF16)


def _out_proj(d_lat, m_lat, w_out2, x, g1, ln_g, ln_b, sh2, sc2, *, tm):
    b, t, d = x.shape
    hw = d_lat.shape[2]
    row = pl.BlockSpec((1, 1, d), lambda bi, i: (bi, 0, 0))
    vec = pl.BlockSpec((1, d), lambda bi, i: (0, 0))
    return pl.pallas_call(
        _outproj_kernel,
        grid=(b, t // tm),
        in_specs=[pl.BlockSpec((1, tm, hw), lambda bi, i: (bi, i, 0)),
                  pl.BlockSpec((1, tm, hw), lambda bi, i: (bi, i, 0)),
                  pl.BlockSpec((2, hw, d), lambda bi, i: (0, 0, 0)),
                  pl.BlockSpec((1, tm, d), lambda bi, i: (bi, i, 0)),
                  row, vec, vec, row, row],
        out_specs=[pl.BlockSpec((1, tm, d), lambda bi, i: (bi, i, 0)),
                   pl.BlockSpec((1, tm, d), lambda bi, i: (bi, i, 0))],
        out_shape=[jax.ShapeDtypeStruct((b, t, d), F32),
                   jax.ShapeDtypeStruct((b, t, d), BF16)],
        compiler_params=_cparams(("arbitrary", "arbitrary")),
        name="out_proj",
    )(d_lat, m_lat, w_out2, x, g1, ln_g, ln_b, sh2, sc2)


def _top_rows(x, k, dst_scr):
    cur = x
    for r in range(k):
        m = jnp.max(cur, axis=0, keepdims=True)
        dst_scr[r:r + 1] = m
        if r + 1 < k:
            cur = jnp.where(cur == m, NEG_INF, cur)


def _sort16_pairs():
    def merge(lo, hi, r):
        step = r * 2
        if step < hi - lo:
            yield from merge(lo, hi, step)
            yield from merge(lo + r, hi, step)
            yield from [(i, i + r) for i in range(lo + r, hi - r, step)]
        else:
            yield (lo, lo + r)

    def sort(lo, hi):
        if hi - lo >= 1:
            mid = lo + (hi - lo) // 2
            yield from sort(lo, mid)
            yield from sort(mid + 1, hi)
            yield from merge(lo, hi, 1)

    return list(sort(0, 15))


def _top16_of_128(x, dst_scr):
    k = PEER_TOPK
    y = [x[r * SUBLANES:(r + 1) * SUBLANES] for r in range(k)]
    for a, b in _sort16_pairs():
        y[a], y[b] = jnp.maximum(y[a], y[b]), jnp.minimum(y[a], y[b])
    for r in range(k):
        m = jnp.max(y[0], axis=0, keepdims=True)
        dst_scr[r:r + 1] = m
        if r + 1 < k:
            hit = y[0] == m
            for i in range(k - 1 - r):
                y[i] = jnp.where(hit, y[i + 1], y[i])


def _rank_among(x, v):
    row = lambda i: v[i:i + 1]
    pick = jnp.where
    c1 = x < row(7)
    c2 = x < pick(c1, row(11), row(3))
    c3 = x < pick(c1, pick(c2, row(13), row(9)), pick(c2, row(5), row(1)))
    c4 = x < pick(c1, pick(c2, pick(c3, row(14), row(12)), pick(c3, row(10), row(8))),
                  pick(c2, pick(c3, row(6), row(4)), pick(c3, row(2), row(0))))
    c5 = x < row(15)
    bit = lambda c, w: jnp.where(c, float(w), 0.0)
    return bit(c1, 8) + bit(c2, 4) + bit(c3, 2) + bit(c4, 1) + bit(c5, 1)


def _peer_prep_kernel(h_ref, wq_ref, k1_ref, k2_ref, n1_ref, r2_ref, e1_ref, e2_ref,
                      v1_scr, v2_scr, tp_scr):
    q = _dot(h_ref[...], wq_ref[...])
    half = PEER_DQ // 2
    for h in range(PEER_HEADS):
        q1 = q[:, h * PEER_DQ:h * PEER_DQ + half].astype(BF16)
        q2 = q[:, h * PEER_DQ + half:(h + 1) * PEER_DQ].astype(BF16)
        s1 = _dot_nt(k1_ref[h], q1)
        s2 = _dot_nt(k2_ref[h], q2)
        _top16_of_128(s1, v1_scr)
        _top16_of_128(s2, v2_scr)
        v1 = v1_scr[...]
        v2 = v2_scr[...]
        pieces = [v1[0:1] + v2]
        pieces += [v1[a:a + 1] + v2[0:SUBLANES] for a in range(1, SUBLANES)]
        pieces += [v1[SUBLANES:PEER_TOPK] + v2[0:1]]
        cand = jnp.concatenate(pieces, axis=0)
        _top_rows(cand, PEER_TOPK, tp_scr)
        tp = tp_scr[...]
        top0 = tp[0:1]
        tau = tp[PEER_TOPK - 1:PEER_TOPK]
        z = jnp.sum(jnp.exp(tp - top0), axis=0, keepdims=True)
        cnt = jnp.zeros(s1.shape, F32)
        for bb in range(PEER_TOPK):
            n_a = jnp.sum(jnp.where(v1[bb:bb + 1] + v2 >= tau, 1.0, 0.0), axis=0, keepdims=True)
            cnt = jnp.where(s1 == v1[bb:bb + 1], n_a, cnt)
        n1_ref[h] = cnt
        r2_ref[h] = _rank_among(s2, v2).astype(BF16)
        e1_ref[h] = jnp.exp(s1 - v1[0:1]) / z
        e2_ref[h] = jnp.exp(s2 - v2[0:1]).astype(BF16)


def _peer_prep(h2, wq, k1, k2, *, tt):
    n, d = h2.shape
    hp = PEER_HEADS
    big = jax.ShapeDtypeStruct((hp, PEER_NKEYS, n), F32)
    big16 = jax.ShapeDtypeStruct((hp, PEER_NKEYS, n), BF16)
    bspec = pl.BlockSpec((hp, PEER_NKEYS, tt), lambda i: (0, 0, i))
    kspec = pl.BlockSpec((hp, PEER_NKEYS, PEER_DQ // 2), lambda i: (0, 0, 0))
    return pl.pallas_call(
        _peer_prep_kernel,
        grid=(n // tt,),
        in_specs=[pl.BlockSpec((tt, d), lambda i: (i, 0)),
                  pl.BlockSpec((d, hp * PEER_DQ), lambda i: (0, 0)),
                  kspec, kspec],
        out_specs=[bspec, bspec, bspec, bspec],
        out_shape=[big, big16, big, big16],
        scratch_shapes=[pltpu.VMEM((PEER_TOPK, tt), F32),
                        pltpu.VMEM((PEER_TOPK, tt), F32),
                        pltpu.VMEM((PEER_TOPK, tt), F32)],
        compiler_params=_cparams(("arbitrary",)),
        name="peer_prep",
    )(h2, wq, k1, k2)


def _transpose_cast_kernel(v_ref, o_ref):
    o_ref[...] = v_ref[...].T.astype(BF16)


def _transpose_cast(v, *, te):
    ne, d = v.shape
    return pl.pallas_call(
        _transpose_cast_kernel,
        grid=(ne // te,),
        in_specs=[pl.BlockSpec((te, d), lambda j: (j, 0))],
        out_specs=pl.BlockSpec((d, te), lambda j: (0, j)),
        out_shape=jax.ShapeDtypeStruct((d, ne), BF16),
        compiler_params=_cparams(("arbitrary",)),
        name="peer_v_layout",
    )(v)


def _peer_dense_kernel(h_ref, u_ref, vt_ref, n1_ref, r2_ref, e1_ref, e2_ref, x1_ref, g2_ref, lg_ref, lb_ref,
                       o_ref, ht_scr, z_scr, acc_scr, *, te):
    j = pl.program_id(1)
    nj = pl.num_programs(1)
    nk = PEER_NKEYS

    @pl.when(j == 0)
    def _():
        ht_scr[...] = h_ref[...].astype(F32).T.astype(BF16)
        acc_scr[...] = jnp.zeros_like(acc_scr)
        z_scr[1] = jnp.zeros(z_scr.shape[1:], F32)

    prev = jnp.maximum(j - 1, 0)

    tt = ht_scr.shape[1]
    tn = TILES["peer_token_slice"]

    def body(wslot, rslot):
        for n0 in range(0, tt, tn):
            z_scr[wslot, :, n0:n0 + tn] = _dot(u_ref[...], ht_scr[:, n0:n0 + tn])
            ws = []
            for a in range(te // nk):
                i1 = prev * (te // nk) + a
                g = None
                for h in range(PEER_HEADS):
                    n1row = n1_ref[h, pl.ds(i1, 1), n0:n0 + tn].astype(BF16)
                    e1row = e1_ref[h, pl.ds(i1, 1), n0:n0 + tn].astype(BF16)
                    hit = r2_ref[h, :, n0:n0 + tn] < n1row
                    term = jnp.where(hit, e2_ref[h, :, n0:n0 + tn], jnp.zeros((), BF16)) * e1row
                    g = term if g is None else g + term
                z = z_scr[rslot, a * nk:(a + 1) * nk, n0:n0 + tn]
                gelu = 0.5 * z * (1.0 + lax.erf(z * math.sqrt(0.5)))
                ws.append(gelu.astype(BF16) * g)
            acc_scr[:, n0:n0 + tn] += _dot(vt_ref[...], jnp.concatenate(ws, axis=0))

    @pl.when(j % 2 == 0)
    def _():
        body(0, 1)

    @pl.when(j % 2 == 1)
    def _():
        body(1, 0)

    @pl.when(j == nj - 1)
    def _():
        v = DEEPNORM_ALPHA * x1_ref[...] + g2_ref[0] * acc_scr[...].T
        o_ref[...] = _layer_norm(v, lg_ref[...], lb_ref[...])


def _peer_dense(h2, u_bf, vt_bf, n1, r2, e1, e2, x1, g2, ln_g, ln_b, *, tt, te):
    n, d = h2.shape
    ne = u_bf.shape[0]
    hp = PEER_HEADS
    nb = ne // te
    blocks_per_sample = n // g2.shape[0] // tt
    bspec = pl.BlockSpec((hp, PEER_NKEYS, tt), lambda i, j: (0, 0, i))
    vec = pl.BlockSpec((1, d), lambda i, j: (0, 0))
    return pl.pallas_call(
        functools.partial(_peer_dense_kernel, te=te),
        grid=(n // tt, nb + 1),
        in_specs=[pl.BlockSpec((tt, d), lambda i, j: (i, 0)),
                  pl.BlockSpec((te, d), lambda i, j: (jnp.minimum(j, nb - 1), 0)),
                  pl.BlockSpec((d, te), lambda i, j: (0, jnp.maximum(j - 1, 0))),
                  bspec, bspec, bspec, bspec,
                  pl.BlockSpec((tt, d), lambda i, j: (i, 0)),
                  pl.BlockSpec((1, 1, d), lambda i, j: (i // blocks_per_sample, 0, 0)),
                  vec, vec],
        out_specs=pl.BlockSpec((tt, d), lambda i, j: (i, 0)),
        out_shape=jax.ShapeDtypeStruct((n, d), F32),
        scratch_shapes=[pltpu.VMEM((d, tt), BF16),
                        pltpu.VMEM((2, te, tt), F32),
                        pltpu.VMEM((d, tt), F32)],
        compiler_params=_cparams(("arbitrary", "arbitrary")),
        name="peer_dense",
    )(h2, u_bf, vt_bf, n1, r2, e1, e2, x1, g2, ln_g, ln_b)


def _rope_tables(t):
    pos = jnp.arange(t, dtype=jnp.int32)
    row = (pos // GRID_W).astype(F32)
    col = (pos % GRID_W).astype(F32)
    inv = ROPE_BASE ** (-jnp.arange(ROPE_NF, dtype=F32) / ROPE_NF)
    lane = jnp.arange(LANES)
    use_col = ((lane // (2 * ROPE_NF)) % 2) == 1
    p = jnp.where(use_col[None, :], col[:, None], row[:, None])
    ang = p * inv[lane % ROPE_NF][None, :]
    sign = jnp.where((lane % (2 * ROPE_NF)) < ROPE_NF, -1.0, 1.0).astype(F32)
    return jnp.cos(ang), jnp.sin(ang) * sign[None, :]


def _gate_layout(g):
    b, _, t = g.shape
    gt = g[:, :N_GATES * MLSTM_HEADS, :].reshape(b, N_GATES, MLSTM_HEADS, t // LANES, LANES)
    gt = gt.transpose(0, 2, 3, 1, 4)
    zeros = jnp.zeros_like(gt[:, :, :, 0:1])
    a = jnp.concatenate([gt[:, :, :, 0:1], gt[:, :, :, 2:3]] + [zeros] * 6, axis=3)
    fb = jnp.concatenate([gt[:, :, :, 1:2], gt[:, :, :, 3:4]], axis=3)
    return a, jnp.tile(fb, (1, 1, 1, 4, 1))


def _gate_bias(gate_b):
    gb = gate_b.astype(F32)
    z = jnp.zeros_like(gb[0])
    a = jnp.stack([gb[0], gb[2]] + [z] * 6, axis=1)
    bm = jnp.stack([gb[1], gb[3]] * 4, axis=1)
    bc = lambda v: jnp.broadcast_to(v[:, :, None], (MLSTM_HEADS, SUBLANES, LANES))
    return bc(a), bc(bm)


def kernel(x, c, ctx, c_ctx, w_ada, b_ada, w_in, conv_w, conv_b, gate_b, diff_lambda, diff_norm_g,
           mlstm_norm_g, w_out, ln1_g, ln1_b, ln2_g, ln2_b, peer_wq, peer_keys, peer_u, peer_v):
    b, t, d = x.shape
    tc = ctx.shape[1]
    l = 0

    c_rows = jnp.concatenate([c, c_ctx[None, :]], axis=0)
    mod = _adaln(c_rows, w_ada[l], b_ada[l])
    sh1, sc1, g1, sh2, sc2, g2 = [m[:, None, :] for m in jnp.split(mod[:b], 6, axis=1)]
    csh1, csc1 = [m[:, None, :] for m in jnp.split(mod[b:b + 1], 6, axis=1)[:2]]

    w = w_in[l]
    col_scale = jnp.where(jnp.arange(MAIN_COLS) < DIFF_W, DIFF_HALF ** -0.5, 1.0).astype(F32)
    w_main = (w[:, :MAIN_COLS] * col_scale[None, :]).astype(BF16)
    wg = jnp.pad(w[:, MAIN_COLS:], ((0, 0), (0, LANES - N_GATES * MLSTM_HEADS)))
    wg_hi = wg.astype(BF16)
    wg_lo = (wg - wg_hi.astype(F32)).astype(BF16)
    wg2 = jnp.concatenate([wg_hi, wg_lo], axis=1)
    cos_t, sin_t = _rope_tables(t)
    proj_l, gates_l = _in_proj(x, sh1, sc1, w_main, wg2, cos_t, sin_t, tm=min(TILES["in_proj_rows"], t))
    nc = b * tc
    proj_c, gates_c = _in_proj(ctx.reshape(1, nc, d), csh1, csc1, w_main, wg2,
                               jnp.ones((nc, LANES), F32), jnp.zeros((nc, LANES), F32), tm=nc)
    proj_c = proj_c.reshape(b, tc, MAIN_COLS)
    gates_c = gates_c.reshape(LANES, b, tc).transpose(1, 0, 2)

    lam_pad = jnp.pad(diff_lambda[l].astype(F32), ((0, SUBLANES - 4), (0, LANES - DIFF_HALF)))
    d_lat = _diff_attn(proj_l, proj_c, lam_pad, diff_norm_g[l].reshape(1, LANES),
                       tq=min(TILES["attn_q_rows"], t), tk=min(TILES["attn_kv_rows"], t),
                       rq=TILES["attn_row_group"])

    ga_l, gb_l = _gate_layout(gates_l)
    ga_c, gb_c = _gate_layout(gates_c)
    bias_a, bias_b = _gate_bias(gate_b[l])
    m_lat = _mlstm(proj_l, proj_c, ga_l, gb_l, ga_c, gb_c, bias_a, bias_b,
                   conv_w[l], conv_b[l], mlstm_norm_g[l].reshape(1, LANES))

    w_out2 = w_out[l].astype(BF16).reshape(2, DIFF_W, d)
    x1, h2 = _out_proj(d_lat, m_lat, w_out2, x, g1, ln1_g[l].reshape(1, d), ln1_b[l].reshape(1, d),
                       sh2, sc2, tm=min(TILES["out_proj_rows"], t))

    n = b * t
    h2f = h2.reshape(n, d)
    keys = peer_keys[l].astype(BF16)
    tt = min(TILES["peer_tokens"], t)
    n1, r2, e1, e2 = _peer_prep(h2f, peer_wq[l].astype(BF16), keys[0], keys[1], tt=tt)
    vt_bf = _transpose_cast(peer_v[l], te=TILES["peer_experts"])
    out = _peer_dense(h2f, peer_u[l].astype(BF16), vt_bf, n1, r2, e1, e2,
                      x1.reshape(n, d), g2, ln2_g[l].reshape(1, d), ln2_b[l].reshape(1, d),
                      tt=tt, te=TILES["peer_experts"])
    return out.reshape(b, t, d)
```

```python
import functools
import math

import jax
import jax.numpy as jnp
from jax import lax
from jax.experimental import pallas as pl
from jax.experimental.pallas import tpu as pltpu

F32 = jnp.float32
BF16 = jnp.bfloat16

LANES = 128
SUBLANES = 8
MXU_WIDTH = 256
VMEM_LIMIT = 56 * 1024 * 1024

TILES = dict(
    adaln_cols=512,
    in_proj_rows=1024, in_proj_cols=512,
    attn_q_rows=4096, attn_kv_rows=2048, attn_row_group=MXU_WIDTH,
    out_proj_rows=512,
    peer_tokens=512, peer_experts=512, peer_token_slice=MXU_WIDTH,
)

GRID_W = 64
DIFF_HEADS = 8
DIFF_HALF = 64
DIFF_W = DIFF_HEADS * 2 * DIFF_HALF
MLSTM_HEADS = 8
MLSTM_DHEAD = 128
ML_W = MLSTM_HEADS * MLSTM_DHEAD
MLSTM_CONV = 5
CHUNK = 64
MLSTM_GROUPS_PER_STEP = 4
N_GATES = 4
MAIN_COLS = 3 * DIFF_W + 4 * ML_W
ROPE_COLS = 2 * DIFF_W
ROPE_BASE = 10000.0
ROPE_NF = 16
PEER_HEADS = 8
PEER_NKEYS = 128
PEER_DQ = 256
PEER_TOPK = 16
DEPTH = 1
DEEPNORM_ALPHA = (2.0 * DEPTH) ** 0.25
LN_EPS = 1e-5
RMS_EPS = 1e-6
LAM_INIT = 0.8 - 0.6 * math.exp(-0.3 * 0)
NEG_INF = float("-inf")


def _cparams(sem):
    return pltpu.CompilerParams(dimension_semantics=sem, vmem_limit_bytes=VMEM_LIMIT)


def _dot(a, b):
    return jnp.dot(a, b, preferred_element_type=F32)


def _dot_nt(a, b):
    return lax.dot_general(a, b, (((1,), (1,)), ((), ())), preferred_element_type=F32)


def _split3(x):
    hi = x.astype(BF16)
    r1 = x - hi.astype(F32)
    mid = r1.astype(BF16)
    lo = (r1 - mid.astype(F32)).astype(BF16)
    return hi, mid, lo


def _adaln_kernel(cb_ref, w_ref, b_ref, o_ref, act_scr, *, n_rows):
    @pl.when(pl.program_id(0) == 0)
    def _():
        c = cb_ref[...]
        act_scr[...] = c * jax.nn.sigmoid(c)

    tn = w_ref.shape[1]
    o_ref[...] = jnp.zeros_like(o_ref)
    for cb in range(tn // LANES):
        w = w_ref[:, cb * LANES:(cb + 1) * LANES]
        for r in range(n_rows):
            s = jnp.sum(act_scr[r] * w, axis=0, keepdims=True)
            o_ref[r:r + 1, cb * LANES:(cb + 1) * LANES] = s + b_ref[:, cb * LANES:(cb + 1) * LANES]


def _adaln(c_rows, w_ada, b_ada):
    n_rows, k = c_rows.shape
    n = w_ada.shape[1]
    tn = TILES["adaln_cols"]
    cb = jnp.broadcast_to(c_rows[:, :, None], (n_rows, k, LANES))
    return pl.pallas_call(
        functools.partial(_adaln_kernel, n_rows=n_rows),
        grid=(n // tn,),
        in_specs=[pl.BlockSpec((n_rows, k, LANES), lambda j: (0, 0, 0)),
                  pl.BlockSpec((k, tn), lambda j: (0, j)),
                  pl.BlockSpec((1, tn), lambda j: (0, j))],
        out_specs=pl.BlockSpec((SUBLANES, tn), lambda j: (0, j)),
        out_shape=jax.ShapeDtypeStruct((SUBLANES, n), F32),
        scratch_shapes=[pltpu.VMEM((n_rows, k, LANES), F32)],
        compiler_params=_cparams(("arbitrary",)),
        name="adaln",
    )(cb, w_ada, b_ada.reshape(1, n))


def _inproj_kernel(x_ref, sh_ref, sc_ref, w_ref, wg_ref, cos_ref, sin_ref, o_ref, g_ref, h_scr,
                   *, n_rope_tiles):
    j = pl.program_id(2)

    @pl.when(j == 0)
    def _():
        h = x_ref[0] * (1.0 + sc_ref[0]) + sh_ref[0]
        hb = h.astype(BF16)
        h_scr[...] = hb
        hl = (h - hb.astype(F32)).astype(BF16)
        a = _dot(hb, wg_ref[...])
        g = a[:, 0:LANES] + a[:, LANES:2 * LANES] + _dot(hl, wg_ref[:, 0:LANES])
        g_ref[0] = g.T

    acc = _dot(h_scr[...], w_ref[...])
    tn = acc.shape[1]

    @pl.when(j < n_rope_tiles)
    def _():
        cos = cos_ref[...]
        sin = sin_ref[...]
        lane = lax.broadcasted_iota(jnp.int32, cos.shape, 1)
        first = (lane % (2 * ROPE_NF)) < ROPE_NF
        for cb in range(tn // LANES):
            a = acc[:, cb * LANES:(cb + 1) * LANES]
            sw = jnp.where(first, pltpu.roll(a, LANES - ROPE_NF, 1), pltpu.roll(a, ROPE_NF, 1))
            o_ref[0, :, cb * LANES:(cb + 1) * LANES] = (a * cos + sw * sin).astype(BF16)

    @pl.when(j >= n_rope_tiles)
    def _():
        o_ref[0] = acc.astype(BF16)


def _in_proj(x, shift, scale, w_main, wg, cos_t, sin_t, *, tm):
    b, t, d = x.shape
    n = w_main.shape[1]
    tn = TILES["in_proj_cols"]
    table = pl.BlockSpec((tm, LANES), lambda bi, i, j: (i, 0))
    return pl.pallas_call(
        functools.partial(_inproj_kernel, n_rope_tiles=ROPE_COLS // tn),
        grid=(b, t // tm, n // tn),
        in_specs=[pl.BlockSpec((1, tm, d), lambda bi, i, j: (bi, i, 0)),
                  pl.BlockSpec((1, 1, d), lambda bi, i, j: (bi, 0, 0)),
                  pl.BlockSpec((1, 1, d), lambda bi, i, j: (bi, 0, 0)),
                  pl.BlockSpec((d, tn), lambda bi, i, j: (0, j)),
                  pl.BlockSpec((d, 2 * LANES), lambda bi, i, j: (0, 0)),
                  table, table],
        out_specs=[pl.BlockSpec((1, tm, tn), lambda bi, i, j: (bi, i, j)),
                   pl.BlockSpec((1, LANES, tm), lambda bi, i, j: (bi, 0, i))],
        out_shape=[jax.ShapeDtypeStruct((b, t, n), BF16),
                   jax.ShapeDtypeStruct((b, LANES, t), F32)],
        scratch_shapes=[pltpu.VMEM((tm, d), BF16)],
        compiler_params=_cparams(("arbitrary", "arbitrary", "arbitrary")),
        name="in_proj",
    )(x, shift, scale, w_main, wg, cos_t, sin_t)


def _attn_kernel(lam_ref, gn_ref, q_ref, kc_ref, vc_ref, kl_ref, vl_ref, o_ref,
                 q2_scr, vx_scr, m_scr, acc_scr, *, tq, tk, rq):
    tc = kc_ref.shape[1]
    t = kl_ref.shape[1]

    @pl.when(pl.program_id(2) == 0)
    def _():
        lane = lax.broadcasted_iota(jnp.int32, (tc, LANES), 1)
        ones_c = jnp.where(lane == 0, 1.0, 0.0).astype(BF16)
        vx_scr[0:tc, 0:LANES] = vc_ref[0]
        vx_scr[0:tc, LANES:2 * LANES] = ones_c
        for r0 in range(0, t, tc):
            vx_scr[tc + r0:tc + r0 + tc, 0:LANES] = vl_ref[0, r0:r0 + tc, :]
            vx_scr[tc + r0:tc + r0 + tc, LANES:2 * LANES] = ones_c

    q = q_ref[0]
    qf = q.astype(F32)
    lane = lax.broadcasted_iota(jnp.int32, qf.shape, 1)
    q2_scr[0:tq] = jnp.where(lane < DIFF_HALF, qf, 0.0).astype(BF16)
    q2_scr[tq:2 * tq] = jnp.where(lane >= DIFF_HALF, qf, 0.0).astype(BF16)
    m_scr[...] = jnp.full_like(m_scr, NEG_INF)
    acc_scr[...] = jnp.zeros_like(acc_scr)

    def step(k, vx):
        nkc = k.shape[0] // LANES
        for r0 in range(0, 2 * tq, rq):
            sc = _dot_nt(q2_scr[r0:r0 + rq], k)
            cols = [sc[:, c * LANES:(c + 1) * LANES] for c in range(nkc)]
            mx = cols[0]
            for c in range(1, nkc):
                mx = jnp.maximum(mx, cols[c])
            m_prev = m_scr[r0:r0 + rq]
            m_new = jnp.maximum(m_prev, jnp.max(mx, axis=1, keepdims=True))
            alpha = jnp.exp(m_prev - m_new)
            p = jnp.concatenate([jnp.exp(cb - m_new).astype(BF16) for cb in cols], axis=1)
            alpha2 = jnp.concatenate([alpha, alpha], axis=1)
            acc_scr[r0:r0 + rq] = alpha2 * acc_scr[r0:r0 + rq] + _dot(p, vx)
            m_scr[r0:r0 + rq] = m_new

    step(kc_ref[0], vx_scr[0:tc])

    def body(j, carry):
        r = pl.multiple_of(j * tk, tk)
        step(kl_ref[0, pl.ds(r, tk), :], vx_scr[pl.ds(tc + r, tk)])
        return carry

    lax.fori_loop(0, t // tk, body, 0)

    lm = lam_ref[...]
    d1 = jnp.sum(lm[0:1] * lm[1:2], axis=1, keepdims=True)
    d2 = jnp.sum(lm[2:3] * lm[3:4], axis=1, keepdims=True)
    lam = jnp.exp(d1) - jnp.exp(d2) + LAM_INIT
    a1 = acc_scr[0:tq]
    a2 = acc_scr[tq:2 * tq]
    o1 = a1[:, 0:LANES] / a1[:, LANES:LANES + 1]
    o2 = a2[:, 0:LANES] / a2[:, LANES:LANES + 1]
    o = o1 - lam * o2
    ms = jnp.mean(o * o, axis=1, keepdims=True)
    o = o * lax.rsqrt(ms + RMS_EPS) * gn_ref[...] * (1.0 - LAM_INIT)
    o_ref[0] = o.astype(BF16)


def _diff_attn(proj_l, proj_c, lam_pad, gn, *, tq, tk, rq):
    b, t, _ = proj_l.shape
    tc = proj_c.shape[1]
    h = DIFF_HEADS
    kv = lambda off: (lambda bi, hi, qi: (bi, 0, off + hi))
    return pl.pallas_call(
        functools.partial(_attn_kernel, tq=tq, tk=tk, rq=rq),
        grid=(b, h, t // tq),
        in_specs=[pl.BlockSpec((SUBLANES, LANES), lambda bi, hi, qi: (0, 0)),
                  pl.BlockSpec((1, LANES), lambda bi, hi, qi: (0, 0)),
                  pl.BlockSpec((1, tq, LANES), lambda bi, hi, qi: (bi, qi, hi)),
                  pl.BlockSpec((1, tc, LANES), kv(h)),
                  pl.BlockSpec((1, tc, LANES), kv(2 * h)),
                  pl.BlockSpec((1, t, LANES), kv(h)),
                  pl.BlockSpec((1, t, LANES), kv(2 * h))],
        out_specs=pl.BlockSpec((1, tq, LANES), lambda bi, hi, qi: (bi, qi, hi)),
        out_shape=jax.ShapeDtypeStruct((b, t, DIFF_W), BF16),
        scratch_shapes=[pltpu.VMEM((2 * tq, LANES), BF16),
                        pltpu.VMEM((tc + t, 2 * LANES), BF16),
                        pltpu.VMEM((2 * tq, LANES), F32),
                        pltpu.VMEM((2 * tq, 2 * LANES), F32)],
        compiler_params=_cparams(("arbitrary",) * 3),
        name="diff_attn",
    )(lam_pad, gn, proj_l, proj_c, proj_c, proj_l, proj_l)


def _gate_rows(a_ref, b_ref, ba_ref, bb_ref, rt_scr):
    ng = a_ref.shape[1]
    x = (a_ref[0] + ba_ref[0]).reshape(ng * SUBLANES, LANES)
    ls = jax.nn.log_sigmoid(b_ref[0] + bb_ref[0]).reshape(ng * SUBLANES, LANES)
    ji = lax.broadcasted_iota(jnp.int32, (LANES, LANES), 0)
    si = lax.broadcasted_iota(jnp.int32, (LANES, LANES), 1)
    same = (ji // CHUNK) == (si // CHUNK)
    ones_where = lambda cond: jnp.where(cond, 1.0, 0.0).astype(BF16)
    m_pre = ones_where(same & (ji <= si))
    m_suf = ones_where(same & (ji >= si))
    m_tot = ones_where(same)
    hi, mid, lo = _split3(ls)
    mm = lambda m: _dot(hi, m) + _dot(mid, m) + _dot(lo, m)
    sub = lax.broadcasted_iota(jnp.int32, x.shape, 0) % SUBLANES
    cum = jnp.where(sub % 2 == 0, mm(m_pre), mm(m_suf))
    rt = jnp.where(sub < 2, x - cum, jnp.where(sub < 4, cum, mm(m_tot)))
    rt_scr[...] = rt.reshape(ng, SUBLANES, LANES)


def _conv_silu(raw_ref, w, bias, pad_scr, t, emit):
    pad_scr[0:SUBLANES] = jnp.zeros((SUBLANES, LANES), F32)
    pad_scr[SUBLANES + t:2 * SUBLANES + t] = jnp.zeros((SUBLANES, LANES), F32)
    pad_scr[SUBLANES:SUBLANES + t] = raw_ref[0].astype(F32)
    half = MLSTM_CONV // 2
    for g in range(t // LANES):
        acc = jnp.zeros((LANES, LANES), F32) + bias
        for j in range(MLSTM_CONV):
            off = SUBLANES + g * LANES + j - half
            acc = acc + pad_scr[off:off + LANES] * w[j:j + 1]
        emit(g, acc * jax.nn.sigmoid(acc))


def _mlstm_prologue(q_ref, k_ref, v_ref, cwq_ref, cwk_ref, cbq_ref, cbk_ref,
                    pad_scr, q_scr, kt_scr, vx_scr, t, g0):
    def emit_q(g, blk):
        q_scr[g0 + g] = blk.astype(BF16)

    def emit_k(g, blk):
        kt_scr[g0 + g] = (blk * (MLSTM_DHEAD ** -0.5)).T

    _conv_silu(q_ref, cwq_ref[0], cbq_ref[0], pad_scr, t, emit_q)
    _conv_silu(k_ref, cwk_ref[0], cbk_ref[0], pad_scr, t, emit_k)
    lane = lax.broadcasted_iota(jnp.int32, (LANES, LANES), 1)
    ones_col = jnp.where(lane == 0, 1.0, 0.0).astype(BF16)
    for g in range(t // LANES):
        vx_scr[g0 + g, :, 0:LANES] = v_ref[0, g * LANES:(g + 1) * LANES, :]
        vx_scr[g0 + g, :, LANES:2 * LANES] = ones_col


def _groups(q_scr, kt_scr, vx_scr, c_scr, m_st, items, bwd, out_scr):
    d = 1 if bwd else 0
    order = (1, 0) if bwd else (0, 1)
    lane = lax.broadcasted_iota(jnp.int32, (1, LANES), 1)
    in_c = (lane < CHUNK, lane >= CHUNK)
    li = lax.broadcasted_iota(jnp.int32, (LANES, LANES), 0)
    si = lax.broadcasted_iota(jnp.int32, (LANES, LANES), 1)
    same = (li < CHUNK) == (si < CHUNK)
    mask = same & ((si >= li) if bwd else (si <= li))
    row0 = lax.broadcasted_iota(jnp.int32, (LANES, 1), 0) < CHUNK

    st = []
    for g, rt, og in items:
        r = rt[d:d + 1, :]
        cum = rt[2 + d:3 + d, :]
        totrow = rt[4 + d:5 + d, :]
        maxr = [jnp.max(jnp.where(in_c[c], r, NEG_INF), axis=1, keepdims=True) for c in (0, 1)]
        tot = [jnp.max(jnp.where(in_c[c], totrow, NEG_INF), axis=1, keepdims=True) for c in (0, 1)]
        m_b, mu_f = [None, None], [None, None]
        for c in order:
            m_b[c] = m_st
            mu_f[c] = jnp.maximum(m_st, maxr[c])
            m_st = tot[c] + mu_f[c]
        st.append(dict(g=g, og=og, r=r, cum=cum, m_b=m_b, mu_f=mu_f))
    for e in st:
        kt = kt_scr[e["g"]]
        vx = vx_scr[e["g"]]
        e["kt"], e["vx"] = kt, vx
        e["kv"] = [_dot((kt * jnp.where(in_c[c], jnp.exp(e["r"] - e["mu_f"][c]), 0.0)).astype(BF16), vx)
                   for c in (0, 1)]
        if out_scr is not None:
            e["q"] = q_scr[e["g"]]
            e["sc"] = _dot(e["q"], kt.astype(BF16))
    c_st = c_scr[...]
    for e in st:
        e["c_bf"] = [None, None]
        for c in order:
            e["c_bf"][c] = c_st.astype(BF16)
            c_st = jnp.exp(e["m_b"][c] - e["mu_f"][c]) * c_st + e["kv"][c]
    c_scr[...] = c_st
    if out_scr is not None:
        for e in st:
            m_col = jnp.where(row0, e["m_b"][0], e["m_b"][1])
            rm = jnp.where(mask, e["r"], NEG_INF)
            mu = jnp.maximum(m_col, jnp.max(rm, axis=1, keepdims=True))
            e["mu"], e["m_col"] = mu, m_col
            e["s"] = (e["sc"] * jnp.exp(rm - mu)).astype(BF16)
        for e in st:
            e["intra"] = _dot(e["s"], e["vx"])
            e["inter"] = [_dot(e["q"][c * CHUNK:(c + 1) * CHUNK], e["c_bf"][c]) for c in (0, 1)]
        for e in st:
            inter = jnp.concatenate(e["inter"], axis=0)
            tot_o = e["intra"] + jnp.exp(e["m_col"] - e["mu"]) * inter
            num = tot_o[:, 0:LANES]
            den = tot_o[:, LANES:LANES + 1]
            cum_col = jnp.sum(jnp.where(li == si, e["cum"], 0.0), axis=1, keepdims=True)
            floor = jnp.exp(-(cum_col + e["mu"]))
            out_scr[e["og"]] = num / jnp.maximum(jnp.abs(den), floor)
    return m_st


def _mlstm_kernel(ql_ref, kl_ref, vl_ref, ol_ref, qc_ref, kc_ref, vc_ref,
                  al_ref, bl_ref, ac_ref, bc_ref, ba_ref, bb_ref,
                  cwq_ref, cwk_ref, cbq_ref, cbk_ref, gn_ref, o_ref,
                  pad_scr, q_scr, kt_scr, vx_scr, rtc_scr, rtl_scr, cf_scr, cb_scr, hf_scr, hb_scr,
                  *, t, tc, gpi):
    _gate_rows(ac_ref.at[0], bc_ref.at[0], ba_ref, bb_ref, rtc_scr)
    _gate_rows(al_ref.at[0], bl_ref.at[0], ba_ref, bb_ref, rtl_scr)
    _mlstm_prologue(qc_ref, kc_ref, vc_ref, cwq_ref, cwk_ref, cbq_ref, cbk_ref,
                    pad_scr, q_scr, kt_scr, vx_scr, tc, 0)
    _mlstm_prologue(ql_ref, kl_ref, vl_ref, cwq_ref, cwk_ref, cbq_ref, cbk_ref,
                    pad_scr, q_scr, kt_scr, vx_scr, t, tc // LANES)
    cf_scr[...] = jnp.zeros_like(cf_scr)
    cb_scr[...] = jnp.zeros_like(cb_scr)
    m0 = jnp.zeros((1, 1), F32)

    def group(rt_scr, ng, g0, outs, gpi):
        def body(it, carry):
            mf, mb = carry
            items_f, items_b = [], []
            for u in range(gpi):
                gf = it * gpi + u
                gb = ng - 1 - gf
                items_f.append((g0 + gf, rt_scr[gf], gf))
                items_b.append((g0 + gb, rt_scr[gb], gb))
            args = (q_scr, kt_scr, vx_scr)
            mf = _groups(*args, cf_scr, mf, items_f, False, outs and hf_scr)
            mb = _groups(*args, cb_scr, mb, items_b, True, outs and hb_scr)
            return mf, mb
        return body

    ngc = tc // LANES
    ngl = t // LANES
    gc = math.gcd(ngc, gpi)
    gl = math.gcd(ngl, gpi)
    carry = lax.fori_loop(0, ngc // gc, group(rtc_scr, ngc, 0, None, gc), (m0, m0))
    lax.fori_loop(0, ngl // gl, group(rtl_scr, ngl, ngc, True, gl), carry)

    hm = (hf_scr[...] + hb_scr[...]).reshape(t, LANES)
    ms = jnp.mean(hm * hm, axis=1, keepdims=True)
    out = hm * lax.rsqrt(ms + RMS_EPS) * gn_ref[...] * jax.nn.sigmoid(ol_ref[0].astype(F32))
    o_ref[0] = out.astype(BF16)


def _mlstm(proj_l, proj_c, ga_l, gb_l, ga_c, gb_c, bias_a, bias_b, conv_w, conv_b, gn):
    b, t, _ = proj_l.shape
    tc = proj_c.shape[1]
    h = MLSTM_HEADS
    col = lambda off: (lambda bi, hi: (bi, 0, off + hi))
    q0 = (3 * DIFF_W) // LANES
    gspec = lambda ng: pl.BlockSpec((1, 1, ng, SUBLANES, LANES), lambda bi, hi: (bi, hi, 0, 0, 0))
    hspec = pl.BlockSpec((1, SUBLANES, LANES), lambda bi, hi: (hi, 0, 0))
    cw = jnp.pad(conv_w, ((0, SUBLANES - MLSTM_CONV), (0, 0)))
    cwspec = lambda off: pl.BlockSpec((1, SUBLANES, LANES), lambda bi, hi: (0, 0, off + hi))
    cbspec = lambda off: pl.BlockSpec((1, 1, LANES), lambda bi, hi: (0, 0, off + hi))
    tt = t + tc
    return pl.pallas_call(
        functools.partial(_mlstm_kernel, t=t, tc=tc, gpi=MLSTM_GROUPS_PER_STEP),
        grid=(b, h),
        in_specs=[pl.BlockSpec((1, t, LANES), col(q0)),
                  pl.BlockSpec((1, t, LANES), col(q0 + h)),
                  pl.BlockSpec((1, t, LANES), col(q0 + 2 * h)),
                  pl.BlockSpec((1, t, LANES), col(q0 + 3 * h)),
                  pl.BlockSpec((1, tc, LANES), col(q0)),
                  pl.BlockSpec((1, tc, LANES), col(q0 + h)),
                  pl.BlockSpec((1, tc, LANES), col(q0 + 2 * h)),
                  gspec(t // LANES), gspec(t // LANES), gspec(tc // LANES), gspec(tc // LANES),
                  hspec, hspec,
                  cwspec(0), cwspec(h), cbspec(0), cbspec(h),
                  pl.BlockSpec((1, LANES), lambda bi, hi: (0, 0))],
        out_specs=pl.BlockSpec((1, t, LANES), lambda bi, hi: (bi, 0, hi)),
        out_shape=jax.ShapeDtypeStruct((b, t, ML_W), BF16),
        scratch_shapes=[pltpu.VMEM((t + 2 * SUBLANES, LANES), F32),
                        pltpu.VMEM((tt // LANES, LANES, LANES), BF16),
                        pltpu.VMEM((tt // LANES, LANES, LANES), F32),
                        pltpu.VMEM((tt // LANES, LANES, 2 * LANES), BF16),
                        pltpu.VMEM((tc // LANES, SUBLANES, LANES), F32),
                        pltpu.VMEM((t // LANES, SUBLANES, LANES), F32),
                        pltpu.VMEM((LANES, 2 * LANES), F32),
                        pltpu.VMEM((LANES, 2 * LANES), F32),
                        pltpu.VMEM((t // LANES, LANES, LANES), F32),
                        pltpu.VMEM((t // LANES, LANES, LANES), F32)],
        compiler_params=_cparams(("arbitrary", "arbitrary")),
        name="mlstm",
    )(proj_l, proj_l, proj_l, proj_l, proj_c, proj_c, proj_c,
      ga_l, gb_l, ga_c, gb_c, bias_a, bias_b,
      cw.reshape(1, SUBLANES, 2 * ML_W), cw.reshape(1, SUBLANES, 2 * ML_W),
      conv_b.reshape(1, 1, 2 * ML_W), conv_b.reshape(1, 1, 2 * ML_W), gn)


def _layer_norm(v, g, b):
    mu = jnp.mean(v, axis=1, keepdims=True)
    c = v - mu
    var = jnp.mean(c * c, axis=1, keepdims=True)
    return c * lax.rsqrt(var + LN_EPS) * g + b


def _outproj_kernel(d_ref, m_ref, w_ref, x_ref, g1_ref, lg_ref, lb_ref, sh_ref, sc_ref,
                    x1_ref, h2_ref):
    y = _dot(d_ref[0], w_ref[0]) + _dot(m_ref[0], w_ref[1])
    v = DEEPNORM_ALPHA * x_ref[0] + g1_ref[0] * y
    x1 = _layer_norm(v, lg_ref[...], lb_ref[...])
    x1_ref[0] = x1
    h2_ref[0] = (x1 * (1.0 + sc_ref[0]) + sh_ref[0]).astype(BF16)


def _out_proj(d_lat, m_lat, w_out2, x, g1, ln_g, ln_b, sh2, sc2, *, tm):
    b, t, d = x.shape
    hw = d_lat.shape[2]
    row = pl.BlockSpec((1, 1, d), lambda bi, i: (bi, 0, 0))
    vec = pl.BlockSpec((1, d), lambda bi, i: (0, 0))
    return pl.pallas_call(
        _outproj_kernel,
        grid=(b, t // tm),
        in_specs=[pl.BlockSpec((1, tm, hw), lambda bi, i: (bi, i, 0)),
                  pl.BlockSpec((1, tm, hw), lambda bi, i: (bi, i, 0)),
                  pl.BlockSpec((2, hw, d), lambda bi, i: (0, 0, 0)),
                  pl.BlockSpec((1, tm, d), lambda bi, i: (bi, i, 0)),
                  row, vec, vec, row, row],
        out_specs=[pl.BlockSpec((1, tm, d), lambda bi, i: (bi, i, 0)),
                   pl.BlockSpec((1, tm, d), lambda bi, i: (bi, i, 0))],
        out_shape=[jax.ShapeDtypeStruct((b, t, d), F32),
                   jax.ShapeDtypeStruct((b, t, d), BF16)],
        compiler_params=_cparams(("arbitrary", "arbitrary")),
        name="out_proj",
    )(d_lat, m_lat, w_out2, x, g1, ln_g, ln_b, sh2, sc2)


def _top_rows(x, k, dst_scr):
    cur = x
    for r in range(k):
        m = jnp.max(cur, axis=0, keepdims=True)
        dst_scr[r:r + 1] = m
        if r + 1 < k:
            cur = jnp.where(cur == m, NEG_INF, cur)


def _sort16_pairs():
    def merge(lo, hi, r):
        step = r * 2
        if step < hi - lo:
            yield from merge(lo, hi, step)
            yield from merge(lo + r, hi, step)
            yield from [(i, i + r) for i in range(lo + r, hi - r, step)]
        else:
            yield (lo, lo + r)

    def sort(lo, hi):
        if hi - lo >= 1:
            mid = lo + (hi - lo) // 2
            yield from sort(lo, mid)
            yield from sort(mid + 1, hi)
            yield from merge(lo, hi, 1)

    return list(sort(0, 15))


def _top16_of_128(x, dst_scr):
    k = PEER_TOPK
    y = [x[r * SUBLANES:(r + 1) * SUBLANES] for r in range(k)]
    for a, b in _sort16_pairs():
        y[a], y[b] = jnp.maximum(y[a], y[b]), jnp.minimum(y[a], y[b])
    for r in range(k):
        m = jnp.max(y[0], axis=0, keepdims=True)
        dst_scr[r:r + 1] = m
        if r + 1 < k:
            hit = y[0] == m
            for i in range(k - 1 - r):
                y[i] = jnp.where(hit, y[i + 1], y[i])


def _rank_among(x, v):
    row = lambda i: v[i:i + 1]
    pick = jnp.where
    c1 = x < row(7)
    c2 = x < pick(c1, row(11), row(3))
    c3 = x < pick(c1, pick(c2, row(13), row(9)), pick(c2, row(5), row(1)))
    c4 = x < pick(c1, pick(c2, pick(c3, row(14), row(12)), pick(c3, row(10), row(8))),
                  pick(c2, pick(c3, row(6), row(4)), pick(c3, row(2), row(0))))
    c5 = x < row(15)
    bit = lambda c, w: jnp.where(c, float(w), 0.0)
    return bit(c1, 8) + bit(c2, 4) + bit(c3, 2) + bit(c4, 1) + bit(c5, 1)


def _peer_prep_kernel(h_ref, wq_ref, k1_ref, k2_ref, n1_ref, r2_ref, e1_ref, e2_ref,
                      v1_scr, v2_scr, tp_scr):
    q = _dot(h_ref[...], wq_ref[...])
    half = PEER_DQ // 2
    for h in range(PEER_HEADS):
        q1 = q[:, h * PEER_DQ:h * PEER_DQ + half].astype(BF16)
        q2 = q[:, h * PEER_DQ + half:(h + 1) * PEER_DQ].astype(BF16)
        s1 = _dot_nt(k1_ref[h], q1)
        s2 = _dot_nt(k2_ref[h], q2)
        _top16_of_128(s1, v1_scr)
        _top16_of_128(s2, v2_scr)
        v1 = v1_scr[...]
        v2 = v2_scr[...]
        pieces = [v1[0:1] + v2]
        pieces += [v1[a:a + 1] + v2[0:SUBLANES] for a in range(1, SUBLANES)]
        pieces += [v1[SUBLANES:PEER_TOPK] + v2[0:1]]
        cand = jnp.concatenate(pieces, axis=0)
        _top_rows(cand, PEER_TOPK, tp_scr)
        tp = tp_scr[...]
        top0 = tp[0:1]
        tau = tp[PEER_TOPK - 1:PEER_TOPK]
        z = jnp.sum(jnp.exp(tp - top0), axis=0, keepdims=True)
        cnt = jnp.zeros(s1.shape, F32)
        for bb in range(PEER_TOPK):
            n_a = jnp.sum(jnp.where(v1[bb:bb + 1] + v2 >= tau, 1.0, 0.0), axis=0, keepdims=True)
            cnt = jnp.where(s1 == v1[bb:bb + 1], n_a, cnt)
        n1_ref[h] = cnt
        r2_ref[h] = _rank_among(s2, v2).astype(BF16)
        e1_ref[h] = jnp.exp(s1 - v1[0:1]) / z
        e2_ref[h] = jnp.exp(s2 - v2[0:1]).astype(BF16)


def _peer_prep(h2, wq, k1, k2, *, tt):
    n, d = h2.shape
    hp = PEER_HEADS
    big = jax.ShapeDtypeStruct((hp, PEER_NKEYS, n), F32)
    big16 = jax.ShapeDtypeStruct((hp, PEER_NKEYS, n), BF16)
    bspec = pl.BlockSpec((hp, PEER_NKEYS, tt), lambda i: (0, 0, i))
    kspec = pl.BlockSpec((hp, PEER_NKEYS, PEER_DQ // 2), lambda i: (0, 0, 0))
    return pl.pallas_call(
        _peer_prep_kernel,
        grid=(n // tt,),
        in_specs=[pl.BlockSpec((tt, d), lambda i: (i, 0)),
                  pl.BlockSpec((d, hp * PEER_DQ), lambda i: (0, 0)),
                  kspec, kspec],
        out_specs=[bspec, bspec, bspec, bspec],
        out_shape=[big, big16, big, big16],
        scratch_shapes=[pltpu.VMEM((PEER_TOPK, tt), F32),
                        pltpu.VMEM((PEER_TOPK, tt), F32),
                        pltpu.VMEM((PEER_TOPK, tt), F32)],
        compiler_params=_cparams(("arbitrary",)),
        name="peer_prep",
    )(h2, wq, k1, k2)


def _transpose_cast_kernel(v_ref, o_ref):
    o_ref[...] = v_ref[...].T.astype(BF16)


def _transpose_cast(v, *, te):
    ne, d = v.shape
    return pl.pallas_call(
        _transpose_cast_kernel,
        grid=(ne // te,),
        in_specs=[pl.BlockSpec((te, d), lambda j: (j, 0))],
        out_specs=pl.BlockSpec((d, te), lambda j: (0, j)),
        out_shape=jax.ShapeDtypeStruct((d, ne), BF16),
        compiler_params=_cparams(("arbitrary",)),
        name="peer_v_layout",
    )(v)


def _peer_dense_kernel(h_ref, u_ref, u0_ref, vt_ref, n1_ref, r2_ref, e1_ref, e2_ref, x1_ref, g2_ref, lg_ref,
                       lb_ref, o_ref, ht_scr, z_scr, acc_scr, *, te):
    j = pl.program_id(1)
    nj = pl.num_programs(1)
    nk = PEER_NKEYS

    @pl.when(j == 0)
    def _():
        ht_scr[...] = h_ref[...].astype(F32).T.astype(BF16)
        acc_scr[...] = jnp.zeros_like(acc_scr)
        z_scr[0] = _dot(u0_ref[...], ht_scr[...])

    tt = ht_scr.shape[1]
    tn = TILES["peer_token_slice"]

    def body(wslot, rslot):
        for n0 in range(0, tt, tn):
            z_scr[wslot, :, n0:n0 + tn] = _dot(u_ref[...], ht_scr[:, n0:n0 + tn])
            ws = []
            for a in range(te // nk):
                i1 = j * (te // nk) + a
                g = None
                for h in range(PEER_HEADS):
                    n1row = n1_ref[h, pl.ds(i1, 1), n0:n0 + tn].astype(BF16)
                    e1row = e1_ref[h, pl.ds(i1, 1), n0:n0 + tn].astype(BF16)
                    hit = r2_ref[h, :, n0:n0 + tn] < n1row
                    term = jnp.where(hit, e2_ref[h, :, n0:n0 + tn], jnp.zeros((), BF16)) * e1row
                    g = term if g is None else g + term
                z = z_scr[rslot, a * nk:(a + 1) * nk, n0:n0 + tn]
                gelu = 0.5 * z * (1.0 + lax.erf(z * math.sqrt(0.5)))
                ws.append(gelu.astype(BF16) * g)
            acc_scr[:, n0:n0 + tn] += _dot(vt_ref[...], jnp.concatenate(ws, axis=0))

    @pl.when(j % 2 == 0)
    def _():
        body(1, 0)

    @pl.when(j % 2 == 1)
    def _():
        body(0, 1)

    @pl.when(j == nj - 1)
    def _():
        v = DEEPNORM_ALPHA * x1_ref[...] + g2_ref[0] * acc_scr[...].T
        o_ref[...] = _layer_norm(v, lg_ref[...], lb_ref[...])


def _peer_dense(h2, u_bf, vt_bf, n1, r2, e1, e2, x1, g2, ln_g, ln_b, *, tt, te):
    n, d = h2.shape
    ne = u_bf.shape[0]
    hp = PEER_HEADS
    nb = ne // te
    blocks_per_sample = n // g2.shape[0] // tt
    bspec = pl.BlockSpec((hp, PEER_NKEYS, tt), lambda i, j: (0, 0, i))
    vec = pl.BlockSpec((1, d), lambda i, j: (0, 0))
    return pl.pallas_call(
        functools.partial(_peer_dense_kernel, te=te),
        grid=(n // tt, nb),
        in_specs=[pl.BlockSpec((tt, d), lambda i, j: (i, 0)),
                  pl.BlockSpec((te, d), lambda i, j: (jnp.minimum(j + 1, nb - 1), 0)),
                  pl.BlockSpec((te, d), lambda i, j: (0, 0), pipeline_mode=pl.Buffered(1)),
                  pl.BlockSpec((d, te), lambda i, j: (0, j)),
                  bspec, bspec, bspec, bspec,
                  pl.BlockSpec((tt, d), lambda i, j: (i, 0)),
                  pl.BlockSpec((1, 1, d), lambda i, j: (i // blocks_per_sample, 0, 0)),
                  vec, vec],
        out_specs=pl.BlockSpec((tt, d), lambda i, j: (i, 0)),
        out_shape=jax.ShapeDtypeStruct((n, d), F32),
        scratch_shapes=[pltpu.VMEM((d, tt), BF16),
                        pltpu.VMEM((2, te, tt), F32),
                        pltpu.VMEM((d, tt), F32)],
        compiler_params=_cparams(("arbitrary", "arbitrary")),
        name="peer_dense",
    )(h2, u_bf, u_bf, vt_bf, n1, r2, e1, e2, x1, g2, ln_g, ln_b)


def _rope_tables(t):
    pos = jnp.arange(t, dtype=jnp.int32)
    row = (pos // GRID_W).astype(F32)
    col = (pos % GRID_W).astype(F32)
    inv = ROPE_BASE ** (-jnp.arange(ROPE_NF, dtype=F32) / ROPE_NF)
    lane = jnp.arange(LANES)
    use_col = ((lane // (2 * ROPE_NF)) % 2) == 1
    p = jnp.where(use_col[None, :], col[:, None], row[:, None])
    ang = p * inv[lane % ROPE_NF][None, :]
    sign = jnp.where((lane % (2 * ROPE_NF)) < ROPE_NF, -1.0, 1.0).astype(F32)
    return jnp.cos(ang), jnp.sin(ang) * sign[None, :]


def _gate_layout(g):
    b, _, t = g.shape
    gt = g[:, :N_GATES * MLSTM_HEADS, :].reshape(b, N_GATES, MLSTM_HEADS, t // LANES, LANES)
    gt = gt.transpose(0, 2, 3, 1, 4)
    zeros = jnp.zeros_like(gt[:, :, :, 0:1])
    a = jnp.concatenate([gt[:, :, :, 0:1], gt[:, :, :, 2:3]] + [zeros] * 6, axis=3)
    fb = jnp.concatenate([gt[:, :, :, 1:2], gt[:, :, :, 3:4]], axis=3)
    return a, jnp.tile(fb, (1, 1, 1, 4, 1))


def _gate_bias(gate_b):
    gb = gate_b.astype(F32)
    z = jnp.zeros_like(gb[0])
    a = jnp.stack([gb[0], gb[2]] + [z] * 6, axis=1)
    bm = jnp.stack([gb[1], gb[3]] * 4, axis=1)
    bc = lambda v: jnp.broadcast_to(v[:, :, None], (MLSTM_HEADS, SUBLANES, LANES))
    return bc(a), bc(bm)


def kernel(x, c, ctx, c_ctx, w_ada, b_ada, w_in, conv_w, conv_b, gate_b, diff_lambda, diff_norm_g,
           mlstm_norm_g, w_out, ln1_g, ln1_b, ln2_g, ln2_b, peer_wq, peer_keys, peer_u, peer_v):
    b, t, d = x.shape
    tc = ctx.shape[1]
    l = 0

    c_rows = jnp.concatenate([c, c_ctx[None, :]], axis=0)
    mod = _adaln(c_rows, w_ada[l], b_ada[l])
    sh1, sc1, g1, sh2, sc2, g2 = [m[:, None, :] for m in jnp.split(mod[:b], 6, axis=1)]
    csh1, csc1 = [m[:, None, :] for m in jnp.split(mod[b:b + 1], 6, axis=1)[:2]]

    w = w_in[l]
    col_scale = jnp.where(jnp.arange(MAIN_COLS) < DIFF_W, DIFF_HALF ** -0.5, 1.0).astype(F32)
    w_main = (w[:, :MAIN_COLS] * col_scale[None, :]).astype(BF16)
    wg = jnp.pad(w[:, MAIN_COLS:], ((0, 0), (0, LANES - N_GATES * MLSTM_HEADS)))
    wg_hi = wg.astype(BF16)
    wg_lo = (wg - wg_hi.astype(F32)).astype(BF16)
    wg2 = jnp.concatenate([wg_hi, wg_lo], axis=1)
    cos_t, sin_t = _rope_tables(t)
    proj_l, gates_l = _in_proj(x, sh1, sc1, w_main, wg2, cos_t, sin_t, tm=min(TILES["in_proj_rows"], t))
    nc = b * tc
    proj_c, gates_c = _in_proj(ctx.reshape(1, nc, d), csh1, csc1, w_main, wg2,
                               jnp.ones((nc, LANES), F32), jnp.zeros((nc, LANES), F32), tm=nc)
    proj_c = proj_c.reshape(b, tc, MAIN_COLS)
    gates_c = gates_c.reshape(LANES, b, tc).transpose(1, 0, 2)

    lam_pad = jnp.pad(diff_lambda[l].astype(F32), ((0, SUBLANES - 4), (0, LANES - DIFF_HALF)))
    d_lat = _diff_attn(proj_l, proj_c, lam_pad, diff_norm_g[l].reshape(1, LANES),
                       tq=min(TILES["attn_q_rows"], t), tk=min(TILES["attn_kv_rows"], t),
                       rq=TILES["attn_row_group"])

    ga_l, gb_l = _gate_layout(gates_l)
    ga_c, gb_c = _gate_layout(gates_c)
    bias_a, bias_b = _gate_bias(gate_b[l])
    m_lat = _mlstm(proj_l, proj_c, ga_l, gb_l, ga_c, gb_c, bias_a, bias_b,
                   conv_w[l], conv_b[l], mlstm_norm_g[l].reshape(1, LANES))

    w_out2 = w_out[l].astype(BF16).reshape(2, DIFF_W, d)
    x1, h2 = _out_proj(d_lat, m_lat, w_out2, x, g1, ln1_g[l].reshape(1, d), ln1_b[l].reshape(1, d),
                       sh2, sc2, tm=min(TILES["out_proj_rows"], t))

    n = b * t
    h2f = h2.reshape(n, d)
    keys = peer_keys[l].astype(BF16)
    tt = min(TILES["peer_tokens"], t)
    n1, r2, e1, e2 = _peer_prep(h2f, peer_wq[l].astype(BF16), keys[0], keys[1], tt=tt)
    vt_bf = _transpose_cast(peer_v[l], te=TILES["peer_experts"])
    out = _peer_dense(h2f, peer_u[l].astype(BF16), vt_bf, n1, r2, e1, e2,
                      x1.reshape(n, d), g2, ln2_g[l].reshape(1, d), ln2_b[l].reshape(1, d),
                      tt=tt, te=TILES["peer_experts"])
    return out.reshape(b, t, d)
```
